```python
import math
import jax, jax.numpy as jnp
from jax import lax
import numpy as np

D_MODEL = 1024
BATCH = 8
SEQ = 8192
DEPTH = 1
DEC_BATCH = 128
DEC_SEQ = 8
PAST_LEN = 8192
PAGE_SIZE = 128

HEAD_DIM = 64
SSD_HEADS = 8
SSD_WIDTH = SSD_HEADS * HEAD_DIM
SSD_GROUPS = 2
SSD_HPG = SSD_HEADS // SSD_GROUPS
D_STATE = 128
CONV_WIDTH = 4
CONV_CH = SSD_WIDTH + 2 * SSD_GROUPS * D_STATE
SSD_CHUNK = 256
DT_MIN = 0.001
DT_MAX = 0.1
ATT_HEADS = 8
ATT_WIDTH = ATT_HEADS * HEAD_DIM
KV_HEADS = 2
GQA_GROUP = ATT_HEADS // KV_HEADS
KV_WIDTH = KV_HEADS * HEAD_DIM
CMP_BLOCK = 32
SEL_BLOCK = 64
TOP_N = 16
N_LOCAL_BLOCKS = 2
WINDOW = 512
QUERY_BLOCK = 128
N_BRANCH = 3
ROT_DIM = HEAD_DIM // 4
ROPE_THETA = 500000.0
MIX_WIDTH = SSD_WIDTH + ATT_WIDTH
COL_SIZES = (SSD_WIDTH, CONV_CH, SSD_HEADS, ATT_WIDTH, 2 * KV_WIDTH, 2 * KV_WIDTH, 2 * KV_WIDTH, N_BRANCH * ATT_HEADS, ATT_WIDTH)
IN_COLS = SSD_WIDTH + CONV_CH + SSD_HEADS + ATT_WIDTH + 6 * KV_WIDTH + N_BRANCH * ATT_HEADS + ATT_WIDTH
EPS = 1e-6

kernel_name = 'hybrid_ssd_nsa_adaln_step'


def rms_norm(x, g):
    xf = x.astype(jnp.float32)
    y = xf * lax.rsqrt(jnp.mean(xf * xf, axis=-1, keepdims=True) + EPS)
    return y.astype(x.dtype) * g


def rope(x, pos):
    half = ROT_DIM // 2
    inv_freq = ROPE_THETA ** (-jnp.arange(half, dtype=jnp.float32) * 2.0 / ROT_DIM)
    ang = pos.astype(jnp.float32)[:, None] * inv_freq[None, :]
    cos = jnp.cos(ang)[None, :, None, :].astype(x.dtype)
    sin = jnp.sin(ang)[None, :, None, :].astype(x.dtype)
    x1 = x[..., :half]
    x2 = x[..., half:ROT_DIM]
    return jnp.concatenate([x1 * cos - x2 * sin, x2 * cos + x1 * sin, x[..., ROT_DIM:]], axis=-1)


def masked_softmax(s, mask):
    s = jnp.where(mask, s, -jnp.inf)
    m = jnp.max(s, axis=-1, keepdims=True)
    m = jnp.where(jnp.isfinite(m), m, 0.0)
    e = jnp.exp(s - m)
    d = jnp.sum(e, axis=-1, keepdims=True)
    return e / jnp.where(d > 0, d, 1.0)


def split_cols(u):
    out, o = [], 0
    for n in COL_SIZES:
        out.append(u[..., o:o + n])
        o += n
    return out


def kv_rows(u, pos):
    B, T = u.shape[:2]
    k = rope(u[..., :KV_WIDTH].reshape(B, T, KV_HEADS, HEAD_DIM), pos)
    v = u[..., KV_WIDTH:].reshape(B, T, KV_HEADS, HEAD_DIM)
    return jnp.stack([k, v], axis=2)


def block_means(rows):
    B, L = rows.shape[:2]
    r = rows.reshape(B, L // CMP_BLOCK, CMP_BLOCK, 2, KV_HEADS, HEAD_DIM)
    return jnp.mean(r.astype(jnp.float32), axis=2).astype(rows.dtype)


def nsa_attend(q, q_pos, kv_c, kv_s, kv_w, kw_pos):
    B, Tq = q.shape[:2]
    NC, NS = kv_c.shape[1], kv_s.shape[1]
    scale = HEAD_DIM ** -0.5
    qg = q.reshape(B, Tq, KV_HEADS, GQA_GROUP, HEAD_DIM)
    sc = jnp.einsum('bqkgd,bckd->bqkgc', qg, kv_c[:, :, 0]).astype(jnp.float32) * scale
    c_end = (jnp.arange(NC) + 1) * CMP_BLOCK - 1
    mc = (c_end[None, :] <= q_pos[:, None])[None, :, None, None, :]
    pc = masked_softmax(sc, mc)
    o_c = jnp.einsum('bqkgc,bckd->bqkgd', pc.astype(q.dtype), kv_c[:, :, 1])
    imp = jnp.sum(pc, axis=3).reshape(B, Tq, KV_HEADS, NS, SEL_BLOCK // CMP_BLOCK).sum(-1)
    blk = jnp.arange(NS)[None, :]
    cur = q_pos[:, None] // SEL_BLOCK
    valid = blk <= cur
    forced = (blk == 0) | ((cur - blk >= 0) & (cur - blk < N_LOCAL_BLOCKS))
    score = jnp.where(forced[None, :, None, :], jnp.inf, imp)
    score = jnp.where(valid[None, :, None, :], score, -jnp.inf)
    n_sel = min(TOP_N, NS)
    top_v, top_i = lax.top_k(score, n_sel)
    b_idx = jnp.arange(B)[:, None, None, None]
    h_idx = jnp.arange(KV_HEADS)[None, None, :, None]
    g = kv_s[b_idx, top_i, :, :, h_idx, :]
    ss = jnp.einsum('bqkgd,bqknsd->bqkgns', qg, g[..., 0, :]).astype(jnp.float32) * scale
    tok = top_i[..., None] * SEL_BLOCK + jnp.arange(SEL_BLOCK)
    ms = (top_v > -jnp.inf)[..., None] & (tok <= q_pos[None, :, None, None, None])
    ps = masked_softmax(ss.reshape(B, Tq, KV_HEADS, GQA_GROUP, n_sel * SEL_BLOCK),
                        ms.reshape(B, Tq, KV_HEADS, 1, n_sel * SEL_BLOCK)).reshape(ss.shape)
    o_s = jnp.einsum('bqkgns,bqknsd->bqkgd', ps.astype(q.dtype), g[..., 1, :])
    sw = jnp.einsum('bqkgd,blkd->bqkgl', qg, kv_w[:, :, 0]).astype(jnp.float32) * scale
    dist = q_pos[:, None] - kw_pos[None, :]
    mw = ((dist >= 0) & (dist <= WINDOW) & (kw_pos[None, :] >= 0))[None, :, None, None, :]
    pw = masked_softmax(sw, mw)
    o_w = jnp.einsum('bqkgl,blkd->bqkgd', pw.astype(q.dtype), kv_w[:, :, 1])
    shp = (B, Tq, ATT_HEADS, HEAD_DIM)
    return jnp.stack([o_c.reshape(shp), o_s.reshape(shp), o_w.reshape(shp)], axis=2)


def nsa_prompt(q, kv_c_rows, kv_s_rows, kv_w_rows):
    B, T = q.shape[:2]
    kv_c = block_means(kv_c_rows)
    kv_s = kv_s_rows.reshape(B, T // SEL_BLOCK, SEL_BLOCK, 2, KV_HEADS, HEAD_DIM)
    kv_w_pad = jnp.pad(kv_w_rows, ((0, 0), (WINDOW, 0), (0, 0), (0, 0), (0, 0)))
    qb = min(QUERY_BLOCK, T)

    def one_block(i):
        s0 = i * qb
        q_blk = lax.dynamic_slice_in_dim(q, s0, qb, axis=1)
        q_pos = s0 + jnp.arange(qb)
        kw = lax.dynamic_slice_in_dim(kv_w_pad, s0, qb + WINDOW, axis=1)
        kw_pos = s0 - WINDOW + jnp.arange(qb + WINDOW)
        return nsa_attend(q_blk, q_pos, kv_c, kv_s, kw, kw_pos)

    o = lax.map(one_block, jnp.arange(T // qb))
    return jnp.moveaxis(o, 0, 1).reshape(B, T, N_BRANCH, ATT_HEADS, HEAD_DIM)


def nsa_sample(q, new_c, new_s, new_w, pool_c, pool_s, win_buf, page_table):
    B, Tn = q.shape[:2]
    P = page_table.shape[1] * PAGE_SIZE
    L = P + Tn
    pad = -(-L // SEL_BLOCK) * SEL_BLOCK - L
    padw = ((0, 0), (0, pad), (0, 0), (0, 0), (0, 0))
    past_c = pool_c[page_table].reshape(B, P, 2, KV_HEADS, HEAD_DIM)
    kv_c = jnp.concatenate([block_means(past_c), block_means(jnp.pad(new_c, padw))], axis=1)
    past_s = pool_s[page_table].reshape(B, P // SEL_BLOCK, SEL_BLOCK, 2, KV_HEADS, HEAD_DIM)
    new_sb = jnp.pad(new_s, padw).reshape(B, (Tn + pad) // SEL_BLOCK, SEL_BLOCK, 2, KV_HEADS, HEAD_DIM)
    kv_s = jnp.concatenate([past_s, new_sb], axis=1)
    wb = win_buf.shape[1]
    kv_w = jnp.concatenate([win_buf.astype(new_w.dtype), new_w], axis=1)
    kw_pos = P - wb + jnp.arange(wb + Tn)
    q_pos = P + jnp.arange(Tn)
    o = nsa_attend(q, q_pos, kv_c, kv_s, kv_w, kw_pos)
    new_win = kv_w[:, kv_w.shape[1] - min(WINDOW, L):]
    return o, new_win


def ssd_scan(x, dt, a, bm, cm, h0):
    f32 = jnp.float32
    B, T = x.shape[:2]
    L = min(SSD_CHUNK, T)
    Tp = -(-T // L) * L
    nc = Tp // L

    def padt(t):
        return jnp.pad(t, [(0, 0), (0, Tp - T)] + [(0, 0)] * (t.ndim - 2))

    x = padt(x.astype(f32)).reshape(B, nc, L, SSD_GROUPS, SSD_HPG, HEAD_DIM)
    dt = padt(dt).reshape(B, nc, L, SSD_GROUPS, SSD_HPG)
    bm = padt(bm.astype(f32)).reshape(B, nc, L, SSD_GROUPS, D_STATE)
    cm = padt(cm.astype(f32)).reshape(B, nc, L, SSD_GROUPS, D_STATE)
    la = jnp.cumsum(dt * a, axis=2)
    xdt = x * dt[..., None]
    causal = jnp.tril(jnp.ones((L, L), bool))[None, None, :, :, None, None]
    seg = la[:, :, :, None] - la[:, :, None, :]
    decay = jnp.where(causal, jnp.exp(jnp.where(causal, seg, 0.0)), 0.0)
    cb = jnp.einsum('bclgn,bcsgn->bclsg', cm, bm)
    y_diag = jnp.einsum('bclsg,bclsgr,bcsgrp->bclgrp', cb, decay, xdt)
    to_end = jnp.exp(la[:, :, -1:] - la)
    states = jnp.einsum('bclgn,bclgr,bclgrp->bcgrpn', bm, to_end, xdt)
    chunk_decay = jnp.exp(la[:, :, -1])

    def step(h, inp):
        s, d = inp
        return h * d[..., None, None] + s, h

    h_last, h_in = lax.scan(step, h0.astype(f32), (jnp.moveaxis(states, 1, 0), jnp.moveaxis(chunk_decay, 1, 0)))
    h_in = jnp.moveaxis(h_in, 0, 1)
    y_off = jnp.einsum('bclgn,bcgrpn,bclgr->bclgrp', cm, h_in, jnp.exp(la))
    y = (y_diag + y_off).reshape(B, Tp, SSD_GROUPS, SSD_HPG, HEAD_DIM)[:, :T]
    return y, h_last


def ssd_branch(z, xbc, dt_raw, conv_buf, h0, conv_w, conv_b, dt_bias, a_log, d_skip, norm_g):
    B, T = xbc.shape[:2]
    f32 = jnp.float32
    xpad = jnp.concatenate([conv_buf.astype(xbc.dtype), xbc], axis=1)
    new_conv = xpad[:, T:]
    conv = lax.conv_general_dilated(xpad, conv_w[:, None, :], (1,), 'VALID',
                                    dimension_numbers=('NWC', 'WIO', 'NWC'),
                                    feature_group_count=CONV_CH) + conv_b
    u = jax.nn.silu(conv)
    xs = u[..., :SSD_WIDTH].reshape(B, T, SSD_GROUPS, SSD_HPG, HEAD_DIM)
    bm = u[..., SSD_WIDTH:SSD_WIDTH + SSD_GROUPS * D_STATE].reshape(B, T, SSD_GROUPS, D_STATE)
    cm = u[..., SSD_WIDTH + SSD_GROUPS * D_STATE:].reshape(B, T, SSD_GROUPS, D_STATE)
    dt = jax.nn.softplus(dt_raw.astype(f32) + dt_bias.astype(f32)).reshape(B, T, SSD_GROUPS, SSD_HPG)
    a = -jnp.exp(a_log.astype(f32)).reshape(SSD_GROUPS, SSD_HPG)
    y, h = ssd_scan(xs, dt, a, bm, cm, h0.reshape(B, SSD_GROUPS, SSD_HPG, HEAD_DIM, D_STATE))
    y = y + d_skip.astype(f32).reshape(SSD_GROUPS, SSD_HPG)[:, :, None] * xs.astype(f32)
    y = y.reshape(B, T, SSD_WIDTH).astype(z.dtype)
    out = rms_norm(y * jax.nn.silu(z), norm_g)
    return out, new_conv, h.reshape(B, SSD_HEADS, HEAD_DIM, D_STATE).astype(h0.dtype)


def layer_inputs(x, c, pos, w_ada, b_ada, norm_g, w_in):
    B, T = x.shape[:2]
    mod = jax.nn.silu(c) @ w_ada + b_ada
    shift, scale, gate = jnp.split(mod, 3, axis=-1)
    h = rms_norm(x, norm_g) * (1.0 + scale[:, None]) + shift[:, None]
    u = h @ w_in
    z_s, xbc, dt_raw, q_raw, u_c, u_s, u_w, g_raw, z_a = split_cols(u)
    q = rope(q_raw.reshape(B, T, ATT_HEADS, HEAD_DIM), pos)
    return gate, z_s, xbc, dt_raw, q, kv_rows(u_c, pos), kv_rows(u_s, pos), kv_rows(u_w, pos), g_raw, z_a


def layer_output(x, gate, y_ssd, o_br, g_raw, z_a, att_norm_g, w_out):
    B, T = x.shape[:2]
    gates = jax.nn.sigmoid(g_raw.astype(jnp.float32)).reshape(B, T, N_BRANCH, ATT_HEADS, 1).astype(x.dtype)
    o = jnp.sum(gates * o_br, axis=2).reshape(B, T, ATT_WIDTH)
    y_att = rms_norm(o * jax.nn.silu(z_a), att_norm_g)
    mix = jnp.concatenate([y_ssd, y_att], axis=-1) @ w_out
    return x + gate[:, None] * mix


def setup_inputs(seed: int = 0) -> dict:
    key = jax.random.key(seed)
    ks = jax.random.split(key, 24)
    f32 = jnp.float32
    n_pages = PAST_LEN // PAGE_SIZE
    n_used = DEC_BATCH * n_pages
    n_pool = n_used + n_used // 4
    win_len = min(WINDOW, PAST_LEN)

    def nrm(k, shape, s=1.0):
        return s * jax.random.normal(k, shape, f32)

    x_prompt = nrm(ks[0], (BATCH, SEQ, D_MODEL))
    x_sample = nrm(ks[1], (DEC_BATCH, DEC_SEQ, D_MODEL))
    cache_cmp_kv = nrm(ks[2], (DEPTH, n_pool, PAGE_SIZE, 2, KV_HEADS, HEAD_DIM))
    cache_slc_kv = nrm(ks[3], (DEPTH, n_pool, PAGE_SIZE, 2, KV_HEADS, HEAD_DIM))
    state_win_kv = nrm(ks[4], (DEPTH, DEC_BATCH, win_len, 2, KV_HEADS, HEAD_DIM))
    state_conv = nrm(ks[5], (DEPTH, DEC_BATCH, CONV_WIDTH - 1, CONV_CH))
    state_ssm = nrm(ks[6], (DEPTH, DEC_BATCH, SSD_HEADS, HEAD_DIM, D_STATE), 0.5)
    page_table = jax.random.permutation(ks[7], n_pool)[:n_used].reshape(DEC_BATCH, n_pages).astype(jnp.int32)
    c_prompt = nrm(ks[8], (BATCH, D_MODEL))
    c_sample = nrm(ks[9], (DEC_BATCH, D_MODEL))
    w_ada = nrm(ks[10], (DEPTH, D_MODEL, 3 * D_MODEL), D_MODEL ** -0.5)
    b_ada = nrm(ks[11], (DEPTH, 3 * D_MODEL), 0.02)
    norm_g = 1.0 + nrm(ks[12], (DEPTH, D_MODEL), 0.02)
    w_in = nrm(ks[13], (DEPTH, D_MODEL, IN_COLS), D_MODEL ** -0.5)
    conv_w = nrm(ks[14], (DEPTH, CONV_WIDTH, CONV_CH), CONV_WIDTH ** -0.5)
    conv_b = nrm(ks[15], (DEPTH, CONV_CH), 0.02)
    dt0 = jnp.exp(jax.random.uniform(ks[16], (DEPTH, SSD_HEADS), f32, math.log(DT_MIN), math.log(DT_MAX)))
    dt_bias = dt0 + jnp.log(-jnp.expm1(-dt0))
    a_log = jnp.log(jax.random.uniform(ks[17], (DEPTH, SSD_HEADS), f32, 1.0, 16.0))
    d_skip = 1.0 + nrm(ks[18], (DEPTH, SSD_HEADS), 0.1)
    ssd_norm_g = 1.0 + nrm(ks[19], (DEPTH, SSD_WIDTH), 0.02)
    att_norm_g = 1.0 + nrm(ks[20], (DEPTH, ATT_WIDTH), 0.02)
    w_out = nrm(ks[21], (DEPTH, MIX_WIDTH, D_MODEL), MIX_WIDTH ** -0.5)
    final_g = 1.0 + nrm(ks[22], (D_MODEL,), 0.02)
    return {'x_prompt': x_prompt, 'x_sample': x_sample,
            'cache_cmp_kv': cache_cmp_kv, 'cache_slc_kv': cache_slc_kv,
            'state_win_kv': state_win_kv, 'state_conv': state_conv, 'state_ssm': state_ssm,
            'page_table': page_table, 'c_prompt': c_prompt, 'c_sample': c_sample,
            'w_ada': w_ada, 'b_ada': b_ada, 'norm_g': norm_g, 'w_in': w_in,
            'conv_w': conv_w, 'conv_b': conv_b, 'dt_bias': dt_bias, 'a_log': a_log, 'd_skip': d_skip,
            'ssd_norm_g': ssd_norm_g, 'att_norm_g': att_norm_g, 'w_out': w_out, 'final_g': final_g}


def reference(x_prompt, x_sample, cache_cmp_kv, cache_slc_kv, state_win_kv, state_conv, state_ssm, page_table,
              c_prompt, c_sample, w_ada, b_ada, norm_g, w_in, conv_w, conv_b, dt_bias, a_log, d_skip,
              ssd_norm_g, att_norm_g, w_out, final_g):
    n_prompt, seq = x_prompt.shape[:2]
    dec_seq = x_sample.shape[1]
    past = page_table.shape[1] * PAGE_SIZE
    pos_p = jnp.arange(seq, dtype=jnp.int32)
    pos_s = past + jnp.arange(dec_seq, dtype=jnp.int32)
    xp, xs = x_prompt, x_sample
    out_p = [[], [], [], [], []]
    out_s = [[], [], [], [], []]
    for l in range(DEPTH):
        ssd_w = (conv_w[l], conv_b[l], dt_bias[l], a_log[l], d_skip[l], ssd_norm_g[l])
        gate, z_s, xbc, dt_raw, q, kv_c, kv_s, kv_w, g_raw, z_a = layer_inputs(
            xp, c_prompt, pos_p, w_ada[l], b_ada[l], norm_g[l], w_in[l])
        conv0 = jnp.zeros((n_prompt, CONV_WIDTH - 1, CONV_CH), xp.dtype)
        h0 = jnp.zeros((n_prompt, SSD_HEADS, HEAD_DIM, D_STATE), xp.dtype)
        y_ssd, conv_new, ssm_new = ssd_branch(z_s, xbc, dt_raw, conv0, h0, *ssd_w)
        o = nsa_prompt(q, kv_c, kv_s, kv_w)
        xp = layer_output(xp, gate, y_ssd, o, g_raw, z_a, att_norm_g[l], w_out[l])
        for lst, v in zip(out_p, (kv_c, kv_s, kv_w[:, seq - min(WINDOW, seq):], conv_new, ssm_new)):
            lst.append(v)
        gate, z_s, xbc, dt_raw, q, kv_c, kv_s, kv_w, g_raw, z_a = layer_inputs(
            xs, c_sample, pos_s, w_ada[l], b_ada[l], norm_g[l], w_in[l])
        y_ssd, conv_new, ssm_new = ssd_branch(z_s, xbc, dt_raw, state_conv[l], state_ssm[l], *ssd_w)
        o, win_new = nsa_sample(q, kv_c, kv_s, kv_w, cache_cmp_kv[l], cache_slc_kv[l], state_win_kv[l], page_table)
        xs = layer_output(xs, gate, y_ssd, o, g_raw, z_a, att_norm_g[l], w_out[l])
        for lst, v in zip(out_s, (kv_c, kv_s, win_new, conv_new, ssm_new)):
            lst.append(v)
    y_prompt = rms_norm(xp, final_g)
    y_sample = rms_norm(xs, final_g)
    sp = [jnp.stack(v) for v in out_p]
    sd = [jnp.stack(v) for v in out_s]
    return (y_prompt, y_sample, sp[0], sp[1], sp[2], sp[3], sp[4], sd[0], sd[1], sd[2], sd[3], sd[4])
```

```python
import functools

import jax
import jax.numpy as jnp
from jax import lax
from jax.experimental import pallas as pl
from jax.experimental.pallas import tpu as pltpu

F32 = jnp.float32
BF16 = jnp.bfloat16

HEAD_DIM = 64
SSD_HEADS = 8
SSD_WIDTH = SSD_HEADS * HEAD_DIM
SSD_GROUPS = 2
D_STATE = 128
CONV_WIDTH = 4
CONV_CH = SSD_WIDTH + 2 * SSD_GROUPS * D_STATE
SSD_CHUNK = 256
ATT_HEADS = 8
ATT_WIDTH = ATT_HEADS * HEAD_DIM
KV_HEADS = 2
GQA_GROUP = ATT_HEADS // KV_HEADS
KV_WIDTH = KV_HEADS * HEAD_DIM
CMP_BLOCK = 32
SEL_BLOCK = 64
TOP_N = 16
N_LOCAL_BLOCKS = 2
WINDOW = 512
N_BRANCH = 3
ROT_DIM = HEAD_DIM // 4
ROPE_THETA = 500000.0
PAGE_SIZE = 128
EPS = 1e-6
COL_SIZES = (SSD_WIDTH, CONV_CH, SSD_HEADS, ATT_WIDTH, 2 * KV_WIDTH, 2 * KV_WIDTH, 2 * KV_WIDTH,
             N_BRANCH * ATT_HEADS, ATT_WIDTH)

LANES = 128
MAX_SEL_LANES = LANES
NEG = -1e30
SCALE = HEAD_DIM ** -0.5
GATE_LANE0 = SSD_HEADS
VMEM_LIMIT = 56 * 1024 * 1024

_O_ZS = 0
_O_XBC = _O_ZS + SSD_WIDTH
_O_Q = _O_XBC + CONV_CH
_O_KC = _O_Q + ATT_WIDTH
_O_KS = _O_KC + 2 * KV_WIDTH
_O_KW = _O_KS + 2 * KV_WIDTH
_O_ZA = _O_KW + 2 * KV_WIDTH
_O_SM = _O_ZA + ATT_WIDTH
_W_COLS = _O_SM + LANES


def _cparams(sem):
    return pltpu.CompilerParams(dimension_semantics=sem, vmem_limit_bytes=VMEM_LIMIT)


def _silu(v):
    return v * jax.nn.sigmoid(v)


def _mod_kernel(c_ref, w_ref, b_ref, o_ref):
    a = _silu(c_ref[...])
    o_ref[...] = jnp.dot(a, w_ref[...], preferred_element_type=F32,
                         precision=lax.Precision.HIGHEST) + b_ref[...]


def _mod(c, w_ada, b_ada):
    n, d = c.shape
    cols = w_ada.shape[1]
    tn = d
    assert cols % tn == 0
    return pl.pallas_call(
        _mod_kernel,
        grid=(cols // tn,),
        in_specs=[pl.BlockSpec((n, d), lambda j: (0, 0)),
                  pl.BlockSpec((d, tn), lambda j: (0, j)),
                  pl.BlockSpec((1, tn), lambda j: (0, j))],
        out_specs=pl.BlockSpec((n, tn), lambda j: (0, j)),
        out_shape=jax.ShapeDtypeStruct((n, cols), F32),
        compiler_params=_cparams(("arbitrary",)),
        name="mod",
    )(c, w_ada, b_ada.reshape(1, cols))


def _rope128(v, rc, ra, rb):
    half = ROT_DIM // 2
    return v * rc + pltpu.roll(v, LANES - half, 1) * ra + pltpu.roll(v, half, 1) * rb


def _inproj_kernel(x_ref, sc_ref, sh_ref, g_ref, w_ref, rc_ref, ra_ref, rb_ref,
                   zs_ref, xbc_ref, q_ref, kvc_ref, kvs_ref, kvw_ref, za_ref, sm_ref, *kcm_refs):
    nbk, tt, d = x_ref.shape
    m = nbk * tt
    x = x_ref[...]
    ms = jnp.mean(x * x, axis=-1, keepdims=True)
    y = (x * lax.rsqrt(ms + EPS)) * g_ref[...].reshape(1, 1, d)
    h = y * (1.0 + sc_ref[...]) + sh_ref[...]
    hb = h.reshape(m, d).astype(BF16)

    def proj(lo, n):
        return jnp.dot(hb, w_ref[:, lo:lo + n], preferred_element_type=F32)

    rc, ra, rb = rc_ref[...], ra_ref[...], rb_ref[...]
    zs_ref[...] = proj(_O_ZS, SSD_WIDTH).reshape(nbk, tt, SSD_WIDTH)
    xbc_ref[...] = proj(_O_XBC, CONV_CH).reshape(nbk, tt, CONV_CH)
    za_ref[...] = proj(_O_ZA, ATT_WIDTH).reshape(nbk, tt, ATT_WIDTH)
    sm_ref[...] = proj(_O_SM, LANES).reshape(nbk, tt, LANES)
    qraw = proj(_O_Q, ATT_WIDTH)
    q = jnp.concatenate([_rope128(qraw[:, LANES * j:LANES * (j + 1)], rc, ra, rb)
                         for j in range(ATT_WIDTH // LANES)], axis=1)
    q_ref[...] = q.reshape(nbk, tt, ATT_WIDTH)
    for off, ref in ((_O_KC, kvc_ref), (_O_KS, kvs_ref), (_O_KW, kvw_ref)):
        u = proj(off, 2 * KV_WIDTH)
        kv = jnp.concatenate([_rope128(u[:, :KV_WIDTH], rc, ra, rb), u[:, KV_WIDTH:]], axis=1)
        ref[...] = kv.reshape(nbk, tt, 2 * KV_WIDTH)
        if kcm_refs and ref is kvc_ref:
            nblk = m // CMP_BLOCK
            means = jnp.sum(kv.reshape(nblk, CMP_BLOCK, 2 * KV_WIDTH), axis=1) * (1.0 / CMP_BLOCK)
            kcm_refs[0][...] = means.reshape(1, nblk, 2 * KV_WIDTH)


def _inproj(x3, scale, shift, norm_g, w_big, rope_tabs, *, nbk, groups_per_mod, tab_blocks, emit_means):
    g_total, tt, d = x3.shape
    m = nbk * tt
    steps = g_total // nbk
    if groups_per_mod is None:
        mod_spec = pl.BlockSpec((nbk, 1, d), lambda g: (g, 0, 0))
    else:
        mod_spec = pl.BlockSpec((1, 1, d), lambda g: (g // groups_per_mod, 0, 0))
    tab_spec = pl.BlockSpec((m, LANES), lambda g: (g % tab_blocks, 0))

    def tok_spec(c):
        return pl.BlockSpec((nbk, tt, c), lambda g: (g, 0, 0))

    widths = (SSD_WIDTH, CONV_CH, ATT_WIDTH, 2 * KV_WIDTH, 2 * KV_WIDTH, 2 * KV_WIDTH, ATT_WIDTH, LANES)
    out_specs = [tok_spec(c) for c in widths]
    out_shape = [jax.ShapeDtypeStruct((g_total, tt, c), F32) for c in widths]
    if emit_means:
        nblk = m // CMP_BLOCK
        out_specs.append(pl.BlockSpec((1, nblk, 2 * KV_WIDTH), lambda g: (g, 0, 0)))
        out_shape.append(jax.ShapeDtypeStruct((steps, nblk, 2 * KV_WIDTH), F32))
    return pl.pallas_call(
        _inproj_kernel,
        grid=(steps,),
        in_specs=[tok_spec(d), mod_spec, mod_spec,
                  pl.BlockSpec((1, d), lambda g: (0, 0)),
                  pl.BlockSpec((d, _W_COLS), lambda g: (0, 0)),
                  tab_spec, tab_spec, tab_spec],
        out_specs=out_specs,
        out_shape=out_shape,
        compiler_params=_cparams(("arbitrary",)),
        name="inproj",
    )(x3, scale, shift, norm_g.reshape(1, d), w_big, *rope_tabs)


def _rope_tables(pos):
    half = ROT_DIM // 2
    inv_freq = ROPE_THETA ** (-jnp.arange(half, dtype=F32) * 2.0 / ROT_DIM)
    ang = pos.astype(F32)[:, None] * inv_freq[None, :]
    cos, sin = jnp.cos(ang), jnp.sin(ang)
    n = pos.shape[0]
    one = jnp.ones((n, HEAD_DIM - ROT_DIM), F32)
    zero_h = jnp.zeros((n, half), F32)
    zero_r = jnp.zeros((n, HEAD_DIM - ROT_DIM), F32)
    rc = jnp.concatenate([cos, cos, one], axis=1)
    ra = jnp.concatenate([-sin, zero_h, zero_r], axis=1)
    rb = jnp.concatenate([zero_h, sin, zero_r], axis=1)
    rep = LANES // HEAD_DIM
    return tuple(jnp.tile(t, (1, rep)) for t in (rc, ra, rb))


def _rearrange_w_in(w_in):
    parts, o = [], 0
    for n in COL_SIZES:
        parts.append(w_in[:, o:o + n])
        o += n
    z_s, xbc, dt, q, kc, ks, kw, g, z_a = parts
    d = w_in.shape[0]
    small = jnp.concatenate([dt, g, jnp.zeros((d, LANES - SSD_HEADS - N_BRANCH * ATT_HEADS), w_in.dtype)], axis=1)
    return jnp.concatenate([z_s, xbc, q, kc, ks, kw, z_a, small], axis=1).astype(BF16)


def _pair_cols(mat, p, shape):
    lane = lax.broadcasted_iota(jnp.int32, shape, 1)
    a = jnp.broadcast_to(mat[:, 2 * p:2 * p + 1], shape)
    b = jnp.broadcast_to(mat[:, 2 * p + 1:2 * p + 2], shape)
    return jnp.where(lane < HEAD_DIM, a, b)


def _ssd_kernel(xbc_ref, sm_ref, smt_ref, zs_ref, conv0_ref, h0_ref, cw_ref, cb_ref, dtb_ref, dtbc_ref,
                alog_ref, alogc_ref, dsk_ref, ng_ref, y_ref, hout_ref, xp_ref, h_ref):
    c = pl.program_id(1)
    nc = pl.num_programs(1)
    L = xbc_ref.shape[1]
    pad = xp_ref.shape[0] - L
    hp = jnp.float32

    @pl.when(c == 0)
    def _():
        xp_ref[0:pad, :] = conv0_ref[0]
        h_ref[...] = h0_ref[0]

    xp_ref[pad:pad + L, :] = xbc_ref[0]
    cw = cw_ref[...]
    conv = cb_ref[...]
    for w in range(CONV_WIDTH):
        o = pad - (CONV_WIDTH - 1) + w
        conv = conv + xp_ref[o:o + L, :] * cw[w:w + 1, :]
    halo = xp_ref[L:L + pad, :]
    xp_ref[0:pad, :] = halo

    u = _silu(conv)
    xs = u[:, :SSD_WIDTH]
    gw = SSD_GROUPS * D_STATE
    bm = u[:, SSD_WIDTH:SSD_WIDTH + gw]
    cm = u[:, SSD_WIDTH + gw:]

    dt = jax.nn.softplus(sm_ref[0] + dtb_ref[...])
    dta = dt * (-jnp.exp(alog_ref[...]))
    row = lax.broadcasted_iota(jnp.int32, (L, L), 0)
    col = lax.broadcasted_iota(jnp.int32, (L, L), 1)
    causal = row >= col
    la = jnp.dot(causal.astype(hp), dta, preferred_element_type=hp,
                 precision=lax.Precision.HIGHEST)
    dtt = jax.nn.softplus(smt_ref[0] + dtbc_ref[...])
    dtat = dtt * (-jnp.exp(alogc_ref[...]))
    lat = jnp.dot(dtat, (row <= col).astype(hp), preferred_element_type=hp,
                  precision=lax.Precision.HIGHEST)
    la_last = la[L - 1:L, :]
    ela = jnp.exp(la)
    te = jnp.exp(la_last - la)
    cdec = jnp.exp(la_last)

    lane = lax.broadcasted_iota(jnp.int32, (L, LANES), 1)
    srow = lax.broadcasted_iota(jnp.int32, (LANES, LANES), 0)
    hpg = SSD_HEADS // SSD_GROUPS
    ys = []
    for g in range(SSD_GROUPS):
        bm_g = bm[:, g * D_STATE:(g + 1) * D_STATE]
        cm_g = cm[:, g * D_STATE:(g + 1) * D_STATE].astype(BF16)
        bm_gb = bm_g.astype(BF16)
        cb = lax.dot_general(cm_g, bm_gb, (((1,), (1,)), ((), ())), preferred_element_type=hp)
        for pp in range(hpg // 2):
            p = g * (hpg // 2) + pp
            xs_p = xs[:, LANES * p:LANES * (p + 1)]
            xdt = xs_p * _pair_cols(dt, p, (L, LANES))
            xdt_b = xdt.astype(BF16)
            yd = []
            for r in (2 * p, 2 * p + 1):
                seg = la[:, r:r + 1] - lat[r:r + 1, :]
                dec = jnp.where(causal, jnp.exp(jnp.where(causal, seg, 0.0)), 0.0)
                yd.append(jnp.dot((cb * dec).astype(BF16), xdt_b, preferred_element_type=hp))
            y_diag = jnp.where(lane < HEAD_DIM, yd[0], yd[1])
            h_p = h_ref[p]
            y_off = lax.dot_general(cm_g, h_p.astype(BF16), (((1,), (1,)), ((), ())),
                                    preferred_element_type=hp) * _pair_cols(ela, p, (L, LANES))
            ys.append(y_diag + y_off + dsk_ref[:, LANES * p:LANES * (p + 1)] * xs_p)
            xw = (xdt * _pair_cols(te, p, (L, LANES))).astype(BF16)
            st = lax.dot_general(xw, bm_gb, (((0,), (0,)), ((), ())), preferred_element_type=hp)
            cd = jnp.where(srow < HEAD_DIM,
                           jnp.broadcast_to(cdec[:, 2 * p:2 * p + 1], (LANES, LANES)),
                           jnp.broadcast_to(cdec[:, 2 * p + 1:2 * p + 2], (LANES, LANES)))
            h_ref[p] = h_p * cd + st

    y = jnp.concatenate(ys, axis=1)
    t = y * _silu(zs_ref[0])
    ms = jnp.mean(t * t, axis=-1, keepdims=True)
    y_ref[0] = ((t * lax.rsqrt(ms + EPS)) * ng_ref[...]).astype(y_ref.dtype)

    @pl.when(c == nc - 1)
    def _():
        hout_ref[0] = h_ref[...]


def _ssd(xbc, sm, zs, conv_state, h0, conv_w, conv_b, dt_bias, a_log, d_skip, norm_g):
    bn, t, _ = xbc.shape
    L = min(SSD_CHUNK, t)
    assert t % L == 0
    nc = t // L
    pad = 8
    smt = jnp.swapaxes(sm, 1, 2)
    conv0 = jnp.pad(conv_state, ((0, 0), (pad - (CONV_WIDTH - 1), 0), (0, 0)))
    hp2 = h0.reshape(bn, SSD_HEADS // 2, 2 * HEAD_DIM, D_STATE)
    zpad = jnp.zeros((LANES - SSD_HEADS,), F32)
    dtb = jnp.concatenate([dt_bias.astype(F32), zpad])
    alog = jnp.concatenate([a_log.astype(F32), zpad])
    dsk = jnp.repeat(d_skip.astype(F32), HEAD_DIM).reshape(1, SSD_WIDTH)

    def full(shape):
        return pl.BlockSpec(shape, lambda b, c: tuple(0 for _ in shape))

    y, hout = pl.pallas_call(
        _ssd_kernel,
        grid=(bn, nc),
        in_specs=[pl.BlockSpec((1, L, CONV_CH), lambda b, c: (b, c, 0)),
                  pl.BlockSpec((1, L, LANES), lambda b, c: (b, c, 0)),
                  pl.BlockSpec((1, LANES, L), lambda b, c: (b, 0, c)),
                  pl.BlockSpec((1, L, SSD_WIDTH), lambda b, c: (b, c, 0)),
                  pl.BlockSpec((1, pad, CONV_CH), lambda b, c: (b, 0, 0)),
                  pl.BlockSpec((1, SSD_HEADS // 2, 2 * HEAD_DIM, D_STATE), lambda b, c: (b, 0, 0, 0)),
                  full((CONV_WIDTH, CONV_CH)), full((1, CONV_CH)),
                  full((1, LANES)), full((LANES, 1)), full((1, LANES)), full((LANES, 1)),
                  full((1, SSD_WIDTH)), full((1, SSD_WIDTH))],
        out_specs=[pl.BlockSpec((1, L, SSD_WIDTH), lambda b, c: (b, c, 0)),
                   pl.BlockSpec((1, SSD_HEADS // 2, 2 * HEAD_DIM, D_STATE), lambda b, c: (b, 0, 0, 0))],
        out_shape=[jax.ShapeDtypeStruct((bn, t, SSD_WIDTH), BF16),
                   jax.ShapeDtypeStruct((bn, SSD_HEADS // 2, 2 * HEAD_DIM, D_STATE), F32)],
        scratch_shapes=[pltpu.VMEM((L + pad, CONV_CH), F32),
                        pltpu.VMEM((SSD_HEADS // 2, 2 * HEAD_DIM, D_STATE), F32)],
        compiler_params=_cparams(("arbitrary", "arbitrary")),
        name="ssd",
    )(xbc, sm, smt, zs, conv0, hp2, conv_w, conv_b.reshape(1, CONV_CH),
      dtb.reshape(1, LANES), dtb.reshape(LANES, 1), alog.reshape(1, LANES), alog.reshape(LANES, 1),
      dsk, norm_g.reshape(1, SSD_WIDTH))
    return y, hout.reshape(bn, SSD_HEADS, HEAD_DIM, D_STATE)


def _half_mask(shape, kvh):
    lane = lax.broadcasted_iota(jnp.int32, shape, 1)
    return (lane >= HEAD_DIM) if kvh else (lane < HEAD_DIM)


def _stack_heads(q, kvh):
    tq = q.shape[0]
    keep = _half_mask((tq, LANES), kvh)
    blocks = []
    for g in range(GQA_GROUP):
        h = kvh * GQA_GROUP + g
        v = q[:, LANES * (h // 2):LANES * (h // 2 + 1)]
        if (h % 2) != kvh:
            v = pltpu.roll(v, HEAD_DIM, 1)
        blocks.append(jnp.where(keep, v, 0.0))
    return jnp.concatenate(blocks, axis=0)


def _unstack_heads(o_by_kvh, gates, branch):
    tq = gates.shape[0]
    lane = lax.broadcasted_iota(jnp.int32, (tq, LANES), 1)
    blocks = []
    for h in range(ATT_HEADS):
        kvh, g = divmod(h, GQA_GROUP)
        v = o_by_kvh[kvh][g * tq:(g + 1) * tq, :]
        gl = GATE_LANE0 + branch * ATT_HEADS + h
        v = v * gates[:, gl:gl + 1]
        if (h % 2) != kvh:
            v = pltpu.roll(v, HEAD_DIM, 1)
        blocks.append(v)
    outs = [jnp.where(lane < HEAD_DIM, blocks[2 * j], blocks[2 * j + 1]) for j in range(ATT_HEADS // 2)]
    return jnp.concatenate(outs, axis=1)


def _tile_rows(v, n):
    return jnp.concatenate([v] * n, axis=0)


def _flash_update(s, vaug, m_ref, acc_ref, kvh):
    m_prev = m_ref[kvh]
    m_new = jnp.maximum(m_prev, jnp.max(s, axis=1, keepdims=True))
    alpha = jnp.exp(m_prev - m_new)
    p = jnp.exp(s - m_new[:, 0:1])
    pv = jnp.dot(p.astype(BF16), vaug, preferred_element_type=F32)
    acc_ref[kvh] = acc_ref[kvh] * jnp.concatenate([alpha, alpha], axis=1) + pv
    m_ref[kvh] = m_new


def _flash_out(acc_ref, kvh):
    acc = acc_ref[kvh]
    return acc[:, :LANES] / acc[:, LANES:]


def _vaug(v01):
    return jnp.concatenate([v01.astype(BF16), jnp.ones(v01.shape, BF16)], axis=1)


def _topk_cols(score_t, rounds):
    nb = score_t.shape[0]
    ridx = lax.broadcasted_iota(jnp.int32, score_t.shape, 0)
    sel = jnp.zeros(score_t.shape, F32)
    cur = score_t
    for _ in range(rounds):
        mx = jnp.max(cur, axis=0, keepdims=True)
        idx = jnp.min(jnp.where(cur == mx, ridx, nb), axis=0, keepdims=True)
        hit = ridx == idx
        sel = jnp.where(hit & (mx > -jnp.inf), 1.0, sel)
        cur = jnp.where(hit, -jnp.inf, cur)
    return sel


def _topk_rows(score, rounds):
    nb = score.shape[1]
    lidx = lax.broadcasted_iota(jnp.int32, score.shape, 1)
    sel = jnp.zeros(score.shape, F32)
    cur = score
    for _ in range(rounds):
        mx = jnp.max(cur, axis=1, keepdims=True)
        idx = jnp.min(jnp.where(cur == mx, lidx, nb), axis=1, keepdims=True)
        hit = lidx == idx
        sel = jnp.where(hit & (mx > -jnp.inf), 1.0, sel)
        cur = jnp.where(hit, -jnp.inf, cur)
    return sel


def _cmp_attend(q, kc01, vc01, pos):
    tq = q.shape[0]
    ncl = kc01.shape[0]
    rows = GQA_GROUP * tq
    lane = lax.broadcasted_iota(jnp.int32, (rows, ncl), 1)
    cblk = 2 * (lane % MAX_SEL_LANES) + lane // MAX_SEL_LANES
    c_end = (cblk + 1) * CMP_BLOCK - 1
    mask = c_end <= _tile_rows(pos, GQA_GROUP)
    kcb = kc01.astype(BF16)
    vcb = vc01.astype(BF16)
    outs, imps = [], []
    for kvh in range(KV_HEADS):
        qs = _stack_heads(q, kvh).astype(BF16)
        s = lax.dot_general(qs, kcb, (((1,), (1,)), ((), ())), preferred_element_type=F32)
        s = jnp.where(mask, s, -jnp.inf)
        mx = jnp.max(s, axis=1, keepdims=True)
        mx = jnp.where(mx > -jnp.inf, mx, 0.0)
        e = jnp.exp(s - mx)
        dsum = jnp.sum(e, axis=1, keepdims=True)
        p = e / jnp.where(dsum > 0, dsum, 1.0)
        outs.append(jnp.dot(p.astype(BF16), vcb, preferred_element_type=F32))
        imp = p[0:tq]
        for g in range(1, GQA_GROUP):
            imp = imp + p[g * tq:(g + 1) * tq]
        imps.append(imp[:, :MAX_SEL_LANES] + imp[:, MAX_SEL_LANES:])
    return outs, imps


def _cmp_prompt_kernel(q_ref, kc_ref, vc_ref, sm_ref, o_ref, bias_ref):
    tq = q_ref.shape[1]
    i = pl.program_id(1)
    pos = i * tq + lax.broadcasted_iota(jnp.int32, (tq, 1), 0)
    q = q_ref[0] * SCALE
    outs, imps = _cmp_attend(q, kc_ref[0], vc_ref[0], pos)
    gates = jax.nn.sigmoid(sm_ref[0])
    o_ref[0] = _unstack_heads(outs, gates, 0)
    blk = lax.broadcasted_iota(jnp.int32, (tq, MAX_SEL_LANES), 1)
    cur = pos // SEL_BLOCK
    valid = blk <= cur
    forced = (blk == 0) | ((cur - blk >= 0) & (cur - blk < N_LOCAL_BLOCKS))
    for kvh in range(KV_HEADS):
        score = jnp.where(forced, jnp.inf, imps[kvh])
        score = jnp.where(valid, score, -jnp.inf)
        sel = _topk_cols(score.T, TOP_N).T
        bias_ref[0, kvh] = jnp.where(sel > 0.5, 0.0, NEG).astype(BF16)


def _cmp_prompt(q, kcm_perm, sm, tq):
    b, t, _ = q.shape
    ncl = kcm_perm.shape[1]
    return pl.pallas_call(
        _cmp_prompt_kernel,
        grid=(b, t // tq),
        in_specs=[pl.BlockSpec((1, tq, ATT_WIDTH), lambda bb, i: (bb, i, 0)),
                  pl.BlockSpec((1, ncl, LANES), lambda bb, i: (bb, 0, 0)),
                  pl.BlockSpec((1, ncl, LANES), lambda bb, i: (bb, 0, 1)),
                  pl.BlockSpec((1, tq, LANES), lambda bb, i: (bb, i, 0))],
        out_specs=[pl.BlockSpec((1, tq, ATT_WIDTH), lambda bb, i: (bb, i, 0)),
                   pl.BlockSpec((1, KV_HEADS, tq, LANES), lambda bb, i: (bb, 0, i, 0))],
        out_shape=[jax.ShapeDtypeStruct((b, t, ATT_WIDTH), F32),
                   jax.ShapeDtypeStruct((b, KV_HEADS, t, LANES), BF16)],
        compiler_params=_cparams(("arbitrary", "arbitrary")),
        name="cmp_prompt",
    )(q, kcm_perm, kcm_perm, sm)


def _build_qaug(q, bias_ref, qaug_ref):
    for kvh in range(KV_HEADS):
        qs = _stack_heads(q, kvh).astype(BF16)
        bias = _tile_rows(bias_ref[0, kvh], GQA_GROUP)
        qaug_ref[kvh] = jnp.concatenate([qs, bias], axis=1)


def _slc_prompt_kernel(q_ref, bias_ref, k_ref, v_ref, oh_ref, sm_ref, o_ref, qaug_ref, m_ref, acc_ref):
    tq = q_ref.shape[1]
    tk = k_ref.shape[1]
    i = pl.program_id(1)
    j = pl.program_id(2)
    nk = pl.num_programs(2)
    rows = GQA_GROUP * tq
    last = (i * tq + tq - 1) // tk

    @pl.when(j == 0)
    def _():
        _build_qaug(q_ref[0] * SCALE, bias_ref, qaug_ref)
        m_ref[...] = jnp.full(m_ref.shape, NEG, F32)
        acc_ref[...] = jnp.zeros(acc_ref.shape, F32)

    def step(masked):
        kaug = jnp.concatenate([k_ref[0].astype(BF16), oh_ref[...]], axis=1)
        vaug = _vaug(v_ref[0])
        if masked:
            kpos = j * tk + lax.broadcasted_iota(jnp.int32, (rows, tk), 1)
            qpos = i * tq + lax.broadcasted_iota(jnp.int32, (rows, tk), 0) % tq
            ok = kpos <= qpos
        for kvh in range(KV_HEADS):
            s = lax.dot_general(qaug_ref[kvh], kaug, (((1,), (1,)), ((), ())), preferred_element_type=F32)
            if masked:
                s = jnp.where(ok, s, NEG)
            _flash_update(s, vaug, m_ref, acc_ref, kvh)

    crosses = (j * tk + tk - 1) > (i * tq)

    @pl.when((j <= last) & crosses)
    def _():
        step(True)

    @pl.when((j <= last) & jnp.logical_not(crosses))
    def _():
        step(False)

    @pl.when(j == nk - 1)
    def _():
        outs = [_flash_out(acc_ref, kvh) for kvh in range(KV_HEADS)]
        o_ref[0] = _unstack_heads(outs, jax.nn.sigmoid(sm_ref[0]), 1)


def _block_onehot(n_keys):
    blk = jnp.arange(n_keys, dtype=jnp.int32) // SEL_BLOCK
    return (blk[:, None] == jnp.arange(MAX_SEL_LANES, dtype=jnp.int32)[None, :]).astype(BF16)


def _slc_prompt(q, bias, kvs, sm, tq, tk):
    b, t, _ = q.shape
    nq, nk = t // tq, t // tk
    rows = GQA_GROUP * tq

    def kv_idx(half):
        def f(bb, i, j):
            return (bb, jnp.minimum(j, (i * tq + tq - 1) // tk), half)
        return f

    return pl.pallas_call(
        _slc_prompt_kernel,
        grid=(b, nq, nk),
        in_specs=[pl.BlockSpec((1, tq, ATT_WIDTH), lambda bb, i, j: (bb, i, 0)),
                  pl.BlockSpec((1, KV_HEADS, tq, LANES), lambda bb, i, j: (bb, 0, i, 0)),
                  pl.BlockSpec((1, tk, LANES), kv_idx(0)),
                  pl.BlockSpec((1, tk, LANES), kv_idx(1)),
                  pl.BlockSpec((tk, LANES), lambda bb, i, j: (jnp.minimum(j, (i * tq + tq - 1) // tk), 0)),
                  pl.BlockSpec((1, tq, LANES), lambda bb, i, j: (bb, i, 0))],
        out_specs=pl.BlockSpec((1, tq, ATT_WIDTH), lambda bb, i, j: (bb, i, 0)),
        out_shape=jax.ShapeDtypeStruct((b, t, ATT_WIDTH), F32),
        scratch_shapes=[pltpu.VMEM((KV_HEADS, rows, 2 * LANES), BF16),
                        pltpu.VMEM((KV_HEADS, rows, LANES), F32),
                        pltpu.VMEM((KV_HEADS, rows, 2 * LANES), F32)],
        compiler_params=_cparams(("arbitrary", "arbitrary", "arbitrary")),
        name="slc_prompt",
    )(q, bias, kvs, kvs, _block_onehot(t), sm)


def _win_prompt_kernel(q_ref, k_ref, v_ref, sm_ref, o_ref, qs_ref, m_ref, acc_ref):
    tq = q_ref.shape[1]
    tk = k_ref.shape[1]
    i = pl.program_id(1)
    j = pl.program_id(2)
    nk = pl.num_programs(2)
    rows = GQA_GROUP * tq
    tile = i - (nk - 1) + j

    @pl.when(j == 0)
    def _():
        q = q_ref[0] * SCALE
        for kvh in range(KV_HEADS):
            qs_ref[kvh] = _stack_heads(q, kvh).astype(BF16)
        m_ref[...] = jnp.full(m_ref.shape, NEG, F32)
        acc_ref[...] = jnp.zeros(acc_ref.shape, F32)

    @pl.when(tile >= 0)
    def _():
        kb = k_ref[0].astype(BF16)
        vaug = _vaug(v_ref[0])
        kpos = tile * tk + lax.broadcasted_iota(jnp.int32, (rows, tk), 1)
        qpos = i * tq + lax.broadcasted_iota(jnp.int32, (rows, tk), 0) % tq
        dist = qpos - kpos
        ok = (dist >= 0) & (dist <= WINDOW)
        for kvh in range(KV_HEADS):
            s = lax.dot_general(qs_ref[kvh], kb, (((1,), (1,)), ((), ())), preferred_element_type=F32)
            _flash_update(jnp.where(ok, s, NEG), vaug, m_ref, acc_ref, kvh)

    @pl.when(j == nk - 1)
    def _():
        outs = [_flash_out(acc_ref, kvh) for kvh in range(KV_HEADS)]
        o_ref[0] = _unstack_heads(outs, jax.nn.sigmoid(sm_ref[0]), 2)


def _win_prompt(q, kvw, sm, tq):
    b, t, _ = q.shape
    tk = tq
    nk = WINDOW // tk + 1
    rows = GQA_GROUP * tq

    def kv_idx(half):
        def f(bb, i, j):
            return (bb, jnp.maximum(i - (nk - 1) + j, 0), half)
        return f

    return pl.pallas_call(
        _win_prompt_kernel,
        grid=(b, t // tq, nk),
        in_specs=[pl.BlockSpec((1, tq, ATT_WIDTH), lambda bb, i, j: (bb, i, 0)),
                  pl.BlockSpec((1, tk, LANES), kv_idx(0)),
                  pl.BlockSpec((1, tk, LANES), kv_idx(1)),
                  pl.BlockSpec((1, tq, LANES), lambda bb, i, j: (bb, i, 0))],
        out_specs=pl.BlockSpec((1, tq, ATT_WIDTH), lambda bb, i, j: (bb, i, 0)),
        out_shape=jax.ShapeDtypeStruct((b, t, ATT_WIDTH), F32),
        scratch_shapes=[pltpu.VMEM((KV_HEADS, rows, LANES), BF16),
                        pltpu.VMEM((KV_HEADS, rows, LANES), F32),
                        pltpu.VMEM((KV_HEADS, rows, 2 * LANES), F32)],
        compiler_params=_cparams(("arbitrary", "arbitrary", "arbitrary")),
        name="win_prompt",
    )(q, kvw, kvw, sm)


def _cmp_sample_kernel(pt_ref, *refs, n_pg, past):
    page_refs = refs[:n_pg]
    q_ref, sm_ref, o_ref, bias_ref, kce_ref, kco_ref = refs[n_pg:]
    s = pl.program_id(1)
    ns = pl.num_programs(1)
    tn = q_ref.shape[1]

    @pl.when(s == 0)
    def _():
        kce_ref[...] = jnp.zeros(kce_ref.shape, F32)
        kco_ref[...] = jnp.zeros(kco_ref.shape, F32)

    x = jnp.concatenate([r[0] for r in page_refs], axis=0)
    nblk = n_pg * PAGE_SIZE // SEL_BLOCK
    x3 = x.reshape(nblk, SEL_BLOCK, 2 * KV_WIDTH)
    r0 = pl.multiple_of(s * nblk, nblk)
    kce_ref[pl.ds(r0, nblk), :] = jnp.sum(x3[:, :CMP_BLOCK, :], axis=1) * (1.0 / CMP_BLOCK)
    kco_ref[pl.ds(r0, nblk), :] = jnp.sum(x3[:, CMP_BLOCK:, :], axis=1) * (1.0 / CMP_BLOCK)

    @pl.when(s == ns - 1)
    def _():
        kc = jnp.concatenate([kce_ref[...], kco_ref[...]], axis=0)
        pos = past + lax.broadcasted_iota(jnp.int32, (tn, 1), 0)
        outs, imps = _cmp_attend(q_ref[0] * SCALE, kc[:, :LANES], kc[:, LANES:], pos)
        o_ref[0] = _unstack_heads(outs, jax.nn.sigmoid(sm_ref[0]), 0)
        n_past = past // SEL_BLOCK
        blk = lax.broadcasted_iota(jnp.int32, (tn, MAX_SEL_LANES), 1)
        forced = (blk == 0) | (blk == n_past - 1)
        rounds = min(TOP_N, n_past + 1) - 1
        for kvh in range(KV_HEADS):
            score = jnp.where(forced, jnp.inf, imps[kvh])
            score = jnp.where(blk < n_past, score, -jnp.inf)
            sel = _topk_rows(score, rounds)
            bias_ref[0, kvh] = jnp.where(sel > 0.5, 0.0, NEG).astype(BF16)


def _page_specs(n_pg):
    def spec(k):
        return pl.BlockSpec((1, PAGE_SIZE, 2 * KV_WIDTH), lambda b, s, pt: (pt[b, s * n_pg + k], 0, 0))
    return [spec(k) for k in range(n_pg)]


def _cmp_sample(page_table, pool, q, sm, n_pg):
    bs, tn, _ = q.shape
    n_pages = page_table.shape[1]
    past = n_pages * PAGE_SIZE
    grid_spec = pltpu.PrefetchScalarGridSpec(
        num_scalar_prefetch=1,
        grid=(bs, n_pages // n_pg),
        in_specs=_page_specs(n_pg) + [
            pl.BlockSpec((1, tn, ATT_WIDTH), lambda b, s, pt: (b, 0, 0)),
            pl.BlockSpec((1, tn, LANES), lambda b, s, pt: (b, 0, 0))],
        out_specs=[pl.BlockSpec((1, tn, ATT_WIDTH), lambda b, s, pt: (b, 0, 0)),
                   pl.BlockSpec((1, KV_HEADS, tn, LANES), lambda b, s, pt: (b, 0, 0, 0))],
        scratch_shapes=[pltpu.VMEM((MAX_SEL_LANES, 2 * KV_WIDTH), F32),
                        pltpu.VMEM((MAX_SEL_LANES, 2 * KV_WIDTH), F32)],
    )
    return pl.pallas_call(
        functools.partial(_cmp_sample_kernel, n_pg=n_pg, past=past),
        grid_spec=grid_spec,
        out_shape=[jax.ShapeDtypeStruct((bs, tn, ATT_WIDTH), F32),
                   jax.ShapeDtypeStruct((bs, KV_HEADS, tn, LANES), BF16)],
        compiler_params=_cparams(("arbitrary", "arbitrary")),
        name="cmp_sample",
    )(page_table, *([pool] * n_pg), q, sm)


def _slc_sample_kernel(pt_ref, *refs, n_pg, past):
    page_refs = refs[:n_pg]
    q_ref, bias_ref, new_ref, sm_ref, o_ref, qaug_ref, m_ref, acc_ref = refs[n_pg:]
    s = pl.program_id(1)
    ns = pl.num_programs(1)
    tn = q_ref.shape[1]
    rows = GQA_GROUP * tn

    @pl.when(s == 0)
    def _():
        _build_qaug(q_ref[0] * SCALE, bias_ref, qaug_ref)
        m_ref[...] = jnp.full(m_ref.shape, NEG, F32)
        acc_ref[...] = jnp.zeros(acc_ref.shape, F32)

    x = jnp.concatenate([r[0] for r in page_refs], axis=0)
    tk = n_pg * PAGE_SIZE
    kblk = (s * tk + lax.broadcasted_iota(jnp.int32, (tk, MAX_SEL_LANES), 0)) // SEL_BLOCK
    onehot = jnp.where(kblk == lax.broadcasted_iota(jnp.int32, (tk, MAX_SEL_LANES), 1), 1.0, 0.0).astype(BF16)
    kaug = jnp.concatenate([x[:, :LANES].astype(BF16), onehot], axis=1)
    vaug = _vaug(x[:, LANES:])
    for kvh in range(KV_HEADS):
        sc = lax.dot_general(qaug_ref[kvh], kaug, (((1,), (1,)), ((), ())), preferred_element_type=F32)
        _flash_update(sc, vaug, m_ref, acc_ref, kvh)

    @pl.when(s == ns - 1)
    def _():
        xn = new_ref[0]
        nk = xn.shape[0]
        kn = xn[:, :LANES].astype(BF16)
        vn = _vaug(xn[:, LANES:])
        kidx = lax.broadcasted_iota(jnp.int32, (rows, nk), 1)
        qidx = lax.broadcasted_iota(jnp.int32, (rows, nk), 0) % tn
        ok = kidx <= qidx
        for kvh in range(KV_HEADS):
            sc = lax.dot_general(qaug_ref[kvh, :, :LANES], kn, (((1,), (1,)), ((), ())),
                                 preferred_element_type=F32)
            _flash_update(jnp.where(ok, sc, NEG), vn, m_ref, acc_ref, kvh)
        outs = [_flash_out(acc_ref, kvh) for kvh in range(KV_HEADS)]
        o_ref[0] = _unstack_heads(outs, jax.nn.sigmoid(sm_ref[0]), 1)


def _slc_sample(page_table, pool, q, bias, new_pad, sm, n_pg):
    bs, tn, _ = q.shape
    n_pages = page_table.shape[1]
    past = n_pages * PAGE_SIZE
    rows = GQA_GROUP * tn
    npad = new_pad.shape[1]
    grid_spec = pltpu.PrefetchScalarGridSpec(
        num_scalar_prefetch=1,
        grid=(bs, n_pages // n_pg),
        in_specs=_page_specs(n_pg) + [
            pl.BlockSpec((1, tn, ATT_WIDTH), lambda b, s, pt: (b, 0, 0)),
            pl.BlockSpec((1, KV_HEADS, tn, LANES), lambda b, s, pt: (b, 0, 0, 0)),
            pl.BlockSpec((1, npad, 2 * KV_WIDTH), lambda b, s, pt: (b, 0, 0)),
            pl.BlockSpec((1, tn, LANES), lambda b, s, pt: (b, 0, 0))],
        out_specs=pl.BlockSpec((1, tn, ATT_WIDTH), lambda b, s, pt: (b, 0, 0)),
        scratch_shapes=[pltpu.VMEM((KV_HEADS, rows, 2 * LANES), BF16),
                        pltpu.VMEM((KV_HEADS, rows, LANES), F32),
                        pltpu.VMEM((KV_HEADS, rows, 2 * LANES), F32)],
    )
    return pl.pallas_call(
        functools.partial(_slc_sample_kernel, n_pg=n_pg, past=past),
        grid_spec=grid_spec,
        out_shape=jax.ShapeDtypeStruct((bs, tn, ATT_WIDTH), F32),
        compiler_params=_cparams(("arbitrary", "arbitrary")),
        name="slc_sample",
    )(page_table, *([pool] * n_pg), q, bias, new_pad, sm)


def _win_sample_kernel(q_ref, win_ref, new_ref, sm_ref, o_ref, *, past):
    tn = q_ref.shape[1]
    rows = GQA_GROUP * tn
    wb = win_ref.shape[1]
    kv = jnp.concatenate([win_ref[0], new_ref[0]], axis=0)
    nk = kv.shape[0]
    kb = kv[:, :LANES].astype(BF16)
    vb = kv[:, LANES:].astype(BF16)
    kidx = lax.broadcasted_iota(jnp.int32, (rows, nk), 1)
    qidx = lax.broadcasted_iota(jnp.int32, (rows, nk), 0) % tn
    kpos = past - wb + kidx
    dist = past + qidx - kpos
    ok = (dist >= 0) & (dist <= WINDOW) & (kpos >= 0) & (kidx < wb + tn)
    q = q_ref[0] * SCALE
    outs = []
    for kvh in range(KV_HEADS):
        qs = _stack_heads(q, kvh).astype(BF16)
        s = lax.dot_general(qs, kb, (((1,), (1,)), ((), ())), preferred_element_type=F32)
        s = jnp.where(ok, s, -jnp.inf)
        mx = jnp.max(s, axis=1, keepdims=True)
        e = jnp.exp(s - mx)
        p = e / jnp.sum(e, axis=1, keepdims=True)
        outs.append(jnp.dot(p.astype(BF16), vb, preferred_element_type=F32))
    o_ref[0] = _unstack_heads(outs, jax.nn.sigmoid(sm_ref[0]), 2)


def _win_sample(q, win_buf, new_pad, sm, past):
    bs, tn, _ = q.shape
    wb = win_buf.shape[1]
    npad = new_pad.shape[1]
    return pl.pallas_call(
        functools.partial(_win_sample_kernel, past=past),
        grid=(bs,),
        in_specs=[pl.BlockSpec((1, tn, ATT_WIDTH), lambda b: (b, 0, 0)),
                  pl.BlockSpec((1, wb, 2 * KV_WIDTH), lambda b: (b, 0, 0)),
                  pl.BlockSpec((1, npad, 2 * KV_WIDTH), lambda b: (b, 0, 0)),
                  pl.BlockSpec((1, tn, LANES), lambda b: (b, 0, 0))],
        out_specs=pl.BlockSpec((1, tn, ATT_WIDTH), lambda b: (b, 0, 0)),
        out_shape=jax.ShapeDtypeStruct((bs, tn, ATT_WIDTH), F32),
        compiler_params=_cparams(("arbitrary",)),
        name="win_sample",
    )(q, win_buf, new_pad, sm)


def _out_kernel(x_ref, oc_ref, os_ref, ow_ref, za_ref, ys_ref, gate_ref, ang_ref, wo_ref, fg_ref, y_ref):
    nbk, tt, d = x_ref.shape
    m = nbk * tt
    o = (oc_ref[...] + os_ref[...]) + ow_ref[...]
    t = o * _silu(za_ref[...])
    ms = jnp.mean(t * t, axis=-1, keepdims=True)
    y_att = (t * lax.rsqrt(ms + EPS)) * ang_ref[...].reshape(1, 1, ATT_WIDTH)
    ya = y_att.reshape(m, ATT_WIDTH).astype(BF16)
    ys = ys_ref[...].reshape(m, SSD_WIDTH)
    mix = (jnp.dot(ys, wo_ref[:SSD_WIDTH, :], preferred_element_type=F32)
           + jnp.dot(ya, wo_ref[SSD_WIDTH:, :], preferred_element_type=F32))
    xp = x_ref[...] + gate_ref[...] * mix.reshape(nbk, tt, d)
    ms2 = jnp.mean(xp * xp, axis=-1, keepdims=True)
    y_ref[...] = (xp * lax.rsqrt(ms2 + EPS)) * fg_ref[...].reshape(1, 1, d)


def _out(x3, o_c, o_s, o_w, za, y_ssd, gate, att_norm_g, w_out_b, final_g, *, nbk, groups_per_mod):
    g_total, tt, d = x3.shape
    steps = g_total // nbk
    if groups_per_mod is None:
        mod_spec = pl.BlockSpec((nbk, 1, d), lambda g: (g, 0, 0))
    else:
        mod_spec = pl.BlockSpec((1, 1, d), lambda g: (g // groups_per_mod, 0, 0))

    def tok_spec(c):
        return pl.BlockSpec((nbk, tt, c), lambda g: (g, 0, 0))

    return pl.pallas_call(
        _out_kernel,
        grid=(steps,),
        in_specs=[tok_spec(d), tok_spec(ATT_WIDTH), tok_spec(ATT_WIDTH), tok_spec(ATT_WIDTH),
                  tok_spec(ATT_WIDTH), tok_spec(SSD_WIDTH), mod_spec,
                  pl.BlockSpec((1, ATT_WIDTH), lambda g: (0, 0)),
                  pl.BlockSpec((SSD_WIDTH + ATT_WIDTH, d), lambda g: (0, 0)),
                  pl.BlockSpec((1, d), lambda g: (0, 0))],
        out_specs=tok_spec(d),
        out_shape=jax.ShapeDtypeStruct((g_total, tt, d), F32),
        compiler_params=_cparams(("arbitrary",)),
        name="outproj",
    )(x3, o_c, o_s, o_w, za, y_ssd, gate, att_norm_g.reshape(1, ATT_WIDTH), w_out_b, final_g.reshape(1, d))


def _perm_cmp_means(kcm, b):
    nc = kcm.shape[1]
    ns = nc // 2
    assert ns <= MAX_SEL_LANES
    eo = kcm.reshape(b, ns, 2, 2 * KV_WIDTH).transpose(0, 2, 1, 3)
    eo = jnp.pad(eo, ((0, 0), (0, 0), (0, MAX_SEL_LANES - ns), (0, 0)))
    return eo.reshape(b, 2 * MAX_SEL_LANES, 2 * KV_WIDTH)


def _prompt_layer(x, mod, lw, final_g, apply_final):
    b, t, d = x.shape
    shift, scale, gate = (mod[:, None, i * d:(i + 1) * d] for i in range(3))
    tt = 256
    assert t % tt == 0 and t % SSD_CHUNK == 0 and t >= WINDOW
    gpb = t // tt
    x3 = x.reshape(b * gpb, tt, d)
    tabs = _rope_tables(jnp.arange(t, dtype=jnp.int32))
    zs, xbc, q, kvc, kvs, kvw, za, sm, kcm = _inproj(
        x3, scale, shift, lw["norm_g"], lw["w_big"], tabs,
        nbk=1, groups_per_mod=gpb, tab_blocks=gpb, emit_means=True)
    r = lambda a: a.reshape(b, t, a.shape[-1])
    zs, xbc, q, kvc, kvs, kvw, za, sm = map(r, (zs, xbc, q, kvc, kvs, kvw, za, sm))
    kcm = kcm.reshape(b, t // CMP_BLOCK, 2 * KV_WIDTH)

    conv0 = jnp.zeros((b, CONV_WIDTH - 1, CONV_CH), F32)
    h0 = jnp.zeros((b, SSD_HEADS, HEAD_DIM, D_STATE), F32)
    y_ssd, ssm_new = _ssd(xbc, sm, zs, conv0, h0, lw["conv_w"], lw["conv_b"], lw["dt_bias"], lw["a_log"],
                          lw["d_skip"], lw["ssd_norm_g"])

    tq = 128
    o_c, bias = _cmp_prompt(q, _perm_cmp_means(kcm, b), sm, tq)
    o_s = _slc_prompt(q, bias, kvs, sm, tq, 256)
    o_w = _win_prompt(q, kvw, sm, tq)

    g3 = lambda a: a.reshape(b * gpb, tt, a.shape[-1])
    y3 = _out(x3, g3(o_c), g3(o_s), g3(o_w), g3(za), g3(y_ssd), gate, lw["att_norm_g"], lw["w_out_b"],
              final_g, nbk=1, groups_per_mod=gpb)
    assert apply_final
    kv6 = lambda a: a.reshape(b, a.shape[1], 2, KV_HEADS, HEAD_DIM)
    outs = (kv6(kvc), kv6(kvs), kv6(kvw[:, t - min(WINDOW, t):]), xbc[:, t - (CONV_WIDTH - 1):], ssm_new)
    return y3.reshape(b, t, d), outs


def _sample_layer(x, mod, lw, final_g, pool_c, pool_s, win_buf, conv_buf, ssm, page_table, apply_final):
    bs, tn, d = x.shape
    shift, scale, gate = (mod[:, None, i * d:(i + 1) * d] for i in range(3))
    n_pages = page_table.shape[1]
    past = n_pages * PAGE_SIZE
    nbk = 16
    n_pg = 8
    assert bs % nbk == 0 and tn % 8 == 0 and tn <= SEL_BLOCK and n_pages % n_pg == 0
    assert past // SEL_BLOCK <= MAX_SEL_LANES and past % SEL_BLOCK == 0
    pos = past + jnp.arange(tn, dtype=jnp.int32)
    tabs = tuple(jnp.tile(tb, (nbk, 1)) for tb in _rope_tables(pos))
    zs, xbc, q, kvc, kvs, kvw, za, sm = _inproj(
        x, scale, shift, lw["norm_g"], lw["w_big"], tabs,
        nbk=nbk, groups_per_mod=None, tab_blocks=1, emit_means=False)

    y_ssd, ssm_new = _ssd(xbc, sm, zs, conv_buf, ssm, lw["conv_w"], lw["conv_b"], lw["dt_bias"], lw["a_log"],
                          lw["d_skip"], lw["ssd_norm_g"])

    npad = LANES
    pool_c2 = pool_c.reshape(pool_c.shape[0], PAGE_SIZE, 2 * KV_WIDTH)
    pool_s2 = pool_s.reshape(pool_s.shape[0], PAGE_SIZE, 2 * KV_WIDTH)
    o_c, bias = _cmp_sample(page_table, pool_c2, q, sm, n_pg)
    kvs_pad = jnp.pad(kvs, ((0, 0), (0, npad - tn), (0, 0)))
    o_s = _slc_sample(page_table, pool_s2, q, bias, kvs_pad, sm, n_pg)
    wb = win_buf.shape[1]
    win2 = win_buf.reshape(bs, wb, 2 * KV_WIDTH)
    kvw_pad = jnp.pad(kvw, ((0, 0), (0, npad - tn), (0, 0)))
    o_w = _win_sample(q, win2, kvw_pad, sm, past)

    y = _out(x, o_c, o_s, o_w, za, y_ssd, gate, lw["att_norm_g"], lw["w_out_b"], final_g,
             nbk=nbk, groups_per_mod=None)
    assert apply_final
    kv6 = lambda a: a.reshape(bs, a.shape[1], 2, KV_HEADS, HEAD_DIM)
    kv_w_all = jnp.concatenate([win2, kvw], axis=1)
    win_new = kv_w_all[:, kv_w_all.shape[1] - min(WINDOW, past + tn):]
    conv_new = jnp.concatenate([conv_buf, xbc], axis=1)[:, tn:]
    outs = (kv6(kvc), kv6(kvs), kv6(win_new), conv_new, ssm_new)
    return y, outs


def kernel(x_prompt, x_sample, cache_cmp_kv, cache_slc_kv, state_win_kv, state_conv, state_ssm, page_table,
           c_prompt, c_sample, w_ada, b_ada, norm_g, w_in, conv_w, conv_b, dt_bias, a_log, d_skip,
           ssd_norm_g, att_norm_g, w_out, final_g):
    depth = w_ada.shape[0]
    assert depth == 1
    n_prompt = c_prompt.shape[0]
    xp, xs = x_prompt, x_sample
    out_p, out_s = [], []
    for l in range(depth):
        lw = dict(norm_g=norm_g[l], w_big=_rearrange_w_in(w_in[l]), conv_w=conv_w[l], conv_b=conv_b[l],
                  dt_bias=dt_bias[l], a_log=a_log[l], d_skip=d_skip[l], ssd_norm_g=ssd_norm_g[l],
                  att_norm_g=att_norm_g[l], w_out_b=w_out[l].astype(BF16))
        mod = _mod(jnp.concatenate([c_prompt, c_sample], axis=0), w_ada[l], b_ada[l])
        last = l == depth - 1
        xp, op = _prompt_layer(xp, mod[:n_prompt], lw, final_g, last)
        xs, os_ = _sample_layer(xs, mod[n_prompt:], lw, final_g, cache_cmp_kv[l], cache_slc_kv[l],
                                state_win_kv[l], state_conv[l], state_ssm[l], page_table, last)
        out_p.append(op)
        out_s.append(os_)
    sp = [jnp.stack([o[k] for o in out_p]) for k in range(5)]
    sd = [jnp.stack([o[k] for o in out_s]) for k in range(5)]
    return (xp, xs, sp[0], sp[1], sp[2], sp[3], sp[4], sd[0], sd[1], sd[2], sd[3], sd[4])
```

```python
import functools

import jax
import jax.numpy as jnp
from jax import lax
from jax.experimental import pallas as pl
from jax.experimental.pallas import tpu as pltpu

F32 = jnp.float32
BF16 = jnp.bfloat16

HEAD_DIM = 64
SSD_HEADS = 8
SSD_WIDTH = SSD_HEADS * HEAD_DIM
SSD_GROUPS = 2
D_STATE = 128
CONV_WIDTH = 4
CONV_CH = SSD_WIDTH + 2 * SSD_GROUPS * D_STATE
SSD_CHUNK = 256
ATT_HEADS = 8
ATT_WIDTH = ATT_HEADS * HEAD_DIM
KV_HEADS = 2
GQA_GROUP = ATT_HEADS // KV_HEADS
KV_WIDTH = KV_HEADS * HEAD_DIM
CMP_BLOCK = 32
SEL_BLOCK = 64
TOP_N = 16
N_LOCAL_BLOCKS = 2
WINDOW = 512
N_BRANCH = 3
ROT_DIM = HEAD_DIM // 4
ROPE_THETA = 500000.0
PAGE_SIZE = 128
EPS = 1e-6
COL_SIZES = (SSD_WIDTH, CONV_CH, SSD_HEADS, ATT_WIDTH, 2 * KV_WIDTH, 2 * KV_WIDTH, 2 * KV_WIDTH,
             N_BRANCH * ATT_HEADS, ATT_WIDTH)

LANES = 128
MAX_SEL_LANES = LANES
NEG = -1e30
SCALE = HEAD_DIM ** -0.5
SCALE_LOG2 = SCALE * 1.4426950408889634
GATE_LANE0 = SSD_HEADS
BF16_SUBLANES = 16
VAT_ROWS = HEAD_DIM + BF16_SUBLANES
VMEM_LIMIT = 56 * 1024 * 1024

_O_ZS = 0
_O_XBC = _O_ZS + SSD_WIDTH
_O_Q = _O_XBC + CONV_CH
_O_KC = _O_Q + ATT_WIDTH
_O_KS = _O_KC + 2 * KV_WIDTH
_O_KW = _O_KS + 2 * KV_WIDTH
_O_ZA = _O_KW + 2 * KV_WIDTH
_O_SM = _O_ZA + ATT_WIDTH
_W_COLS = _O_SM + LANES


def _cparams(sem):
    return pltpu.CompilerParams(dimension_semantics=sem, vmem_limit_bytes=VMEM_LIMIT)


def _silu(v):
    return v * jax.nn.sigmoid(v)


def _mod_kernel(c_ref, w_ref, b_ref, o_ref):
    a = _silu(c_ref[...])
    o_ref[...] = jnp.dot(a, w_ref[...], preferred_element_type=F32,
                         precision=lax.Precision.HIGHEST) + b_ref[...]


def _mod(c, w_ada, b_ada):
    n, d = c.shape
    cols = w_ada.shape[1]
    tn = d
    assert cols % tn == 0
    return pl.pallas_call(
        _mod_kernel,
        grid=(cols // tn,),
        in_specs=[pl.BlockSpec((n, d), lambda j: (0, 0)),
                  pl.BlockSpec((d, tn), lambda j: (0, j)),
                  pl.BlockSpec((1, tn), lambda j: (0, j))],
        out_specs=pl.BlockSpec((n, tn), lambda j: (0, j)),
        out_shape=jax.ShapeDtypeStruct((n, cols), F32),
        compiler_params=_cparams(("arbitrary",)),
        name="mod",
    )(c, w_ada, b_ada.reshape(1, cols))


def _rope128(v, rc, ra, rb):
    half = ROT_DIM // 2
    return v * rc + pltpu.roll(v, LANES - half, 1) * ra + pltpu.roll(v, half, 1) * rb


def _modulated_norm(x_ref, sc_ref, sh_ref, g_ref):
    nbk, tt, d = x_ref.shape
    x = x_ref[...]
    ms = jnp.mean(x * x, axis=-1, keepdims=True)
    y = (x * lax.rsqrt(ms + EPS)) * g_ref[...].reshape(1, 1, d)
    h = y * (1.0 + sc_ref[...]) + sh_ref[...]
    return h.reshape(nbk * tt, d).astype(BF16)


def _inproj_common(hb, w_ref, rc, ra, rb, zs_ref, xbc_ref, q_ref, za_ref, sm_ref):
    nbk, tt, _ = zs_ref.shape

    def proj(lo, n):
        return jnp.dot(hb, w_ref[:, lo:lo + n], preferred_element_type=F32)

    zs_ref[...] = proj(_O_ZS, SSD_WIDTH).reshape(nbk, tt, SSD_WIDTH)
    xbc_ref[...] = proj(_O_XBC, CONV_CH).reshape(nbk, tt, CONV_CH)
    za_ref[...] = proj(_O_ZA, ATT_WIDTH).reshape(nbk, tt, ATT_WIDTH)
    sm_ref[...] = proj(_O_SM, LANES).reshape(nbk, tt, LANES)
    qraw = proj(_O_Q, ATT_WIDTH)
    q = jnp.concatenate([_rope128(qraw[:, LANES * j:LANES * (j + 1)], rc, ra, rb)
                         for j in range(ATT_WIDTH // LANES)], axis=1)
    q_ref[...] = q.reshape(nbk, tt, ATT_WIDTH)
    return proj


def _inproj_sample_kernel(x_ref, sc_ref, sh_ref, g_ref, w_ref, rc_ref, ra_ref, rb_ref,
                          zs_ref, xbc_ref, q_ref, za_ref, sm_ref, kvc_ref, kvs_ref, kvw_ref):
    nbk, tt, _ = x_ref.shape
    hb = _modulated_norm(x_ref, sc_ref, sh_ref, g_ref)
    rc, ra, rb = rc_ref[...], ra_ref[...], rb_ref[...]
    proj = _inproj_common(hb, w_ref, rc, ra, rb, zs_ref, xbc_ref, q_ref, za_ref, sm_ref)
    for off, ref in ((_O_KC, kvc_ref), (_O_KS, kvs_ref), (_O_KW, kvw_ref)):
        u = proj(off, 2 * KV_WIDTH)
        kv = jnp.concatenate([_rope128(u[:, :KV_WIDTH], rc, ra, rb), u[:, KV_WIDTH:]], axis=1)
        ref[...] = kv.reshape(nbk, tt, 2 * KV_WIDTH)


def _rope_rows(k, cos, sin):
    half = ROT_DIM // 2
    x1, x2 = k[:half], k[half:ROT_DIM]
    return jnp.concatenate([x1 * cos - x2 * sin, x2 * cos + x1 * sin, k[ROT_DIM:]], axis=0)


def _inproj_prompt_kernel(x_ref, sc_ref, sh_ref, g_ref, w_ref, wt_ref, rc_ref, ra_ref, rb_ref, cos_ref, sin_ref,
                          zs_ref, xbc_ref, q_ref, za_ref, sm_ref, kcm_ref,
                          kvtc_ref, kvts_ref, kvtw_ref, kas_ref, vats_ref, kw_ref, vatw_ref, *, groups_per_seq):
    _, tt, _ = x_ref.shape
    hb = _modulated_norm(x_ref, sc_ref, sh_ref, g_ref)
    rc, ra, rb = rc_ref[...], ra_ref[...], rb_ref[...]
    proj = _inproj_common(hb, w_ref, rc, ra, rb, zs_ref, xbc_ref, q_ref, za_ref, sm_ref)

    u = proj(_O_KC, 2 * KV_WIDTH)
    kv = jnp.concatenate([_rope128(u[:, :KV_WIDTH], rc, ra, rb), u[:, KV_WIDTH:]], axis=1)
    nblk = tt // CMP_BLOCK
    means = jnp.sum(kv.reshape(nblk, CMP_BLOCK, 2 * KV_WIDTH), axis=1) * (1.0 / CMP_BLOCK)
    kcm_ref[...] = means.reshape(1, nblk, 2 * KV_WIDTH)

    ti = pl.program_id(0) % groups_per_seq
    blk = (ti * tt + lax.broadcasted_iota(jnp.int32, (tt, MAX_SEL_LANES), 0)) // SEL_BLOCK
    onehot = jnp.where(blk == lax.broadcasted_iota(jnp.int32, (tt, MAX_SEL_LANES), 1), 1.0, 0.0)
    ks = _rope128(proj(_O_KS, KV_WIDTH), rc, ra, rb)
    kas_ref[0] = jnp.concatenate([onehot, ks], axis=1).astype(BF16)
    kw_ref[0] = _rope128(proj(_O_KW, KV_WIDTH), rc, ra, rb).astype(BF16)

    ut = lax.dot_general(wt_ref[...], hb, (((1,), (1,)), ((), ())), preferred_element_type=F32)
    cos, sin = cos_ref[...], sin_ref[...]
    w2 = 2 * KV_WIDTH
    slabs = []
    for br in range(N_BRANCH):
        s = ut[br * w2:(br + 1) * w2]
        ks = [_rope_rows(s[h * HEAD_DIM:(h + 1) * HEAD_DIM], cos, sin) for h in range(KV_HEADS)]
        slabs.append(jnp.concatenate(ks + [s[KV_WIDTH:]], axis=0))
    kvtc_ref[0] = slabs[0]
    kvts_ref[0] = slabs[1]
    kvtw_ref[0] = slabs[2]
    ones = jnp.ones((VAT_ROWS - HEAD_DIM, tt), F32)
    for slab, vat_ref in ((slabs[1], vats_ref), (slabs[2], vatw_ref)):
        for h in range(KV_HEADS):
            v = slab[KV_WIDTH + h * HEAD_DIM:KV_WIDTH + (h + 1) * HEAD_DIM]
            vat_ref[0, h] = jnp.concatenate([v, ones], axis=0).astype(BF16)


def _mod_spec(nbk, d, groups_per_mod):
    if groups_per_mod is None:
        return pl.BlockSpec((nbk, 1, d), lambda g: (g, 0, 0))
    return pl.BlockSpec((1, 1, d), lambda g: (g // groups_per_mod, 0, 0))


def _inproj_sample(x3, scale, shift, norm_g, w_big, rope_tabs, *, nbk):
    g_total, tt, d = x3.shape
    m = nbk * tt
    mod_spec = _mod_spec(nbk, d, None)
    tab_spec = pl.BlockSpec((m, LANES), lambda g: (0, 0))

    def tok_spec(c):
        return pl.BlockSpec((nbk, tt, c), lambda g: (g, 0, 0))

    widths = (SSD_WIDTH, CONV_CH, ATT_WIDTH, ATT_WIDTH, LANES, 2 * KV_WIDTH, 2 * KV_WIDTH, 2 * KV_WIDTH)
    return pl.pallas_call(
        _inproj_sample_kernel,
        grid=(g_total // nbk,),
        in_specs=[tok_spec(d), mod_spec, mod_spec,
                  pl.BlockSpec((1, d), lambda g: (0, 0)),
                  pl.BlockSpec((d, _W_COLS), lambda g: (0, 0)),
                  tab_spec, tab_spec, tab_spec],
        out_specs=[tok_spec(c) for c in widths],
        out_shape=[jax.ShapeDtypeStruct((g_total, tt, c), F32) for c in widths],
        compiler_params=_cparams(("arbitrary",)),
        name="inproj_sample",
    )(x3, scale, shift, norm_g.reshape(1, d), w_big, *rope_tabs)


def _inproj_prompt(x, scale, shift, norm_g, w_big, w_kvt, tt):
    b, t, d = x.shape
    gps = t // tt
    steps = b * gps
    x3 = x.reshape(steps, tt, d)
    pos = jnp.arange(t, dtype=jnp.int32)
    tabs = _rope_tables(pos)
    cos_t, sin_t = _rope_angles(pos)
    mod_spec = _mod_spec(1, d, gps)
    tab_spec = pl.BlockSpec((tt, LANES), lambda g: (g % gps, 0))
    ang_spec = pl.BlockSpec((ROT_DIM // 2, tt), lambda g: (0, g % gps))

    def tok_spec(c):
        return pl.BlockSpec((1, tt, c), lambda g: (g, 0, 0))

    def row_spec(r):
        return pl.BlockSpec((1, r, tt), lambda g: (g // gps, 0, g % gps))

    vat_spec = pl.BlockSpec((1, KV_HEADS, VAT_ROWS, tt), lambda g: (g // gps, 0, 0, g % gps))
    nblk = tt // CMP_BLOCK
    w2 = 2 * KV_WIDTH
    tok_widths = (SSD_WIDTH, CONV_CH, ATT_WIDTH, ATT_WIDTH, LANES)
    out_specs = ([tok_spec(c) for c in tok_widths]
                 + [pl.BlockSpec((1, nblk, w2), lambda g: (g, 0, 0))]
                 + [row_spec(w2)] * 3
                 + [tok_spec(MAX_SEL_LANES + KV_WIDTH), vat_spec, tok_spec(KV_WIDTH), vat_spec])
    vat_shape = jax.ShapeDtypeStruct((b, KV_HEADS, VAT_ROWS, t), BF16)
    out_shape = ([jax.ShapeDtypeStruct((steps, tt, c), F32) for c in tok_widths]
                 + [jax.ShapeDtypeStruct((steps, nblk, w2), F32)]
                 + [jax.ShapeDtypeStruct((b, w2, t), F32)] * 3
                 + [jax.ShapeDtypeStruct((steps, tt, MAX_SEL_LANES + KV_WIDTH), BF16), vat_shape,
                    jax.ShapeDtypeStruct((steps, tt, KV_WIDTH), BF16), vat_shape])
    return pl.pallas_call(
        functools.partial(_inproj_prompt_kernel, groups_per_seq=gps),
        grid=(steps,),
        in_specs=[tok_spec(d), mod_spec, mod_spec,
                  pl.BlockSpec((1, d), lambda g: (0, 0)),
                  pl.BlockSpec((d, _W_COLS), lambda g: (0, 0)),
                  pl.BlockSpec((N_BRANCH * w2, d), lambda g: (0, 0)),
                  tab_spec, tab_spec, tab_spec, ang_spec, ang_spec],
        out_specs=out_specs,
        out_shape=out_shape,
        compiler_params=_cparams(("arbitrary",)),
        name="inproj_prompt",
    )(x3, scale, shift, norm_g.reshape(1, d), w_big, w_kvt, *tabs, cos_t, sin_t)


def _rope_angles(pos):
    half = ROT_DIM // 2
    inv_freq = ROPE_THETA ** (-jnp.arange(half, dtype=F32) * 2.0 / ROT_DIM)
    ang = pos.astype(F32)[:, None] * inv_freq[None, :]
    return jnp.cos(ang).T, jnp.sin(ang).T


def _rope_tables(pos):
    half = ROT_DIM // 2
    inv_freq = ROPE_THETA ** (-jnp.arange(half, dtype=F32) * 2.0 / ROT_DIM)
    ang = pos.astype(F32)[:, None] * inv_freq[None, :]
    cos, sin = jnp.cos(ang), jnp.sin(ang)
    n = pos.shape[0]
    one = jnp.ones((n, HEAD_DIM - ROT_DIM), F32)
    zero_h = jnp.zeros((n, half), F32)
    zero_r = jnp.zeros((n, HEAD_DIM - ROT_DIM), F32)
    rc = jnp.concatenate([cos, cos, one], axis=1)
    ra = jnp.concatenate([-sin, zero_h, zero_r], axis=1)
    rb = jnp.concatenate([zero_h, sin, zero_r], axis=1)
    rep = LANES // HEAD_DIM
    return tuple(jnp.tile(t, (1, rep)) for t in (rc, ra, rb))


def _rearrange_w_in(w_in):
    parts, o = [], 0
    for n in COL_SIZES:
        parts.append(w_in[:, o:o + n])
        o += n
    z_s, xbc, dt, q, kc, ks, kw, g, z_a = parts
    d = w_in.shape[0]
    small = jnp.concatenate([dt, g, jnp.zeros((d, LANES - SSD_HEADS - N_BRANCH * ATT_HEADS), w_in.dtype)], axis=1)
    w_big = jnp.concatenate([z_s, xbc, q, kc, ks, kw, z_a, small], axis=1).astype(BF16)
    w_kvt = jnp.concatenate([kc, ks, kw], axis=1).T.astype(BF16)
    return w_big, w_kvt


def _pair_cols(mat, p, shape):
    lane = lax.broadcasted_iota(jnp.int32, shape, 1)
    a = jnp.broadcast_to(mat[:, 2 * p:2 * p + 1], shape)
    b = jnp.broadcast_to(mat[:, 2 * p + 1:2 * p + 2], shape)
    return jnp.where(lane < HEAD_DIM, a, b)


def _ssd_kernel(xbc_ref, sm_ref, smt_ref, zs_ref, conv0_ref, h0_ref, cw_ref, cb_ref, dtb_ref, dtbc_ref,
                alog_ref, alogc_ref, dsk_ref, ng_ref, y_ref, hout_ref, xp_ref, h_ref):
    c = pl.program_id(1)
    nc = pl.num_programs(1)
    L = xbc_ref.shape[1]
    pad = xp_ref.shape[0] - L
    hp = jnp.float32

    @pl.when(c == 0)
    def _():
        xp_ref[0:pad, :] = conv0_ref[0]
        h_ref[...] = h0_ref[0]

    xp_ref[pad:pad + L, :] = xbc_ref[0]
    cw = cw_ref[...]
    conv = cb_ref[...]
    for w in range(CONV_WIDTH):
        o = pad - (CONV_WIDTH - 1) + w
        conv = conv + xp_ref[o:o + L, :] * cw[w:w + 1, :]
    halo = xp_ref[L:L + pad, :]
    xp_ref[0:pad, :] = halo

    u = _silu(conv)
    xs = u[:, :SSD_WIDTH]
    gw = SSD_GROUPS * D_STATE
    bm = u[:, SSD_WIDTH:SSD_WIDTH + gw]
    cm = u[:, SSD_WIDTH + gw:]

    dt = jax.nn.softplus(sm_ref[0] + dtb_ref[...])
    dta = dt * (-jnp.exp(alog_ref[...]))
    row = lax.broadcasted_iota(jnp.int32, (L, L), 0)
    col = lax.broadcasted_iota(jnp.int32, (L, L), 1)
    causal = row >= col
    la = jnp.dot(causal.astype(hp), dta, preferred_element_type=hp,
                 precision=lax.Precision.HIGHEST)
    dtt = jax.nn.softplus(smt_ref[0] + dtbc_ref[...])
    dtat = dtt * (-jnp.exp(alogc_ref[...]))
    lat = jnp.dot(dtat, (row <= col).astype(hp), preferred_element_type=hp,
                  precision=lax.Precision.HIGHEST)
    la_last = la[L - 1:L, :]
    ela = jnp.exp(la)
    te = jnp.exp(la_last - la)
    cdec = jnp.exp(la_last)

    lane = lax.broadcasted_iota(jnp.int32, (L, LANES), 1)
    srow = lax.broadcasted_iota(jnp.int32, (LANES, LANES), 0)
    hpg = SSD_HEADS // SSD_GROUPS
    ys = []
    for g in range(SSD_GROUPS):
        bm_g = bm[:, g * D_STATE:(g + 1) * D_STATE]
        cm_g = cm[:, g * D_STATE:(g + 1) * D_STATE].astype(BF16)
        bm_gb = bm_g.astype(BF16)
        cb = lax.dot_general(cm_g, bm_gb, (((1,), (1,)), ((), ())), preferred_element_type=hp)
        for pp in range(hpg // 2):
            p = g * (hpg // 2) + pp
            xs_p = xs[:, LANES * p:LANES * (p + 1)]
            xdt = xs_p * _pair_cols(dt, p, (L, LANES))
            xdt_b = xdt.astype(BF16)
            yd = []
            for r in (2 * p, 2 * p + 1):
                seg = la[:, r:r + 1] - lat[r:r + 1, :]
                dec = jnp.where(causal, jnp.exp(jnp.where(causal, seg, 0.0)), 0.0)
                yd.append(jnp.dot((cb * dec).astype(BF16), xdt_b, preferred_element_type=hp))
            y_diag = jnp.where(lane < HEAD_DIM, yd[0], yd[1])
            h_p = h_ref[p]
            y_off = lax.dot_general(cm_g, h_p.astype(BF16), (((1,), (1,)), ((), ())),
                                    preferred_element_type=hp) * _pair_cols(ela, p, (L, LANES))
            ys.append(y_diag + y_off + dsk_ref[:, LANES * p:LANES * (p + 1)] * xs_p)
            xw = (xdt * _pair_cols(te, p, (L, LANES))).astype(BF16)
            st = lax.dot_general(xw, bm_gb, (((0,), (0,)), ((), ())), preferred_element_type=hp)
            cd = jnp.where(srow < HEAD_DIM,
                           jnp.broadcast_to(cdec[:, 2 * p:2 * p + 1], (LANES, LANES)),
                           jnp.broadcast_to(cdec[:, 2 * p + 1:2 * p + 2], (LANES, LANES)))
            h_ref[p] = h_p * cd + st

    y = jnp.concatenate(ys, axis=1)
    t = y * _silu(zs_ref[0])
    ms = jnp.mean(t * t, axis=-1, keepdims=True)
    y_ref[0] = ((t * lax.rsqrt(ms + EPS)) * ng_ref[...]).astype(y_ref.dtype)

    @pl.when(c == nc - 1)
    def _():
        hout_ref[0] = h_ref[...]


def _ssd(xbc, sm, zs, conv_state, h0, conv_w, conv_b, dt_bias, a_log, d_skip, norm_g):
    bn, t, _ = xbc.shape
    L = min(SSD_CHUNK, t)
    assert t % L == 0
    nc = t // L
    pad = 8
    smt = jnp.swapaxes(sm, 1, 2)
    conv0 = jnp.pad(conv_state, ((0, 0), (pad - (CONV_WIDTH - 1), 0), (0, 0)))
    hp2 = h0.reshape(bn, SSD_HEADS // 2, 2 * HEAD_DIM, D_STATE)
    zpad = jnp.zeros((LANES - SSD_HEADS,), F32)
    dtb = jnp.concatenate([dt_bias.astype(F32), zpad])
    alog = jnp.concatenate([a_log.astype(F32), zpad])
    dsk = jnp.repeat(d_skip.astype(F32), HEAD_DIM).reshape(1, SSD_WIDTH)

    def full(shape):
        return pl.BlockSpec(shape, lambda b, c: tuple(0 for _ in shape))

    y, hout = pl.pallas_call(
        _ssd_kernel,
        grid=(bn, nc),
        in_specs=[pl.BlockSpec((1, L, CONV_CH), lambda b, c: (b, c, 0)),
                  pl.BlockSpec((1, L, LANES), lambda b, c: (b, c, 0)),
                  pl.BlockSpec((1, LANES, L), lambda b, c: (b, 0, c)),
                  pl.BlockSpec((1, L, SSD_WIDTH), lambda b, c: (b, c, 0)),
                  pl.BlockSpec((1, pad, CONV_CH), lambda b, c: (b, 0, 0)),
                  pl.BlockSpec((1, SSD_HEADS // 2, 2 * HEAD_DIM, D_STATE), lambda b, c: (b, 0, 0, 0)),
                  full((CONV_WIDTH, CONV_CH)), full((1, CONV_CH)),
                  full((1, LANES)), full((LANES, 1)), full((1, LANES)), full((LANES, 1)),
                  full((1, SSD_WIDTH)), full((1, SSD_WIDTH))],
        out_specs=[pl.BlockSpec((1, L, SSD_WIDTH), lambda b, c: (b, c, 0)),
                   pl.BlockSpec((1, SSD_HEADS // 2, 2 * HEAD_DIM, D_STATE), lambda b, c: (b, 0, 0, 0))],
        out_shape=[jax.ShapeDtypeStruct((bn, t, SSD_WIDTH), BF16),
                   jax.ShapeDtypeStruct((bn, SSD_HEADS // 2, 2 * HEAD_DIM, D_STATE), F32)],
        scratch_shapes=[pltpu.VMEM((L + pad, CONV_CH), F32),
                        pltpu.VMEM((SSD_HEADS // 2, 2 * HEAD_DIM, D_STATE), F32)],
        compiler_params=_cparams(("arbitrary", "arbitrary")),
        name="ssd",
    )(xbc, sm, smt, zs, conv0, hp2, conv_w, conv_b.reshape(1, CONV_CH),
      dtb.reshape(1, LANES), dtb.reshape(LANES, 1), alog.reshape(1, LANES), alog.reshape(LANES, 1),
      dsk, norm_g.reshape(1, SSD_WIDTH))
    return y, hout.reshape(bn, SSD_HEADS, HEAD_DIM, D_STATE)


def _half_mask(shape, kvh):
    lane = lax.broadcasted_iota(jnp.int32, shape, 1)
    return (lane >= HEAD_DIM) if kvh else (lane < HEAD_DIM)


def _stack_heads(q, kvh):
    tq = q.shape[0]
    keep = _half_mask((tq, LANES), kvh)
    blocks = []
    for g in range(GQA_GROUP):
        h = kvh * GQA_GROUP + g
        v = q[:, LANES * (h // 2):LANES * (h // 2 + 1)]
        if (h % 2) != kvh:
            v = pltpu.roll(v, HEAD_DIM, 1)
        blocks.append(jnp.where(keep, v, 0.0))
    return jnp.concatenate(blocks, axis=0)


def _unstack_heads(o_by_kvh, gates, branch, src_half_is_kvh=True):
    tq = gates.shape[0]
    lane = lax.broadcasted_iota(jnp.int32, (tq, LANES), 1)
    blocks = []
    for h in range(ATT_HEADS):
        kvh, g = divmod(h, GQA_GROUP)
        v = o_by_kvh[kvh][g * tq:(g + 1) * tq, :]
        gl = GATE_LANE0 + branch * ATT_HEADS + h
        v = v * gates[:, gl:gl + 1]
        src_half = kvh if src_half_is_kvh else 0
        if (h % 2) != src_half:
            v = pltpu.roll(v, HEAD_DIM, 1)
        blocks.append(v)
    outs = [jnp.where(lane < HEAD_DIM, blocks[2 * j], blocks[2 * j + 1]) for j in range(ATT_HEADS // 2)]
    return jnp.concatenate(outs, axis=1)


def _tile_rows(v, n):
    return jnp.concatenate([v] * n, axis=0)


def _flash_update(s, vaug, m_ref, acc_ref, kvh):
    m_prev = m_ref[kvh]
    m_new = jnp.maximum(m_prev, jnp.max(s, axis=1, keepdims=True))
    alpha = jnp.exp(m_prev - m_new)
    p = jnp.exp(s - m_new[:, 0:1])
    pv = jnp.dot(p.astype(BF16), vaug, preferred_element_type=F32)
    acc_ref[kvh] = acc_ref[kvh] * jnp.concatenate([alpha, alpha], axis=1) + pv
    m_ref[kvh] = m_new


def _flash_out(acc_ref, kvh):
    acc = acc_ref[kvh]
    return acc[:, :LANES] / acc[:, LANES:]


def _vaug(v01):
    return jnp.concatenate([v01.astype(BF16), jnp.ones(v01.shape, BF16)], axis=1)


def _topk_cols(score_t, rounds):
    nb = score_t.shape[0]
    ridx = lax.broadcasted_iota(jnp.int32, score_t.shape, 0)
    sel = jnp.zeros(score_t.shape, F32)
    cur = score_t
    for _ in range(rounds):
        mx = jnp.max(cur, axis=0, keepdims=True)
        idx = jnp.min(jnp.where(cur == mx, ridx, nb), axis=0, keepdims=True)
        hit = ridx == idx
        sel = jnp.where(hit & (mx > -jnp.inf), 1.0, sel)
        cur = jnp.where(hit, -jnp.inf, cur)
    return sel


def _topk_rows(score, rounds):
    nb = score.shape[1]
    lidx = lax.broadcasted_iota(jnp.int32, score.shape, 1)
    sel = jnp.zeros(score.shape, F32)
    cur = score
    for _ in range(rounds):
        mx = jnp.max(cur, axis=1, keepdims=True)
        idx = jnp.min(jnp.where(cur == mx, lidx, nb), axis=1, keepdims=True)
        hit = lidx == idx
        sel = jnp.where(hit & (mx > -jnp.inf), 1.0, sel)
        cur = jnp.where(hit, -jnp.inf, cur)
    return sel


def _cmp_attend(q, kc01, vc01, pos):
    tq = q.shape[0]
    ncl = kc01.shape[0]
    rows = GQA_GROUP * tq
    lane = lax.broadcasted_iota(jnp.int32, (rows, ncl), 1)
    cblk = 2 * (lane % MAX_SEL_LANES) + lane // MAX_SEL_LANES
    c_end = (cblk + 1) * CMP_BLOCK - 1
    mask = c_end <= _tile_rows(pos, GQA_GROUP)
    kcb = kc01.astype(BF16)
    vcb = vc01.astype(BF16)
    outs, imps = [], []
    for kvh in range(KV_HEADS):
        qs = _stack_heads(q, kvh).astype(BF16)
        s = lax.dot_general(qs, kcb, (((1,), (1,)), ((), ())), preferred_element_type=F32)
        s = jnp.where(mask, s, -jnp.inf)
        mx = jnp.max(s, axis=1, keepdims=True)
        mx = jnp.where(mx > -jnp.inf, mx, 0.0)
        e = jnp.exp(s - mx)
        dsum = jnp.sum(e, axis=1, keepdims=True)
        p = e / jnp.where(dsum > 0, dsum, 1.0)
        outs.append(jnp.dot(p.astype(BF16), vcb, preferred_element_type=F32))
        imp = p[0:tq]
        for g in range(1, GQA_GROUP):
            imp = imp + p[g * tq:(g + 1) * tq]
        imps.append(imp[:, :MAX_SEL_LANES] + imp[:, MAX_SEL_LANES:])
    return outs, imps


def _cmp_prompt_kernel(q_ref, kc_ref, vc_ref, sm_ref, o_ref, bias_ref):
    tq = q_ref.shape[1]
    i = pl.program_id(1)
    pos = i * tq + lax.broadcasted_iota(jnp.int32, (tq, 1), 0)
    q = q_ref[0] * SCALE
    outs, imps = _cmp_attend(q, kc_ref[0], vc_ref[0], pos)
    gates = jax.nn.sigmoid(sm_ref[0])
    o_ref[0] = _unstack_heads(outs, gates, 0)
    blk = lax.broadcasted_iota(jnp.int32, (tq, MAX_SEL_LANES), 1)
    cur = pos // SEL_BLOCK
    valid = blk <= cur
    forced = (blk == 0) | ((cur - blk >= 0) & (cur - blk < N_LOCAL_BLOCKS))
    for kvh in range(KV_HEADS):
        score = jnp.where(forced, jnp.inf, imps[kvh])
        score = jnp.where(valid, score, -jnp.inf)
        sel_t = _topk_cols(score.T, TOP_N)
        bias_ref[0, kvh] = jnp.where(sel_t > 0.5, 0.0, NEG).astype(BF16)


def _cmp_prompt(q, kcm_perm, sm, tq):
    b, t, _ = q.shape
    ncl = kcm_perm.shape[1]
    return pl.pallas_call(
        _cmp_prompt_kernel,
        grid=(b, t // tq),
        in_specs=[pl.BlockSpec((1, tq, ATT_WIDTH), lambda bb, i: (bb, i, 0)),
                  pl.BlockSpec((1, ncl, LANES), lambda bb, i: (bb, 0, 0)),
                  pl.BlockSpec((1, ncl, LANES), lambda bb, i: (bb, 0, 1)),
                  pl.BlockSpec((1, tq, LANES), lambda bb, i: (bb, i, 0))],
        out_specs=[pl.BlockSpec((1, tq, ATT_WIDTH), lambda bb, i: (bb, i, 0)),
                   pl.BlockSpec((1, KV_HEADS, MAX_SEL_LANES, tq), lambda bb, i: (bb, 0, 0, i))],
        out_shape=[jax.ShapeDtypeStruct((b, t, ATT_WIDTH), F32),
                   jax.ShapeDtypeStruct((b, KV_HEADS, MAX_SEL_LANES, t), BF16)],
        compiler_params=_cparams(("arbitrary", "arbitrary")),
        name="cmp_prompt",
    )(q, kcm_perm, kcm_perm, sm)


def _build_qaug(q, bias_ref, qaug_ref):
    for kvh in range(KV_HEADS):
        qs = _stack_heads(q, kvh).astype(BF16)
        bias = _tile_rows(bias_ref[0, kvh], GQA_GROUP)
        qaug_ref[kvh] = jnp.concatenate([qs, bias], axis=1)


def _head_rows(qt, h):
    blk = qt[h * HEAD_DIM:(h + 1) * HEAD_DIM]
    z = jnp.zeros_like(blk)
    return jnp.concatenate([blk, z] if h // GQA_GROUP == 0 else [z, blk], axis=0)


def _emit_heads(o_by_head, sm_ref, branch, o_ref):
    gates_t = jax.nn.sigmoid(sm_ref[0]).T
    cols = []
    for j in range(ATT_HEADS // 2):
        pair = []
        for h in (2 * j, 2 * j + 1):
            gl = GATE_LANE0 + branch * ATT_HEADS + h
            pair.append(o_by_head[h] * gates_t[gl:gl + 1, :])
        cols.append(jnp.concatenate(pair, axis=0).T)
    o_ref[0] = jnp.concatenate(cols, axis=1)


def _pipeline_order(n, lead):
    order = [("qk", h) for h in range(min(lead, n))]
    for h in range(n):
        order.append(("sm", h))
        if h + lead < n:
            order.append(("qk", h + lead))
        if h >= 1:
            order.append(("pv", h - 1))
    order.append(("pv", n - 1))
    return order


_SLC_ORDER = _pipeline_order(ATT_HEADS, 8)


def _slc_prompt_kernel(q_ref, bias_ref, ka_ref, vat_ref, sm_ref, o_ref, qa_ref, m_ref, acc_ref):
    tq = q_ref.shape[1]
    tk = tq
    i = pl.program_id(1)
    qt = (q_ref[0] * SCALE_LOG2).T
    for h in range(ATT_HEADS):
        qa_ref[h] = jnp.concatenate([bias_ref[0, h // GQA_GROUP], _head_rows(qt, h).astype(BF16)], axis=0)
    m_ref[...] = jnp.full(m_ref.shape, NEG, F32)
    acc_ref[...] = jnp.zeros(acc_ref.shape, F32)

    def tile(j, masked):
        k0 = pl.multiple_of(j * tk, tk)
        ka = ka_ref[0, pl.ds(k0, tk), :]
        ss, ps, alphas = {}, {}, {}

        def qk(h):
            s = jnp.dot(ka, qa_ref[h], preferred_element_type=F32)
            if masked:
                kofs = lax.broadcasted_iota(jnp.int32, (tk, tq), 0)
                qofs = lax.broadcasted_iota(jnp.int32, (tk, tq), 1)
                s = jnp.where(kofs <= qofs, s, NEG)
            ss[h] = s

        def sm(h):
            m_prev = m_ref[h]
            m_new = jnp.maximum(m_prev, jnp.max(ss[h], axis=0, keepdims=True))
            alphas[h] = jnp.exp2(m_prev - m_new)
            ps[h] = jnp.exp2(ss[h] - m_new).astype(BF16)
            m_ref[h] = m_new

        def pv(h):
            o = jnp.dot(vat_ref[0, h // GQA_GROUP, :, pl.ds(k0, tk)], ps[h], preferred_element_type=F32)
            acc_ref[h] = acc_ref[h] * alphas[h] + o

        for step in _SLC_ORDER:
            {"qk": qk, "sm": sm, "pv": pv}[step[0]](step[1])

    def body(j, carry):
        tile(j, False)
        return carry

    lax.fori_loop(0, i, body, 0)
    tile(i, True)
    outs = []
    for h in range(ATT_HEADS):
        acc = acc_ref[h]
        outs.append(acc[:HEAD_DIM] / acc[HEAD_DIM:HEAD_DIM + 1])
    _emit_heads(outs, sm_ref, 1, o_ref)


def _slc_prompt(q, bias_t, ka, vat, sm, tq):
    b, t, _ = q.shape
    kc = ka.shape[2]
    return pl.pallas_call(
        _slc_prompt_kernel,
        grid=(b, t // tq),
        in_specs=[pl.BlockSpec((1, tq, ATT_WIDTH), lambda bb, i: (bb, i, 0)),
                  pl.BlockSpec((1, KV_HEADS, MAX_SEL_LANES, tq), lambda bb, i: (bb, 0, 0, i)),
                  pl.BlockSpec((1, t, kc), lambda bb, i: (bb, 0, 0)),
                  pl.BlockSpec((1, KV_HEADS, VAT_ROWS, t), lambda bb, i: (bb, 0, 0, 0)),
                  pl.BlockSpec((1, tq, LANES), lambda bb, i: (bb, i, 0))],
        out_specs=pl.BlockSpec((1, tq, ATT_WIDTH), lambda bb, i: (bb, i, 0)),
        out_shape=jax.ShapeDtypeStruct((b, t, ATT_WIDTH), F32),
        scratch_shapes=[pltpu.VMEM((ATT_HEADS, kc, tq), BF16),
                        pltpu.VMEM((ATT_HEADS, 1, tq), F32),
                        pltpu.VMEM((ATT_HEADS, VAT_ROWS, tq), F32)],
        compiler_params=_cparams(("arbitrary", "arbitrary")),
        name="slc_prompt",
    )(q, bias_t, ka, vat, sm)


def _win_prompt_kernel(q_ref, sm_ref, *refs, n_prev):
    nt = n_prev + 1
    k_refs, vat_refs, o_ref = refs[:nt], refs[nt:2 * nt], refs[2 * nt]
    tq = q_ref.shape[1]
    i = pl.program_id(1)
    kofs = lax.broadcasted_iota(jnp.int32, (tq, tq), 0)
    qofs = lax.broadcasted_iota(jnp.int32, (tq, tq), 1)
    qt = (q_ref[0] * SCALE_LOG2).T
    scores = []
    for h in range(ATT_HEADS):
        qh = _head_rows(qt, h).astype(BF16)
        ss = []
        for n in range(nt):
            s = jnp.dot(k_refs[n][0], qh, preferred_element_type=F32)
            back = n_prev - n
            ok = i >= back
            if n == 0:
                ok = ok & (kofs >= qofs)
            if back == 0:
                ok = kofs <= qofs
            ss.append(jnp.where(ok, s, NEG))
        scores.append(ss)
    probs = []
    for ss in scores:
        mx = ss[0].max(axis=0, keepdims=True)
        for s in ss[1:]:
            mx = jnp.maximum(mx, s.max(axis=0, keepdims=True))
        probs.append([jnp.exp2(s - mx).astype(BF16) for s in ss])
    outs = []
    for h in range(ATT_HEADS):
        acc = None
        for n in range(nt):
            pv = jnp.dot(vat_refs[n][0, h // GQA_GROUP], probs[h][n], preferred_element_type=F32)
            acc = pv if acc is None else acc + pv
        outs.append(acc[:HEAD_DIM] / acc[HEAD_DIM:HEAD_DIM + 1])
    _emit_heads(outs, sm_ref, 2, o_ref)


def _win_prompt(q, kw, vat, sm, tq):
    b, t, _ = q.shape
    assert WINDOW % tq == 0 and WINDOW >= tq
    n_prev = WINDOW // tq

    def k_spec(back):
        return pl.BlockSpec((1, tq, KV_WIDTH), lambda bb, i: (bb, jnp.maximum(i - back, 0), 0))

    def v_spec(back):
        return pl.BlockSpec((1, KV_HEADS, VAT_ROWS, tq), lambda bb, i: (bb, 0, 0, jnp.maximum(i - back, 0)))

    backs = [n_prev - n for n in range(n_prev + 1)]
    nt = n_prev + 1
    return pl.pallas_call(
        functools.partial(_win_prompt_kernel, n_prev=n_prev),
        grid=(b, t // tq),
        in_specs=([pl.BlockSpec((1, tq, ATT_WIDTH), lambda bb, i: (bb, i, 0)),
                   pl.BlockSpec((1, tq, LANES), lambda bb, i: (bb, i, 0))]
                  + [k_spec(bk) for bk in backs] + [v_spec(bk) for bk in backs]),
        out_specs=pl.BlockSpec((1, tq, ATT_WIDTH), lambda bb, i: (bb, i, 0)),
        out_shape=jax.ShapeDtypeStruct((b, t, ATT_WIDTH), F32),
        compiler_params=_cparams(("arbitrary", "arbitrary")),
        name="win_prompt",
    )(q, sm, *([kw] * nt), *([vat] * nt))


def _cmp_sample_kernel(pt_ref, *refs, n_pg, past):
    page_refs = refs[:n_pg]
    q_ref, sm_ref, o_ref, bias_ref, kce_ref, kco_ref = refs[n_pg:]
    s = pl.program_id(1)
    ns = pl.num_programs(1)
    tn = q_ref.shape[1]

    @pl.when(s == 0)
    def _():
        kce_ref[...] = jnp.zeros(kce_ref.shape, F32)
        kco_ref[...] = jnp.zeros(kco_ref.shape, F32)

    x = jnp.concatenate([r[0] for r in page_refs], axis=0)
    nblk = n_pg * PAGE_SIZE // SEL_BLOCK
    x3 = x.reshape(nblk, SEL_BLOCK, 2 * KV_WIDTH)
    r0 = pl.multiple_of(s * nblk, nblk)
    kce_ref[pl.ds(r0, nblk), :] = jnp.sum(x3[:, :CMP_BLOCK, :], axis=1) * (1.0 / CMP_BLOCK)
    kco_ref[pl.ds(r0, nblk), :] = jnp.sum(x3[:, CMP_BLOCK:, :], axis=1) * (1.0 / CMP_BLOCK)

    @pl.when(s == ns - 1)
    def _():
        kc = jnp.concatenate([kce_ref[...], kco_ref[...]], axis=0)
        pos = past + lax.broadcasted_iota(jnp.int32, (tn, 1), 0)
        outs, imps = _cmp_attend(q_ref[0] * SCALE, kc[:, :LANES], kc[:, LANES:], pos)
        o_ref[0] = _unstack_heads(outs, jax.nn.sigmoid(sm_ref[0]), 0)
        n_past = past // SEL_BLOCK
        blk = lax.broadcasted_iota(jnp.int32, (tn, MAX_SEL_LANES), 1)
        forced = (blk == 0) | (blk == n_past - 1)
        rounds = min(TOP_N, n_past + 1) - 1
        for kvh in range(KV_HEADS):
            score = jnp.where(forced, jnp.inf, imps[kvh])
            score = jnp.where(blk < n_past, score, -jnp.inf)
            sel = _topk_rows(score, rounds)
            bias_ref[0, kvh] = jnp.where(sel > 0.5, 0.0, NEG).astype(BF16)


def _page_specs(n_pg):
    def spec(k):
        return pl.BlockSpec((1, PAGE_SIZE, 2 * KV_WIDTH), lambda b, s, pt: (pt[b, s * n_pg + k], 0, 0))
    return [spec(k) for k in range(n_pg)]


def _cmp_sample(page_table, pool, q, sm, n_pg):
    bs, tn, _ = q.shape
    n_pages = page_table.shape[1]
    past = n_pages * PAGE_SIZE
    grid_spec = pltpu.PrefetchScalarGridSpec(
        num_scalar_prefetch=1,
        grid=(bs, n_pages // n_pg),
        in_specs=_page_specs(n_pg) + [
            pl.BlockSpec((1, tn, ATT_WIDTH), lambda b, s, pt: (b, 0, 0)),
            pl.BlockSpec((1, tn, LANES), lambda b, s, pt: (b, 0, 0))],
        out_specs=[pl.BlockSpec((1, tn, ATT_WIDTH), lambda b, s, pt: (b, 0, 0)),
                   pl.BlockSpec((1, KV_HEADS, tn, LANES), lambda b, s, pt: (b, 0, 0, 0))],
        scratch_shapes=[pltpu.VMEM((MAX_SEL_LANES, 2 * KV_WIDTH), F32),
                        pltpu.VMEM((MAX_SEL_LANES, 2 * KV_WIDTH), F32)],
    )
    return pl.pallas_call(
        functools.partial(_cmp_sample_kernel, n_pg=n_pg, past=past),
        grid_spec=grid_spec,
        out_shape=[jax.ShapeDtypeStruct((bs, tn, ATT_WIDTH), F32),
                   jax.ShapeDtypeStruct((bs, KV_HEADS, tn, LANES), BF16)],
        compiler_params=_cparams(("arbitrary", "arbitrary")),
        name="cmp_sample",
    )(page_table, *([pool] * n_pg), q, sm)


def _slc_sample_kernel(pt_ref, *refs, n_pg, past):
    page_refs = refs[:n_pg]
    q_ref, bias_ref, new_ref, sm_ref, o_ref, qaug_ref, m_ref, acc_ref = refs[n_pg:]
    s = pl.program_id(1)
    ns = pl.num_programs(1)
    tn = q_ref.shape[1]
    rows = GQA_GROUP * tn

    @pl.when(s == 0)
    def _():
        _build_qaug(q_ref[0] * SCALE, bias_ref, qaug_ref)
        m_ref[...] = jnp.full(m_ref.shape, NEG, F32)
        acc_ref[...] = jnp.zeros(acc_ref.shape, F32)

    x = jnp.concatenate([r[0] for r in page_refs], axis=0)
    tk = n_pg * PAGE_SIZE
    kblk = (s * tk + lax.broadcasted_iota(jnp.int32, (tk, MAX_SEL_LANES), 0)) // SEL_BLOCK
    onehot = jnp.where(kblk == lax.broadcasted_iota(jnp.int32, (tk, MAX_SEL_LANES), 1), 1.0, 0.0).astype(BF16)
    kaug = jnp.concatenate([x[:, :LANES].astype(BF16), onehot], axis=1)
    vaug = _vaug(x[:, LANES:])
    for kvh in range(KV_HEADS):
        sc = lax.dot_general(qaug_ref[kvh], kaug, (((1,), (1,)), ((), ())), preferred_element_type=F32)
        _flash_update(sc, vaug, m_ref, acc_ref, kvh)

    @pl.when(s == ns - 1)
    def _():
        xn = new_ref[0]
        nk = xn.shape[0]
        kn = xn[:, :LANES].astype(BF16)
        vn = _vaug(xn[:, LANES:])
        kidx = lax.broadcasted_iota(jnp.int32, (rows, nk), 1)
        qidx = lax.broadcasted_iota(jnp.int32, (rows, nk), 0) % tn
        ok = kidx <= qidx
        for kvh in range(KV_HEADS):
            sc = lax.dot_general(qaug_ref[kvh, :, :LANES], kn, (((1,), (1,)), ((), ())),
                                 preferred_element_type=F32)
            _flash_update(jnp.where(ok, sc, NEG), vn, m_ref, acc_ref, kvh)
        outs = [_flash_out(acc_ref, kvh) for kvh in range(KV_HEADS)]
        o_ref[0] = _unstack_heads(outs, jax.nn.sigmoid(sm_ref[0]), 1)


def _slc_sample(page_table, pool, q, bias, new_pad, sm, n_pg):
    bs, tn, _ = q.shape
    n_pages = page_table.shape[1]
    past = n_pages * PAGE_SIZE
    rows = GQA_GROUP * tn
    npad = new_pad.shape[1]
    grid_spec = pltpu.PrefetchScalarGridSpec(
        num_scalar_prefetch=1,
        grid=(bs, n_pages // n_pg),
        in_specs=_page_specs(n_pg) + [
            pl.BlockSpec((1, tn, ATT_WIDTH), lambda b, s, pt: (b, 0, 0)),
            pl.BlockSpec((1, KV_HEADS, tn, LANES), lambda b, s, pt: (b, 0, 0, 0)),
            pl.BlockSpec((1, npad, 2 * KV_WIDTH), lambda b, s, pt: (b, 0, 0)),
            pl.BlockSpec((1, tn, LANES), lambda b, s, pt: (b, 0, 0))],
        out_specs=pl.BlockSpec((1, tn, ATT_WIDTH), lambda b, s, pt: (b, 0, 0)),
        scratch_shapes=[pltpu.VMEM((KV_HEADS, rows, 2 * LANES), BF16),
                        pltpu.VMEM((KV_HEADS, rows, LANES), F32),
                        pltpu.VMEM((KV_HEADS, rows, 2 * LANES), F32)],
    )
    return pl.pallas_call(
        functools.partial(_slc_sample_kernel, n_pg=n_pg, past=past),
        grid_spec=grid_spec,
        out_shape=jax.ShapeDtypeStruct((bs, tn, ATT_WIDTH), F32),
        compiler_params=_cparams(("arbitrary", "arbitrary")),
        name="slc_sample",
    )(page_table, *([pool] * n_pg), q, bias, new_pad, sm)


def _win_sample_kernel(q_ref, win_ref, new_ref, sm_ref, o_ref, *, past):
    tn = q_ref.shape[1]
    rows = GQA_GROUP * tn
    wb = win_ref.shape[1]
    kv = jnp.concatenate([win_ref[0], new_ref[0]], axis=0)
    nk = kv.shape[0]
    kb = kv[:, :LANES].astype(BF16)
    vb = kv[:, LANES:].astype(BF16)
    kidx = lax.broadcasted_iota(jnp.int32, (rows, nk), 1)
    qidx = lax.broadcasted_iota(jnp.int32, (rows, nk), 0) % tn
    kpos = past - wb + kidx
    dist = past + qidx - kpos
    ok = (dist >= 0) & (dist <= WINDOW) & (kpos >= 0) & (kidx < wb + tn)
    q = q_ref[0] * SCALE
    outs = []
    for kvh in range(KV_HEADS):
        qs = _stack_heads(q, kvh).astype(BF16)
        s = lax.dot_general(qs, kb, (((1,), (1,)), ((), ())), preferred_element_type=F32)
        s = jnp.where(ok, s, -jnp.inf)
        mx = jnp.max(s, axis=1, keepdims=True)
        e = jnp.exp(s - mx)
        p = e / jnp.sum(e, axis=1, keepdims=True)
        outs.append(jnp.dot(p.astype(BF16), vb, preferred_element_type=F32))
    o_ref[0] = _unstack_heads(outs, jax.nn.sigmoid(sm_ref[0]), 2)


def _win_sample(q, win_buf, new_pad, sm, past):
    bs, tn, _ = q.shape
    wb = win_buf.shape[1]
    npad = new_pad.shape[1]
    return pl.pallas_call(
        functools.partial(_win_sample_kernel, past=past),
        grid=(bs,),
        in_specs=[pl.BlockSpec((1, tn, ATT_WIDTH), lambda b: (b, 0, 0)),
                  pl.BlockSpec((1, wb, 2 * KV_WIDTH), lambda b: (b, 0, 0)),
                  pl.BlockSpec((1, npad, 2 * KV_WIDTH), lambda b: (b, 0, 0)),
                  pl.BlockSpec((1, tn, LANES), lambda b: (b, 0, 0))],
        out_specs=pl.BlockSpec((1, tn, ATT_WIDTH), lambda b: (b, 0, 0)),
        out_shape=jax.ShapeDtypeStruct((bs, tn, ATT_WIDTH), F32),
        compiler_params=_cparams(("arbitrary",)),
        name="win_sample",
    )(q, win_buf, new_pad, sm)


def _out_kernel(x_ref, oc_ref, os_ref, ow_ref, za_ref, ys_ref, gate_ref, ang_ref, wo_ref, fg_ref, y_ref):
    nbk, tt, d = x_ref.shape
    m = nbk * tt
    o = (oc_ref[...] + os_ref[...]) + ow_ref[...]
    t = o * _silu(za_ref[...])
    ms = jnp.mean(t * t, axis=-1, keepdims=True)
    y_att = (t * lax.rsqrt(ms + EPS)) * ang_ref[...].reshape(1, 1, ATT_WIDTH)
    ya = y_att.reshape(m, ATT_WIDTH).astype(BF16)
    ys = ys_ref[...].reshape(m, SSD_WIDTH)
    mix = (jnp.dot(ys, wo_ref[:SSD_WIDTH, :], preferred_element_type=F32)
           + jnp.dot(ya, wo_ref[SSD_WIDTH:, :], preferred_element_type=F32))
    xp = x_ref[...] + gate_ref[...] * mix.reshape(nbk, tt, d)
    ms2 = jnp.mean(xp * xp, axis=-1, keepdims=True)
    y_ref[...] = (xp * lax.rsqrt(ms2 + EPS)) * fg_ref[...].reshape(1, 1, d)


def _out(x3, o_c, o_s, o_w, za, y_ssd, gate, att_norm_g, w_out_b, final_g, *, nbk, groups_per_mod):
    g_total, tt, d = x3.shape
    steps = g_total // nbk
    if groups_per_mod is None:
        mod_spec = pl.BlockSpec((nbk, 1, d), lambda g: (g, 0, 0))
    else:
        mod_spec = pl.BlockSpec((1, 1, d), lambda g: (g // groups_per_mod, 0, 0))

    def tok_spec(c):
        return pl.BlockSpec((nbk, tt, c), lambda g: (g, 0, 0))

    return pl.pallas_call(
        _out_kernel,
        grid=(steps,),
        in_specs=[tok_spec(d), tok_spec(ATT_WIDTH), tok_spec(ATT_WIDTH), tok_spec(ATT_WIDTH),
                  tok_spec(ATT_WIDTH), tok_spec(SSD_WIDTH), mod_spec,
                  pl.BlockSpec((1, ATT_WIDTH), lambda g: (0, 0)),
                  pl.BlockSpec((SSD_WIDTH + ATT_WIDTH, d), lambda g: (0, 0)),
                  pl.BlockSpec((1, d), lambda g: (0, 0))],
        out_specs=tok_spec(d),
        out_shape=jax.ShapeDtypeStruct((g_total, tt, d), F32),
        compiler_params=_cparams(("arbitrary",)),
        name="outproj",
    )(x3, o_c, o_s, o_w, za, y_ssd, gate, att_norm_g.reshape(1, ATT_WIDTH), w_out_b, final_g.reshape(1, d))


def _perm_cmp_means(kcm, b):
    nc = kcm.shape[1]
    ns = nc // 2
    assert ns <= MAX_SEL_LANES
    eo = kcm.reshape(b, ns, 2, 2 * KV_WIDTH).transpose(0, 2, 1, 3)
    eo = jnp.pad(eo, ((0, 0), (0, 0), (0, MAX_SEL_LANES - ns), (0, 0)))
    return eo.reshape(b, 2 * MAX_SEL_LANES, 2 * KV_WIDTH)


def _prompt_layer(x, mod, lw, final_g, apply_final):
    b, t, d = x.shape
    shift, scale, gate = (mod[:, None, i * d:(i + 1) * d] for i in range(3))
    tt = 256
    assert t % tt == 0 and t % SSD_CHUNK == 0 and t >= WINDOW
    gpb = t // tt
    x3 = x.reshape(b * gpb, tt, d)
    zs, xbc, q, za, sm, kcm, kvt_c, kvt_s, kvt_w, ka_s, vat_s, kw, vat_w = _inproj_prompt(
        x, scale, shift, lw["norm_g"], lw["w_big"], lw["w_kvt"], tt)
    r = lambda a: a.reshape(b, t, a.shape[-1])
    zs, xbc, q, za, sm, ka_s, kw = map(r, (zs, xbc, q, za, sm, ka_s, kw))
    kcm = kcm.reshape(b, t // CMP_BLOCK, 2 * KV_WIDTH)

    conv0 = jnp.zeros((b, CONV_WIDTH - 1, CONV_CH), F32)
    h0 = jnp.zeros((b, SSD_HEADS, HEAD_DIM, D_STATE), F32)
    y_ssd, ssm_new = _ssd(xbc, sm, zs, conv0, h0, lw["conv_w"], lw["conv_b"], lw["dt_bias"], lw["a_log"],
                          lw["d_skip"], lw["ssd_norm_g"])

    o_c, bias = _cmp_prompt(q, _perm_cmp_means(kcm, b), sm, 128)
    tq = 256
    o_s = _slc_prompt(q, bias, ka_s, vat_s, sm, tq)
    o_w = _win_prompt(q, kw, vat_w, sm, tq)

    g3 = lambda a: a.reshape(b * gpb, tt, a.shape[-1])
    y3 = _out(x3, g3(o_c), g3(o_s), g3(o_w), g3(za), g3(y_ssd), gate, lw["att_norm_g"], lw["w_out_b"],
              final_g, nbk=1, groups_per_mod=gpb)
    assert apply_final
    kv6 = lambda a: a.reshape(b, 2, KV_HEADS, HEAD_DIM, a.shape[-1]).transpose(0, 4, 1, 2, 3)
    outs = (kv6(kvt_c), kv6(kvt_s), kv6(kvt_w[:, :, t - min(WINDOW, t):]), xbc[:, t - (CONV_WIDTH - 1):], ssm_new)
    return y3.reshape(b, t, d), outs


def _sample_layer(x, mod, lw, final_g, pool_c, pool_s, win_buf, conv_buf, ssm, page_table, apply_final):
    bs, tn, d = x.shape
    shift, scale, gate = (mod[:, None, i * d:(i + 1) * d] for i in range(3))
    n_pages = page_table.shape[1]
    past = n_pages * PAGE_SIZE
    nbk = 16
    n_pg = 8
    assert bs % nbk == 0 and tn % 8 == 0 and tn <= SEL_BLOCK and n_pages % n_pg == 0
    assert past // SEL_BLOCK <= MAX_SEL_LANES and past % SEL_BLOCK == 0
    pos = past + jnp.arange(tn, dtype=jnp.int32)
    tabs = tuple(jnp.tile(tb, (nbk, 1)) for tb in _rope_tables(pos))
    zs, xbc, q, za, sm, kvc, kvs, kvw = _inproj_sample(
        x, scale, shift, lw["norm_g"], lw["w_big"], tabs, nbk=nbk)

    y_ssd, ssm_new = _ssd(xbc, sm, zs, conv_buf, ssm, lw["conv_w"], lw["conv_b"], lw["dt_bias"], lw["a_log"],
                          lw["d_skip"], lw["ssd_norm_g"])

    npad = LANES
    pool_c2 = pool_c.reshape(pool_c.shape[0], PAGE_SIZE, 2 * KV_WIDTH)
    pool_s2 = pool_s.reshape(pool_s.shape[0], PAGE_SIZE, 2 * KV_WIDTH)
    o_c, bias = _cmp_sample(page_table, pool_c2, q, sm, n_pg)
    kvs_pad = jnp.pad(kvs, ((0, 0), (0, npad - tn), (0, 0)))
    o_s = _slc_sample(page_table, pool_s2, q, bias, kvs_pad, sm, n_pg)
    wb = win_buf.shape[1]
    win2 = win_buf.reshape(bs, wb, 2 * KV_WIDTH)
    kvw_pad = jnp.pad(kvw, ((0, 0), (0, npad - tn), (0, 0)))
    o_w = _win_sample(q, win2, kvw_pad, sm, past)

    y = _out(x, o_c, o_s, o_w, za, y_ssd, gate, lw["att_norm_g"], lw["w_out_b"], final_g,
             nbk=nbk, groups_per_mod=None)
    assert apply_final
    kv6 = lambda a: a.reshape(bs, a.shape[1], 2, KV_HEADS, HEAD_DIM)
    kv_w_all = jnp.concatenate([win2, kvw], axis=1)
    win_new = kv_w_all[:, kv_w_all.shape[1] - min(WINDOW, past + tn):]
    conv_new = jnp.concatenate([conv_buf, xbc], axis=1)[:, tn:]
    outs = (kv6(kvc), kv6(kvs), kv6(win_new), conv_new, ssm_new)
    return y, outs


def kernel(x_prompt, x_sample, cache_cmp_kv, cache_slc_kv, state_win_kv, state_conv, state_ssm, page_table,
           c_prompt, c_sample, w_ada, b_ada, norm_g, w_in, conv_w, conv_b, dt_bias, a_log, d_skip,
           ssd_norm_g, att_norm_g, w_out, final_g):
    depth = w_ada.shape[0]
    assert depth == 1
    n_prompt = c_prompt.shape[0]
    xp, xs = x_prompt, x_sample
    out_p, out_s = [], []
    for l in range(depth):
        w_big, w_kvt = _rearrange_w_in(w_in[l])
        lw = dict(norm_g=norm_g[l], w_big=w_big, w_kvt=w_kvt, conv_w=conv_w[l], conv_b=conv_b[l],
                  dt_bias=dt_bias[l], a_log=a_log[l], d_skip=d_skip[l], ssd_norm_g=ssd_norm_g[l],
                  att_norm_g=att_norm_g[l], w_out_b=w_out[l].astype(BF16))
        mod = _mod(jnp.concatenate([c_prompt, c_sample], axis=0), w_ada[l], b_ada[l])
        last = l == depth - 1
        xp, op = _prompt_layer(xp, mod[:n_prompt], lw, final_g, last)
        xs, os_ = _sample_layer(xs, mod[n_prompt:], lw, final_g, cache_cmp_kv[l], cache_slc_kv[l],
                                state_win_kv[l], state_conv[l], state_ssm[l], page_table, last)
        out_p.append(op)
        out_s.append(os_)
    sp = [jnp.stack([o[k] for o in out_p]) for k in range(5)]
    sd = [jnp.stack([o[k] for o in out_s]) for k in range(5)]
    return (xp, xs, sp[0], sp[1], sp[2], sp[3], sp[4], sd[0], sd[1], sd[2], sd[3], sd[4])
```

```python
import functools

import jax
import jax.numpy as jnp
from jax import lax
from jax.experimental import pallas as pl
from jax.experimental.pallas import tpu as pltpu

F32 = jnp.float32
BF16 = jnp.bfloat16

HEAD_DIM = 64
SSD_HEADS = 8
SSD_WIDTH = SSD_HEADS * HEAD_DIM
SSD_GROUPS = 2
D_STATE = 128
CONV_WIDTH = 4
CONV_CH = SSD_WIDTH + 2 * SSD_GROUPS * D_STATE
SSD_CHUNK = 256
ATT_HEADS = 8
ATT_WIDTH = ATT_HEADS * HEAD_DIM
KV_HEADS = 2
GQA_GROUP = ATT_HEADS // KV_HEADS
KV_WIDTH = KV_HEADS * HEAD_DIM
CMP_BLOCK = 32
SEL_BLOCK = 64
TOP_N = 16
N_LOCAL_BLOCKS = 2
WINDOW = 512
N_BRANCH = 3
ROT_DIM = HEAD_DIM // 4
ROPE_THETA = 500000.0
PAGE_SIZE = 128
EPS = 1e-6
COL_SIZES = (SSD_WIDTH, CONV_CH, SSD_HEADS, ATT_WIDTH, 2 * KV_WIDTH, 2 * KV_WIDTH, 2 * KV_WIDTH,
             N_BRANCH * ATT_HEADS, ATT_WIDTH)

LANES = 128
MAX_SEL_LANES = LANES
NEG = -1e30
SCALE = HEAD_DIM ** -0.5
SCALE_LOG2 = SCALE * 1.4426950408889634
GATE_LANE0 = SSD_HEADS
BF16_SUBLANES = 16
VAT_ROWS = HEAD_DIM + BF16_SUBLANES
VMEM_LIMIT = 56 * 1024 * 1024

_O_ZS = 0
_O_XBC = _O_ZS + SSD_WIDTH
_O_Q = _O_XBC + CONV_CH
_O_KC = _O_Q + ATT_WIDTH
_O_KS = _O_KC + 2 * KV_WIDTH
_O_KW = _O_KS + 2 * KV_WIDTH
_O_ZA = _O_KW + 2 * KV_WIDTH
_O_SM = _O_ZA + ATT_WIDTH
_W_COLS = _O_SM + LANES


def _cparams(sem):
    return pltpu.CompilerParams(dimension_semantics=sem, vmem_limit_bytes=VMEM_LIMIT)


def _silu(v):
    return v * jax.nn.sigmoid(v)


def _mod_kernel(c_ref, w_ref, b_ref, o_ref):
    a = _silu(c_ref[...])
    o_ref[...] = jnp.dot(a, w_ref[...], preferred_element_type=F32,
                         precision=lax.Precision.HIGHEST) + b_ref[...]


def _mod(c, w_ada, b_ada):
    n, d = c.shape
    cols = w_ada.shape[1]
    tn = d
    assert cols % tn == 0
    return pl.pallas_call(
        _mod_kernel,
        grid=(cols // tn,),
        in_specs=[pl.BlockSpec((n, d), lambda j: (0, 0)),
                  pl.BlockSpec((d, tn), lambda j: (0, j)),
                  pl.BlockSpec((1, tn), lambda j: (0, j))],
        out_specs=pl.BlockSpec((n, tn), lambda j: (0, j)),
        out_shape=jax.ShapeDtypeStruct((n, cols), F32),
        compiler_params=_cparams(("arbitrary",)),
        name="mod",
    )(c, w_ada, b_ada.reshape(1, cols))


def _rope128(v, rc, ra, rb):
    half = ROT_DIM // 2
    return v * rc + pltpu.roll(v, LANES - half, 1) * ra + pltpu.roll(v, half, 1) * rb


def _modulated_norm(x_ref, sc_ref, sh_ref, g_ref):
    nbk, tt, d = x_ref.shape
    x = x_ref[...]
    ms = jnp.mean(x * x, axis=-1, keepdims=True)
    y = (x * lax.rsqrt(ms + EPS)) * g_ref[...].reshape(1, 1, d)
    h = y * (1.0 + sc_ref[...]) + sh_ref[...]
    return h.reshape(nbk * tt, d).astype(BF16)


def _inproj_common(hb, w_ref, rc, ra, rb, zs_ref, xbc_ref, q_ref, za_ref, sm_ref):
    nbk, tt, _ = zs_ref.shape

    def proj(lo, n):
        return jnp.dot(hb, w_ref[:, lo:lo + n], preferred_element_type=F32)

    zs_ref[...] = proj(_O_ZS, SSD_WIDTH).reshape(nbk, tt, SSD_WIDTH)
    xbc_ref[...] = proj(_O_XBC, CONV_CH).reshape(nbk, tt, CONV_CH)
    za_ref[...] = proj(_O_ZA, ATT_WIDTH).reshape(nbk, tt, ATT_WIDTH)
    sm_ref[...] = proj(_O_SM, LANES).reshape(nbk, tt, LANES)
    qraw = proj(_O_Q, ATT_WIDTH)
    q = jnp.concatenate([_rope128(qraw[:, LANES * j:LANES * (j + 1)], rc, ra, rb)
                         for j in range(ATT_WIDTH // LANES)], axis=1)
    q_ref[...] = q.reshape(nbk, tt, ATT_WIDTH)
    return proj


def _inproj_sample_kernel(x_ref, sc_ref, sh_ref, g_ref, w_ref, rc_ref, ra_ref, rb_ref,
                          zs_ref, xbc_ref, q_ref, za_ref, sm_ref, kvc_ref, kvs_ref, kvw_ref):
    nbk, tt, _ = x_ref.shape
    hb = _modulated_norm(x_ref, sc_ref, sh_ref, g_ref)
    rc, ra, rb = rc_ref[...], ra_ref[...], rb_ref[...]
    proj = _inproj_common(hb, w_ref, rc, ra, rb, zs_ref, xbc_ref, q_ref, za_ref, sm_ref)
    for off, ref in ((_O_KC, kvc_ref), (_O_KS, kvs_ref), (_O_KW, kvw_ref)):
        u = proj(off, 2 * KV_WIDTH)
        kv = jnp.concatenate([_rope128(u[:, :KV_WIDTH], rc, ra, rb), u[:, KV_WIDTH:]], axis=1)
        ref[...] = kv.reshape(nbk, tt, 2 * KV_WIDTH)


def _rope_rows(k, cos, sin):
    half = ROT_DIM // 2
    x1, x2 = k[:half], k[half:ROT_DIM]
    return jnp.concatenate([x1 * cos - x2 * sin, x2 * cos + x1 * sin, k[ROT_DIM:]], axis=0)


def _inproj_prompt_kernel(x_ref, sc_ref, sh_ref, g_ref, w_ref, wt_ref, rc_ref, ra_ref, rb_ref, cos_ref, sin_ref,
                          zs_ref, xbc_ref, q_ref, za_ref, sm_ref, kcm_ref,
                          kvtc_ref, kvts_ref, kvtw_ref, kas_ref, vats_ref, kw_ref, vatw_ref, *, groups_per_seq):
    _, tt, _ = x_ref.shape
    hb = _modulated_norm(x_ref, sc_ref, sh_ref, g_ref)
    rc, ra, rb = rc_ref[...], ra_ref[...], rb_ref[...]
    proj = _inproj_common(hb, w_ref, rc, ra, rb, zs_ref, xbc_ref, q_ref, za_ref, sm_ref)

    u = proj(_O_KC, 2 * KV_WIDTH)
    kv = jnp.concatenate([_rope128(u[:, :KV_WIDTH], rc, ra, rb), u[:, KV_WIDTH:]], axis=1)
    nblk = tt // CMP_BLOCK
    means = jnp.sum(kv.reshape(nblk, CMP_BLOCK, 2 * KV_WIDTH), axis=1) * (1.0 / CMP_BLOCK)
    kcm_ref[...] = means.reshape(1, nblk, 2 * KV_WIDTH)

    ti = pl.program_id(0) % groups_per_seq
    blk = (ti * tt + lax.broadcasted_iota(jnp.int32, (tt, MAX_SEL_LANES), 0)) // SEL_BLOCK
    onehot = jnp.where(blk == lax.broadcasted_iota(jnp.int32, (tt, MAX_SEL_LANES), 1), 1.0, 0.0)
    ks = _rope128(proj(_O_KS, KV_WIDTH), rc, ra, rb)
    kas_ref[0] = jnp.concatenate([onehot, ks], axis=1).astype(BF16)
    kw_ref[0] = _rope128(proj(_O_KW, KV_WIDTH), rc, ra, rb).astype(BF16)

    ut = lax.dot_general(wt_ref[...], hb, (((1,), (1,)), ((), ())), preferred_element_type=F32)
    cos, sin = cos_ref[...], sin_ref[...]
    w2 = 2 * KV_WIDTH
    slabs = []
    for br in range(N_BRANCH):
        s = ut[br * w2:(br + 1) * w2]
        ks = [_rope_rows(s[h * HEAD_DIM:(h + 1) * HEAD_DIM], cos, sin) for h in range(KV_HEADS)]
        slabs.append(jnp.concatenate(ks + [s[KV_WIDTH:]], axis=0))
    kvtc_ref[0] = slabs[0]
    kvts_ref[0] = slabs[1]
    kvtw_ref[0] = slabs[2]
    ones = jnp.ones((VAT_ROWS - HEAD_DIM, tt), F32)
    for slab, vat_ref in ((slabs[1], vats_ref), (slabs[2], vatw_ref)):
        for h in range(KV_HEADS):
            v = slab[KV_WIDTH + h * HEAD_DIM:KV_WIDTH + (h + 1) * HEAD_DIM]
            vat_ref[0, h] = jnp.concatenate([v, ones], axis=0).astype(BF16)


def _mod_spec(nbk, d, groups_per_mod):
    if groups_per_mod is None:
        return pl.BlockSpec((nbk, 1, d), lambda g: (g, 0, 0))
    return pl.BlockSpec((1, 1, d), lambda g: (g // groups_per_mod, 0, 0))


def _inproj_sample(x3, scale, shift, norm_g, w_big, rope_tabs, *, nbk):
    g_total, tt, d = x3.shape
    m = nbk * tt
    mod_spec = _mod_spec(nbk, d, None)
    tab_spec = pl.BlockSpec((m, LANES), lambda g: (0, 0))

    def tok_spec(c):
        return pl.BlockSpec((nbk, tt, c), lambda g: (g, 0, 0))

    widths = (SSD_WIDTH, CONV_CH, ATT_WIDTH, ATT_WIDTH, LANES, 2 * KV_WIDTH, 2 * KV_WIDTH, 2 * KV_WIDTH)
    return pl.pallas_call(
        _inproj_sample_kernel,
        grid=(g_total // nbk,),
        in_specs=[tok_spec(d), mod_spec, mod_spec,
                  pl.BlockSpec((1, d), lambda g: (0, 0)),
                  pl.BlockSpec((d, _W_COLS), lambda g: (0, 0)),
                  tab_spec, tab_spec, tab_spec],
        out_specs=[tok_spec(c) for c in widths],
        out_shape=[jax.ShapeDtypeStruct((g_total, tt, c), F32) for c in widths],
        compiler_params=_cparams(("arbitrary",)),
        name="inproj_sample",
    )(x3, scale, shift, norm_g.reshape(1, d), w_big, *rope_tabs)


def _inproj_prompt(x, scale, shift, norm_g, w_big, w_kvt, tt):
    b, t, d = x.shape
    gps = t // tt
    steps = b * gps
    x3 = x.reshape(steps, tt, d)
    pos = jnp.arange(t, dtype=jnp.int32)
    tabs = _rope_tables(pos)
    cos_t, sin_t = _rope_angles(pos)
    mod_spec = _mod_spec(1, d, gps)
    tab_spec = pl.BlockSpec((tt, LANES), lambda g: (g % gps, 0))
    ang_spec = pl.BlockSpec((ROT_DIM // 2, tt), lambda g: (0, g % gps))

    def tok_spec(c):
        return pl.BlockSpec((1, tt, c), lambda g: (g, 0, 0))

    def row_spec(r):
        return pl.BlockSpec((1, r, tt), lambda g: (g // gps, 0, g % gps))

    vat_spec = pl.BlockSpec((1, KV_HEADS, VAT_ROWS, tt), lambda g: (g // gps, 0, 0, g % gps))
    nblk = tt // CMP_BLOCK
    w2 = 2 * KV_WIDTH
    tok_widths = (SSD_WIDTH, CONV_CH, ATT_WIDTH, ATT_WIDTH, LANES)
    out_specs = ([tok_spec(c) for c in tok_widths]
                 + [pl.BlockSpec((1, nblk, w2), lambda g: (g, 0, 0))]
                 + [row_spec(w2)] * 3
                 + [tok_spec(MAX_SEL_LANES + KV_WIDTH), vat_spec, tok_spec(KV_WIDTH), vat_spec])
    vat_shape = jax.ShapeDtypeStruct((b, KV_HEADS, VAT_ROWS, t), BF16)
    out_shape = ([jax.ShapeDtypeStruct((steps, tt, c), F32) for c in tok_widths]
                 + [jax.ShapeDtypeStruct((steps, nblk, w2), F32)]
                 + [jax.ShapeDtypeStruct((b, w2, t), F32)] * 3
                 + [jax.ShapeDtypeStruct((steps, tt, MAX_SEL_LANES + KV_WIDTH), BF16), vat_shape,
                    jax.ShapeDtypeStruct((steps, tt, KV_WIDTH), BF16), vat_shape])
    return pl.pallas_call(
        functools.partial(_inproj_prompt_kernel, groups_per_seq=gps),
        grid=(steps,),
        in_specs=[tok_spec(d), mod_spec, mod_spec,
                  pl.BlockSpec((1, d), lambda g: (0, 0)),
                  pl.BlockSpec((d, _W_COLS), lambda g: (0, 0)),
                  pl.BlockSpec((N_BRANCH * w2, d), lambda g: (0, 0)),
                  tab_spec, tab_spec, tab_spec, ang_spec, ang_spec],
        out_specs=out_specs,
        out_shape=out_shape,
        compiler_params=_cparams(("arbitrary",)),
        name="inproj_prompt",
    )(x3, scale, shift, norm_g.reshape(1, d), w_big, w_kvt, *tabs, cos_t, sin_t)


def _rope_angles(pos):
    half = ROT_DIM // 2
    inv_freq = ROPE_THETA ** (-jnp.arange(half, dtype=F32) * 2.0 / ROT_DIM)
    ang = pos.astype(F32)[:, None] * inv_freq[None, :]
    return jnp.cos(ang).T, jnp.sin(ang).T


def _rope_tables(pos):
    half = ROT_DIM // 2
    inv_freq = ROPE_THETA ** (-jnp.arange(half, dtype=F32) * 2.0 / ROT_DIM)
    ang = pos.astype(F32)[:, None] * inv_freq[None, :]
    cos, sin = jnp.cos(ang), jnp.sin(ang)
    n = pos.shape[0]
    one = jnp.ones((n, HEAD_DIM - ROT_DIM), F32)
    zero_h = jnp.zeros((n, half), F32)
    zero_r = jnp.zeros((n, HEAD_DIM - ROT_DIM), F32)
    rc = jnp.concatenate([cos, cos, one], axis=1)
    ra = jnp.concatenate([-sin, zero_h, zero_r], axis=1)
    rb = jnp.concatenate([zero_h, sin, zero_r], axis=1)
    rep = LANES // HEAD_DIM
    return tuple(jnp.tile(t, (1, rep)) for t in (rc, ra, rb))


def _rearrange_w_in(w_in):
    parts, o = [], 0
    for n in COL_SIZES:
        parts.append(w_in[:, o:o + n])
        o += n
    z_s, xbc, dt, q, kc, ks, kw, g, z_a = parts
    d = w_in.shape[0]
    small = jnp.concatenate([dt, g, jnp.zeros((d, LANES - SSD_HEADS - N_BRANCH * ATT_HEADS), w_in.dtype)], axis=1)
    w_big = jnp.concatenate([z_s, xbc, q, kc, ks, kw, z_a, small], axis=1).astype(BF16)
    w_kvt = jnp.concatenate([kc, ks, kw], axis=1).T.astype(BF16)
    return w_big, w_kvt


def _pair_cols(mat, p, shape):
    lane = lax.broadcasted_iota(jnp.int32, shape, 1)
    a = jnp.broadcast_to(mat[:, 2 * p:2 * p + 1], shape)
    b = jnp.broadcast_to(mat[:, 2 * p + 1:2 * p + 2], shape)
    return jnp.where(lane < HEAD_DIM, a, b)


def _ssd_kernel(xbc_ref, sm_ref, smt_ref, zs_ref, conv0_ref, h0_ref, cw_ref, cb_ref, dtb_ref, dtbc_ref,
                alog_ref, alogc_ref, dsk_ref, ng_ref, y_ref, hout_ref, xp_ref, h_ref):
    c = pl.program_id(1)
    nc = pl.num_programs(1)
    L = xbc_ref.shape[1]
    pad = xp_ref.shape[0] - L
    hp = jnp.float32

    @pl.when(c == 0)
    def _():
        xp_ref[0:pad, :] = conv0_ref[0]
        h_ref[...] = h0_ref[0]

    xp_ref[pad:pad + L, :] = xbc_ref[0]
    cw = cw_ref[...]
    conv = cb_ref[...]
    for w in range(CONV_WIDTH):
        o = pad - (CONV_WIDTH - 1) + w
        conv = conv + xp_ref[o:o + L, :] * cw[w:w + 1, :]
    halo = xp_ref[L:L + pad, :]
    xp_ref[0:pad, :] = halo

    u = _silu(conv)
    xs = u[:, :SSD_WIDTH]
    gw = SSD_GROUPS * D_STATE
    bm = u[:, SSD_WIDTH:SSD_WIDTH + gw]
    cm = u[:, SSD_WIDTH + gw:]

    dt = jax.nn.softplus(sm_ref[0] + dtb_ref[...])
    dta = dt * (-jnp.exp(alog_ref[...]))
    row = lax.broadcasted_iota(jnp.int32, (L, L), 0)
    col = lax.broadcasted_iota(jnp.int32, (L, L), 1)
    causal = row >= col
    la = jnp.dot(causal.astype(hp), dta, preferred_element_type=hp,
                 precision=lax.Precision.HIGHEST)
    dtt = jax.nn.softplus(smt_ref[0] + dtbc_ref[...])
    dtat = dtt * (-jnp.exp(alogc_ref[...]))
    lat = jnp.dot(dtat, (row <= col).astype(hp), preferred_element_type=hp,
                  precision=lax.Precision.HIGHEST)
    la_last = la[L - 1:L, :]
    ela = jnp.exp(la)
    te = jnp.exp(la_last - la)
    cdec = jnp.exp(la_last)

    lane = lax.broadcasted_iota(jnp.int32, (L, LANES), 1)
    srow = lax.broadcasted_iota(jnp.int32, (LANES, LANES), 0)
    hpg = SSD_HEADS // SSD_GROUPS
    ys = []
    for g in range(SSD_GROUPS):
        bm_g = bm[:, g * D_STATE:(g + 1) * D_STATE]
        cm_g = cm[:, g * D_STATE:(g + 1) * D_STATE].astype(BF16)
        bm_gb = bm_g.astype(BF16)
        cb = lax.dot_general(cm_g, bm_gb, (((1,), (1,)), ((), ())), preferred_element_type=hp)
        for pp in range(hpg // 2):
            p = g * (hpg // 2) + pp
            xs_p = xs[:, LANES * p:LANES * (p + 1)]
            xdt = xs_p * _pair_cols(dt, p, (L, LANES))
            xdt_b = xdt.astype(BF16)
            yd = []
            for r in (2 * p, 2 * p + 1):
                seg = la[:, r:r + 1] - lat[r:r + 1, :]
                dec = jnp.where(causal, jnp.exp(jnp.where(causal, seg, 0.0)), 0.0)
                yd.append(jnp.dot((cb * dec).astype(BF16), xdt_b, preferred_element_type=hp))
            y_diag = jnp.where(lane < HEAD_DIM, yd[0], yd[1])
            h_p = h_ref[p]
            y_off = lax.dot_general(cm_g, h_p.astype(BF16), (((1,), (1,)), ((), ())),
                                    preferred_element_type=hp) * _pair_cols(ela, p, (L, LANES))
            ys.append(y_diag + y_off + dsk_ref[:, LANES * p:LANES * (p + 1)] * xs_p)
            xw = (xdt * _pair_cols(te, p, (L, LANES))).astype(BF16)
            st = lax.dot_general(xw, bm_gb, (((0,), (0,)), ((), ())), preferred_element_type=hp)
            cd = jnp.where(srow < HEAD_DIM,
                           jnp.broadcast_to(cdec[:, 2 * p:2 * p + 1], (LANES, LANES)),
                           jnp.broadcast_to(cdec[:, 2 * p + 1:2 * p + 2], (LANES, LANES)))
            h_ref[p] = h_p * cd + st

    y = jnp.concatenate(ys, axis=1)
    t = y * _silu(zs_ref[0])
    ms = jnp.mean(t * t, axis=-1, keepdims=True)
    y_ref[0] = ((t * lax.rsqrt(ms + EPS)) * ng_ref[...]).astype(y_ref.dtype)

    @pl.when(c == nc - 1)
    def _():
        hout_ref[0] = h_ref[...]


def _ssd(xbc, sm, zs, conv_state, h0, conv_w, conv_b, dt_bias, a_log, d_skip, norm_g):
    bn, t, _ = xbc.shape
    L = min(SSD_CHUNK, t)
    assert t % L == 0
    nc = t // L
    pad = 8
    smt = jnp.swapaxes(sm, 1, 2)
    conv0 = jnp.pad(conv_state, ((0, 0), (pad - (CONV_WIDTH - 1), 0), (0, 0)))
    hp2 = h0.reshape(bn, SSD_HEADS // 2, 2 * HEAD_DIM, D_STATE)
    zpad = jnp.zeros((LANES - SSD_HEADS,), F32)
    dtb = jnp.concatenate([dt_bias.astype(F32), zpad])
    alog = jnp.concatenate([a_log.astype(F32), zpad])
    dsk = jnp.repeat(d_skip.astype(F32), HEAD_DIM).reshape(1, SSD_WIDTH)

    def full(shape):
        return pl.BlockSpec(shape, lambda b, c: tuple(0 for _ in shape))

    y, hout = pl.pallas_call(
        _ssd_kernel,
        grid=(bn, nc),
        in_specs=[pl.BlockSpec((1, L, CONV_CH), lambda b, c: (b, c, 0)),
                  pl.BlockSpec((1, L, LANES), lambda b, c: (b, c, 0)),
                  pl.BlockSpec((1, LANES, L), lambda b, c: (b, 0, c)),
                  pl.BlockSpec((1, L, SSD_WIDTH), lambda b, c: (b, c, 0)),
                  pl.BlockSpec((1, pad, CONV_CH), lambda b, c: (b, 0, 0)),
                  pl.BlockSpec((1, SSD_HEADS // 2, 2 * HEAD_DIM, D_STATE), lambda b, c: (b, 0, 0, 0)),
                  full((CONV_WIDTH, CONV_CH)), full((1, CONV_CH)),
                  full((1, LANES)), full((LANES, 1)), full((1, LANES)), full((LANES, 1)),
                  full((1, SSD_WIDTH)), full((1, SSD_WIDTH))],
        out_specs=[pl.BlockSpec((1, L, SSD_WIDTH), lambda b, c: (b, c, 0)),
                   pl.BlockSpec((1, SSD_HEADS // 2, 2 * HEAD_DIM, D_STATE), lambda b, c: (b, 0, 0, 0))],
        out_shape=[jax.ShapeDtypeStruct((bn, t, SSD_WIDTH), BF16),
                   jax.ShapeDtypeStruct((bn, SSD_HEADS // 2, 2 * HEAD_DIM, D_STATE), F32)],
        scratch_shapes=[pltpu.VMEM((L + pad, CONV_CH), F32),
                        pltpu.VMEM((SSD_HEADS // 2, 2 * HEAD_DIM, D_STATE), F32)],
        compiler_params=_cparams(("arbitrary", "arbitrary")),
        name="ssd",
    )(xbc, sm, smt, zs, conv0, hp2, conv_w, conv_b.reshape(1, CONV_CH),
      dtb.reshape(1, LANES), dtb.reshape(LANES, 1), alog.reshape(1, LANES), alog.reshape(LANES, 1),
      dsk, norm_g.reshape(1, SSD_WIDTH))
    return y, hout.reshape(bn, SSD_HEADS, HEAD_DIM, D_STATE)


def _half_mask(shape, kvh):
    lane = lax.broadcasted_iota(jnp.int32, shape, 1)
    return (lane >= HEAD_DIM) if kvh else (lane < HEAD_DIM)


def _stack_heads(q, kvh):
    tq = q.shape[0]
    keep = _half_mask((tq, LANES), kvh)
    blocks = []
    for g in range(GQA_GROUP):
        h = kvh * GQA_GROUP + g
        v = q[:, LANES * (h // 2):LANES * (h // 2 + 1)]
        if (h % 2) != kvh:
            v = pltpu.roll(v, HEAD_DIM, 1)
        blocks.append(jnp.where(keep, v, 0.0))
    return jnp.concatenate(blocks, axis=0)


def _unstack_heads(o_by_kvh, gates, branch, src_half_is_kvh=True):
    tq = gates.shape[0]
    lane = lax.broadcasted_iota(jnp.int32, (tq, LANES), 1)
    blocks = []
    for h in range(ATT_HEADS):
        kvh, g = divmod(h, GQA_GROUP)
        v = o_by_kvh[kvh][g * tq:(g + 1) * tq, :]
        gl = GATE_LANE0 + branch * ATT_HEADS + h
        v = v * gates[:, gl:gl + 1]
        src_half = kvh if src_half_is_kvh else 0
        if (h % 2) != src_half:
            v = pltpu.roll(v, HEAD_DIM, 1)
        blocks.append(v)
    outs = [jnp.where(lane < HEAD_DIM, blocks[2 * j], blocks[2 * j + 1]) for j in range(ATT_HEADS // 2)]
    return jnp.concatenate(outs, axis=1)


def _tile_rows(v, n):
    return jnp.concatenate([v] * n, axis=0)


def _flash_update(s, vaug, m_ref, acc_ref, kvh):
    m_prev = m_ref[kvh]
    m_new = jnp.maximum(m_prev, jnp.max(s, axis=1, keepdims=True))
    alpha = jnp.exp(m_prev - m_new)
    p = jnp.exp(s - m_new[:, 0:1])
    pv = jnp.dot(p.astype(BF16), vaug, preferred_element_type=F32)
    acc_ref[kvh] = acc_ref[kvh] * jnp.concatenate([alpha, alpha], axis=1) + pv
    m_ref[kvh] = m_new


def _flash_update_t(s, vaug_t, m_ref, acc_ref, kvh):
    m_prev = m_ref[kvh]
    m_new = jnp.maximum(m_prev, jnp.max(s, axis=1, keepdims=True))
    alpha = jnp.exp(m_prev - m_new)
    p = jnp.exp(s - m_new[:, 0:1])
    pv = lax.dot_general(p.astype(BF16), vaug_t, (((1,), (1,)), ((), ())), preferred_element_type=F32)
    acc_ref[kvh] = acc_ref[kvh] * jnp.concatenate([alpha, alpha], axis=1) + pv
    m_ref[kvh] = m_new


def _flash_out(acc_ref, kvh):
    acc = acc_ref[kvh]
    return acc[:, :LANES] / acc[:, LANES:]


def _vaug(v01):
    return jnp.concatenate([v01.astype(BF16), jnp.ones(v01.shape, BF16)], axis=1)


def _topk_cols(score_t, rounds):
    nb = score_t.shape[0]
    ridx = lax.broadcasted_iota(jnp.int32, score_t.shape, 0)
    sel = jnp.zeros(score_t.shape, F32)
    cur = score_t
    for _ in range(rounds):
        mx = jnp.max(cur, axis=0, keepdims=True)
        idx = jnp.min(jnp.where(cur == mx, ridx, nb), axis=0, keepdims=True)
        hit = ridx == idx
        sel = jnp.where(hit & (mx > -jnp.inf), 1.0, sel)
        cur = jnp.where(hit, -jnp.inf, cur)
    return sel


def _topk_rows(score, rounds):
    nb = score.shape[1]
    lidx = lax.broadcasted_iota(jnp.int32, score.shape, 1)
    ahead = jnp.zeros(score.shape, jnp.int32)
    for r in range(1, nb):
        other = pltpu.roll(score, r, 1)
        wins = (other > score) | ((other == score) & (lidx >= r))
        ahead = ahead + jnp.where(wins, 1, 0)
    return jnp.where((ahead < rounds) & (score > -jnp.inf), 1.0, 0.0)


def _cmp_attend(q, kc01, vc01, pos):
    tq = q.shape[0]
    ncl = kc01.shape[0]
    rows = GQA_GROUP * tq
    lane = lax.broadcasted_iota(jnp.int32, (rows, ncl), 1)
    cblk = 2 * (lane % MAX_SEL_LANES) + lane // MAX_SEL_LANES
    c_end = (cblk + 1) * CMP_BLOCK - 1
    mask = c_end <= _tile_rows(pos, GQA_GROUP)
    kcb = kc01.astype(BF16)
    vcb = vc01.astype(BF16)
    outs, imps = [], []
    for kvh in range(KV_HEADS):
        qs = _stack_heads(q, kvh).astype(BF16)
        s = lax.dot_general(qs, kcb, (((1,), (1,)), ((), ())), preferred_element_type=F32)
        s = jnp.where(mask, s, -jnp.inf)
        mx = jnp.max(s, axis=1, keepdims=True)
        mx = jnp.where(mx > -jnp.inf, mx, 0.0)
        e = jnp.exp(s - mx)
        dsum = jnp.sum(e, axis=1, keepdims=True)
        p = e / jnp.where(dsum > 0, dsum, 1.0)
        outs.append(jnp.dot(p.astype(BF16), vcb, preferred_element_type=F32))
        imp = p[0:tq]
        for g in range(1, GQA_GROUP):
            imp = imp + p[g * tq:(g + 1) * tq]
        imps.append(imp[:, :MAX_SEL_LANES] + imp[:, MAX_SEL_LANES:])
    return outs, imps


def _cmp_prompt_kernel(q_ref, kc_ref, vc_ref, sm_ref, o_ref, bias_ref):
    tq = q_ref.shape[1]
    i = pl.program_id(1)
    pos = i * tq + lax.broadcasted_iota(jnp.int32, (tq, 1), 0)
    q = q_ref[0] * SCALE
    outs, imps = _cmp_attend(q, kc_ref[0], vc_ref[0], pos)
    gates = jax.nn.sigmoid(sm_ref[0])
    o_ref[0] = _unstack_heads(outs, gates, 0)
    blk = lax.broadcasted_iota(jnp.int32, (tq, MAX_SEL_LANES), 1)
    cur = pos // SEL_BLOCK
    valid = blk <= cur
    forced = (blk == 0) | ((cur - blk >= 0) & (cur - blk < N_LOCAL_BLOCKS))
    for kvh in range(KV_HEADS):
        score = jnp.where(forced, jnp.inf, imps[kvh])
        score = jnp.where(valid, score, -jnp.inf)
        sel_t = _topk_cols(score.T, TOP_N)
        bias_ref[0, kvh] = jnp.where(sel_t > 0.5, 0.0, NEG).astype(BF16)


def _cmp_prompt(q, kcm_perm, sm, tq):
    b, t, _ = q.shape
    ncl = kcm_perm.shape[1]
    return pl.pallas_call(
        _cmp_prompt_kernel,
        grid=(b, t // tq),
        in_specs=[pl.BlockSpec((1, tq, ATT_WIDTH), lambda bb, i: (bb, i, 0)),
                  pl.BlockSpec((1, ncl, LANES), lambda bb, i: (bb, 0, 0)),
                  pl.BlockSpec((1, ncl, LANES), lambda bb, i: (bb, 0, 1)),
                  pl.BlockSpec((1, tq, LANES), lambda bb, i: (bb, i, 0))],
        out_specs=[pl.BlockSpec((1, tq, ATT_WIDTH), lambda bb, i: (bb, i, 0)),
                   pl.BlockSpec((1, KV_HEADS, MAX_SEL_LANES, tq), lambda bb, i: (bb, 0, 0, i))],
        out_shape=[jax.ShapeDtypeStruct((b, t, ATT_WIDTH), F32),
                   jax.ShapeDtypeStruct((b, KV_HEADS, MAX_SEL_LANES, t), BF16)],
        compiler_params=_cparams(("arbitrary", "arbitrary")),
        name="cmp_prompt",
    )(q, kcm_perm, kcm_perm, sm)


def _build_qaug(q, bias_ref, qaug_ref):
    for kvh in range(KV_HEADS):
        qs = _stack_heads(q, kvh).astype(BF16)
        bias = _tile_rows(bias_ref[0, kvh], GQA_GROUP)
        qaug_ref[kvh] = jnp.concatenate([qs, bias], axis=1)


def _head_rows(qt, h):
    blk = qt[h * HEAD_DIM:(h + 1) * HEAD_DIM]
    z = jnp.zeros_like(blk)
    return jnp.concatenate([blk, z] if h // GQA_GROUP == 0 else [z, blk], axis=0)


def _emit_heads(o_by_head, sm_ref, branch, o_ref):
    gates_t = jax.nn.sigmoid(sm_ref[0]).T
    cols = []
    for j in range(ATT_HEADS // 2):
        pair = []
        for h in (2 * j, 2 * j + 1):
            gl = GATE_LANE0 + branch * ATT_HEADS + h
            pair.append(o_by_head[h] * gates_t[gl:gl + 1, :])
        cols.append(jnp.concatenate(pair, axis=0).T)
    o_ref[0] = jnp.concatenate(cols, axis=1)


def _pipeline_order(n, lead):
    order = [("qk", h) for h in range(min(lead, n))]
    for h in range(n):
        order.append(("sm", h))
        if h + lead < n:
            order.append(("qk", h + lead))
        if h >= 1:
            order.append(("pv", h - 1))
    order.append(("pv", n - 1))
    return order


_SLC_ORDER = _pipeline_order(ATT_HEADS, 8)


def _slc_prompt_kernel(q_ref, bias_ref, ka_ref, vat_ref, sm_ref, o_ref, qa_ref, m_ref, acc_ref):
    tq = q_ref.shape[1]
    tk = tq
    i = pl.program_id(1)
    qt = (q_ref[0] * SCALE_LOG2).T
    for h in range(ATT_HEADS):
        qa_ref[h] = jnp.concatenate([bias_ref[0, h // GQA_GROUP], _head_rows(qt, h).astype(BF16)], axis=0)
    m_ref[...] = jnp.full(m_ref.shape, NEG, F32)
    acc_ref[...] = jnp.zeros(acc_ref.shape, F32)

    def tiles(j0, n, masked):
        k0s = [pl.multiple_of((j0 + t) * tk, tk) for t in range(n)]
        kas = [ka_ref[0, pl.ds(k0, tk), :] for k0 in k0s]
        ss, ps, alphas = {}, {}, {}

        def qk(h):
            ss[h] = []
            for ka in kas:
                s = jnp.dot(ka, qa_ref[h], preferred_element_type=F32)
                if masked:
                    kofs = lax.broadcasted_iota(jnp.int32, (tk, tq), 0)
                    qofs = lax.broadcasted_iota(jnp.int32, (tk, tq), 1)
                    s = jnp.where(kofs <= qofs, s, NEG)
                ss[h].append(s)

        def sm(h):
            m_prev = m_ref[h]
            m_new = m_prev
            for s in ss[h]:
                m_new = jnp.maximum(m_new, jnp.max(s, axis=0, keepdims=True))
            alphas[h] = jnp.exp2(m_prev - m_new)
            ps[h] = [jnp.exp2(s - m_new).astype(BF16) for s in ss[h]]
            m_ref[h] = m_new

        def pv(h):
            acc = acc_ref[h] * alphas[h]
            for k0, p in zip(k0s, ps[h]):
                acc = acc + jnp.dot(vat_ref[0, h // GQA_GROUP, :, pl.ds(k0, tk)], p, preferred_element_type=F32)
            acc_ref[h] = acc

        for step in _SLC_ORDER:
            {"qk": qk, "sm": sm, "pv": pv}[step[0]](step[1])

    def body(jp, carry):
        tiles(2 * jp, 2, False)
        return carry

    lax.fori_loop(0, i // 2, body, 0)

    @pl.when(i % 2 == 1)
    def _():
        tiles(i - 1, 1, False)

    tiles(i, 1, True)
    outs = []
    for h in range(ATT_HEADS):
        acc = acc_ref[h]
        outs.append(acc[:HEAD_DIM] / acc[HEAD_DIM:HEAD_DIM + 1])
    _emit_heads(outs, sm_ref, 1, o_ref)


def _slc_prompt(q, bias_t, ka, vat, sm, tq):
    b, t, _ = q.shape
    kc = ka.shape[2]
    return pl.pallas_call(
        _slc_prompt_kernel,
        grid=(b, t // tq),
        in_specs=[pl.BlockSpec((1, tq, ATT_WIDTH), lambda bb, i: (bb, i, 0)),
                  pl.BlockSpec((1, KV_HEADS, MAX_SEL_LANES, tq), lambda bb, i: (bb, 0, 0, i)),
                  pl.BlockSpec((1, t, kc), lambda bb, i: (bb, 0, 0)),
                  pl.BlockSpec((1, KV_HEADS, VAT_ROWS, t), lambda bb, i: (bb, 0, 0, 0)),
                  pl.BlockSpec((1, tq, LANES), lambda bb, i: (bb, i, 0))],
        out_specs=pl.BlockSpec((1, tq, ATT_WIDTH), lambda bb, i: (bb, i, 0)),
        out_shape=jax.ShapeDtypeStruct((b, t, ATT_WIDTH), F32),
        scratch_shapes=[pltpu.VMEM((ATT_HEADS, kc, tq), BF16),
                        pltpu.VMEM((ATT_HEADS, 1, tq), F32),
                        pltpu.VMEM((ATT_HEADS, VAT_ROWS, tq), F32)],
        compiler_params=_cparams(("arbitrary", "arbitrary")),
        name="slc_prompt",
    )(q, bias_t, ka, vat, sm)


def _win_prompt_kernel(q_ref, sm_ref, *refs, n_prev):
    nt = n_prev + 1
    k_refs, vat_refs, o_ref = refs[:nt], refs[nt:2 * nt], refs[2 * nt]
    tq = q_ref.shape[1]
    i = pl.program_id(1)
    kofs = lax.broadcasted_iota(jnp.int32, (tq, tq), 0)
    qofs = lax.broadcasted_iota(jnp.int32, (tq, tq), 1)
    qt = (q_ref[0] * SCALE_LOG2).T
    scores = []
    for h in range(ATT_HEADS):
        qh = _head_rows(qt, h).astype(BF16)
        ss = []
        for n in range(nt):
            s = jnp.dot(k_refs[n][0], qh, preferred_element_type=F32)
            back = n_prev - n
            ok = i >= back
            if n == 0:
                ok = ok & (kofs >= qofs)
            if back == 0:
                ok = kofs <= qofs
            ss.append(jnp.where(ok, s, NEG))
        scores.append(ss)
    probs = []
    for ss in scores:
        mx = ss[0].max(axis=0, keepdims=True)
        for s in ss[1:]:
            mx = jnp.maximum(mx, s.max(axis=0, keepdims=True))
        probs.append([jnp.exp2(s - mx).astype(BF16) for s in ss])
    outs = []
    for h in range(ATT_HEADS):
        acc = None
        for n in range(nt):
            pv = jnp.dot(vat_refs[n][0, h // GQA_GROUP], probs[h][n], preferred_element_type=F32)
            acc = pv if acc is None else acc + pv
        outs.append(acc[:HEAD_DIM] / acc[HEAD_DIM:HEAD_DIM + 1])
    _emit_heads(outs, sm_ref, 2, o_ref)


def _win_prompt(q, kw, vat, sm, tq):
    b, t, _ = q.shape
    assert WINDOW % tq == 0 and WINDOW >= tq
    n_prev = WINDOW // tq

    def k_spec(back):
        return pl.BlockSpec((1, tq, KV_WIDTH), lambda bb, i: (bb, jnp.maximum(i - back, 0), 0))

    def v_spec(back):
        return pl.BlockSpec((1, KV_HEADS, VAT_ROWS, tq), lambda bb, i: (bb, 0, 0, jnp.maximum(i - back, 0)))

    backs = [n_prev - n for n in range(n_prev + 1)]
    nt = n_prev + 1
    return pl.pallas_call(
        functools.partial(_win_prompt_kernel, n_prev=n_prev),
        grid=(b, t // tq),
        in_specs=([pl.BlockSpec((1, tq, ATT_WIDTH), lambda bb, i: (bb, i, 0)),
                   pl.BlockSpec((1, tq, LANES), lambda bb, i: (bb, i, 0))]
                  + [k_spec(bk) for bk in backs] + [v_spec(bk) for bk in backs]),
        out_specs=pl.BlockSpec((1, tq, ATT_WIDTH), lambda bb, i: (bb, i, 0)),
        out_shape=jax.ShapeDtypeStruct((b, t, ATT_WIDTH), F32),
        compiler_params=_cparams(("arbitrary", "arbitrary")),
        name="win_prompt",
    )(q, sm, *([kw] * nt), *([vat] * nt))


CMP_PAGES = LANES * CMP_BLOCK // PAGE_SIZE


def _cmp_local_block(lane):
    half = LANES // 2
    return 2 * (lane % half) + lane // half


def _block_mean_matrix(n_pages):
    tok = jnp.arange(n_pages * PAGE_SIZE, dtype=jnp.int32)[:, None] // CMP_BLOCK
    col = jnp.arange(n_pages * PAGE_SIZE // CMP_BLOCK, dtype=jnp.int32)[None, :]
    blk = LANES * (col // LANES) + _cmp_local_block(col % LANES)
    return jnp.where(tok == blk, 1.0 / CMP_BLOCK, 0.0).astype(BF16)


def _cmp_sample_kernel(pt_ref, *refs, n_pg, past):
    page_refs = refs[:n_pg]
    a_ref, q_ref, sm_ref, o_ref, bias_ref, kct_ref = refs[n_pg:]
    s = pl.program_id(1)
    ns = pl.num_programs(1)
    n_chunks = kct_ref.shape[0]
    tn = q_ref.shape[1]
    rows = GQA_GROUP * tn

    x = jnp.concatenate([r[0] for r in page_refs], axis=1)
    hi = x.astype(BF16)
    lo = (x - hi.astype(F32)).astype(BF16)
    a = a_ref[...]
    means = jnp.dot(hi, a, preferred_element_type=F32) + jnp.dot(lo, a, preferred_element_type=F32)
    for c in range(n_chunks):
        kct_ref[c] = means[:, c * LANES:(c + 1) * LANES]

    @pl.when(s == ns - 1)
    def _():
        pos = past + lax.broadcasted_iota(jnp.int32, (tn, 1), 0)
        lane = lax.broadcasted_iota(jnp.int32, (rows, LANES), 1)
        q = q_ref[0] * SCALE
        outs, imps = [], []
        for kvh in range(KV_HEADS):
            qs = _stack_heads(q, kvh).astype(BF16)
            ss = []
            for c in range(n_chunks):
                sc = jnp.dot(qs, kct_ref[c, :KV_WIDTH, :].astype(BF16), preferred_element_type=F32)
                c_end = (c * LANES + _cmp_local_block(lane) + 1) * CMP_BLOCK - 1
                ss.append(jnp.where(c_end <= _tile_rows(pos, GQA_GROUP), sc, -jnp.inf))
            mx = ss[0].max(axis=1, keepdims=True)
            for sc in ss[1:]:
                mx = jnp.maximum(mx, sc.max(axis=1, keepdims=True))
            mx = jnp.where(mx > -jnp.inf, mx, 0.0)
            es = [jnp.exp(sc - mx) for sc in ss]
            dsum = es[0].sum(axis=1, keepdims=True)
            for e in es[1:]:
                dsum = dsum + e.sum(axis=1, keepdims=True)
            inv = 1.0 / jnp.where(dsum > 0, dsum, 1.0)
            o = None
            imp_blocks = []
            for c in range(n_chunks):
                p = es[c] * inv
                vc = kct_ref[c, KV_WIDTH:, :].astype(BF16)
                pv = lax.dot_general(p.astype(BF16), vc, (((1,), (1,)), ((), ())), preferred_element_type=F32)
                o = pv if o is None else o + pv
                imp = p[0:tn]
                for g in range(1, GQA_GROUP):
                    imp = imp + p[g * tn:(g + 1) * tn]
                imp_blocks.append(imp + pltpu.roll(imp, LANES // 2, 1))
            outs.append(o)
            if n_chunks == 1:
                imps.append(imp_blocks[0])
            else:
                lane_t = lax.broadcasted_iota(jnp.int32, (tn, LANES), 1)
                imps.append(jnp.where(lane_t < LANES // 2, imp_blocks[0], pltpu.roll(imp_blocks[1], LANES // 2, 1)))
        o_ref[0] = _unstack_heads(outs, jax.nn.sigmoid(sm_ref[0]), 0)
        n_past = past // SEL_BLOCK
        blk = lax.broadcasted_iota(jnp.int32, (tn, MAX_SEL_LANES), 1)
        forced = (blk == 0) | (blk == n_past - 1)
        rounds = min(TOP_N, n_past + 1) - 1
        for kvh in range(KV_HEADS):
            score = jnp.where(forced, jnp.inf, imps[kvh])
            score = jnp.where(blk < n_past, score, -jnp.inf)
            sel = _topk_rows(score, rounds)
            bias_ref[0, kvh] = jnp.where(sel > 0.5, 0.0, NEG).astype(BF16)


def _page_specs(n_pg):
    def spec(k):
        return pl.BlockSpec((1, 2 * KV_WIDTH, PAGE_SIZE), lambda b, s, pt: (pt[b, s * n_pg + k], 0, 0))
    return [spec(k) for k in range(n_pg)]


def _cmp_sample(page_table, pool_t, q, sm):
    bs, tn, _ = q.shape
    n_pages = page_table.shape[1]
    past = n_pages * PAGE_SIZE
    n_pg = n_pages
    assert n_pages % CMP_PAGES == 0 and n_pages // CMP_PAGES <= 2
    n_chunks = n_pages // CMP_PAGES
    grid_spec = pltpu.PrefetchScalarGridSpec(
        num_scalar_prefetch=1,
        grid=(bs, 1),
        in_specs=_page_specs(n_pg) + [
            pl.BlockSpec((n_pg * PAGE_SIZE, n_chunks * LANES), lambda b, s, pt: (0, 0)),
            pl.BlockSpec((1, tn, ATT_WIDTH), lambda b, s, pt: (b, 0, 0)),
            pl.BlockSpec((1, tn, LANES), lambda b, s, pt: (b, 0, 0))],
        out_specs=[pl.BlockSpec((1, tn, ATT_WIDTH), lambda b, s, pt: (b, 0, 0)),
                   pl.BlockSpec((1, KV_HEADS, tn, LANES), lambda b, s, pt: (b, 0, 0, 0))],
        scratch_shapes=[pltpu.VMEM((n_chunks, 2 * KV_WIDTH, LANES), F32)],
    )
    return pl.pallas_call(
        functools.partial(_cmp_sample_kernel, n_pg=n_pg, past=past),
        grid_spec=grid_spec,
        out_shape=[jax.ShapeDtypeStruct((bs, tn, ATT_WIDTH), F32),
                   jax.ShapeDtypeStruct((bs, KV_HEADS, tn, LANES), BF16)],
        compiler_params=_cparams(("arbitrary", "arbitrary")),
        name="cmp_sample",
    )(page_table, *([pool_t] * n_pg), _block_mean_matrix(n_pages), q, sm)


def _slc_sample_kernel(pt_ref, *refs, n_pg, past):
    page_refs = refs[:n_pg]
    q_ref, bias_ref, new_ref, sm_ref, o_ref, qaug_ref, m_ref, acc_ref = refs[n_pg:]
    s = pl.program_id(1)
    ns = pl.num_programs(1)
    tn = q_ref.shape[1]
    rows = GQA_GROUP * tn

    nq = KV_HEADS * rows

    @pl.when(s == 0)
    def _():
        q = q_ref[0] * SCALE
        blocks = [jnp.concatenate([_stack_heads(q, kvh).astype(BF16), _tile_rows(bias_ref[0, kvh], GQA_GROUP)],
                                  axis=1) for kvh in range(KV_HEADS)]
        qaug_ref[...] = jnp.concatenate(blocks + [jnp.zeros((LANES - nq, 2 * LANES), BF16)], axis=0)
        m_ref[...] = jnp.full(m_ref.shape, NEG, F32)
        acc_ref[...] = jnp.zeros(acc_ref.shape, F32)

    def update(sts, vaugs):
        m_prev = m_ref[...]
        m_new = m_prev
        for st in sts:
            m_new = jnp.maximum(m_new, jnp.max(st, axis=0, keepdims=True))
        acc = acc_ref[...] * jnp.exp(m_prev - m_new)
        for st, va in zip(sts, vaugs):
            acc = acc + jnp.dot(va, jnp.exp(st - m_new).astype(BF16), preferred_element_type=F32)
        acc_ref[...] = acc
        m_ref[...] = m_new

    x = jnp.concatenate([r[0] for r in page_refs], axis=1)
    tk = n_pg * PAGE_SIZE
    kblk = (s * tk + lax.broadcasted_iota(jnp.int32, (MAX_SEL_LANES, tk), 1)) // SEL_BLOCK
    onehot = jnp.where(kblk == lax.broadcasted_iota(jnp.int32, (MAX_SEL_LANES, tk), 0), 1.0, 0.0).astype(BF16)
    kaug_t = jnp.concatenate([x[:KV_WIDTH].astype(BF16), onehot], axis=0)
    vaug_t = jnp.concatenate([x[KV_WIDTH:].astype(BF16), jnp.ones((BF16_SUBLANES, tk), BF16)], axis=0)
    n_piece = 2
    w = tk // n_piece
    qa = qaug_ref[...]
    sts = [lax.dot_general(kaug_t[:, i * w:(i + 1) * w], qa, (((0,), (1,)), ((), ())), preferred_element_type=F32)
           for i in range(n_piece)]
    update(sts, [vaug_t[:, i * w:(i + 1) * w] for i in range(n_piece)])

    @pl.when(s == ns - 1)
    def _():
        xn = new_ref[0]
        nk = xn.shape[0]
        st = lax.dot_general(xn[:, :LANES].astype(BF16), qaug_ref[:, :LANES], (((1,), (1,)), ((), ())),
                             preferred_element_type=F32)
        kidx = lax.broadcasted_iota(jnp.int32, (nk, LANES), 0)
        qidx = lax.broadcasted_iota(jnp.int32, (nk, LANES), 1) % tn
        vn_t = jnp.concatenate([xn[:, LANES:].T.astype(BF16), jnp.ones((BF16_SUBLANES, nk), BF16)], axis=0)
        update([jnp.where(kidx <= qidx, st, NEG)], [vn_t])
        acc = acc_ref[...]
        o_t = (acc[:KV_WIDTH] / acc[KV_WIDTH:KV_WIDTH + 1]).T
        outs = [o_t[kvh * rows:(kvh + 1) * rows] for kvh in range(KV_HEADS)]
        o_ref[0] = _unstack_heads(outs, jax.nn.sigmoid(sm_ref[0]), 1)


def _slc_sample(page_table, pool, q, bias, new_pad, sm, n_pg):
    bs, tn, _ = q.shape
    n_pages = page_table.shape[1]
    past = n_pages * PAGE_SIZE
    rows = GQA_GROUP * tn
    npad = new_pad.shape[1]
    grid_spec = pltpu.PrefetchScalarGridSpec(
        num_scalar_prefetch=1,
        grid=(bs, n_pages // n_pg),
        in_specs=_page_specs(n_pg) + [
            pl.BlockSpec((1, tn, ATT_WIDTH), lambda b, s, pt: (b, 0, 0)),
            pl.BlockSpec((1, KV_HEADS, tn, LANES), lambda b, s, pt: (b, 0, 0, 0)),
            pl.BlockSpec((1, npad, 2 * KV_WIDTH), lambda b, s, pt: (b, 0, 0)),
            pl.BlockSpec((1, tn, LANES), lambda b, s, pt: (b, 0, 0))],
        out_specs=pl.BlockSpec((1, tn, ATT_WIDTH), lambda b, s, pt: (b, 0, 0)),
        scratch_shapes=[pltpu.VMEM((LANES, 2 * LANES), BF16),
                        pltpu.VMEM((1, LANES), F32),
                        pltpu.VMEM((KV_WIDTH + BF16_SUBLANES, LANES), F32)],
    )
    assert KV_HEADS * rows <= LANES
    return pl.pallas_call(
        functools.partial(_slc_sample_kernel, n_pg=n_pg, past=past),
        grid_spec=grid_spec,
        out_shape=jax.ShapeDtypeStruct((bs, tn, ATT_WIDTH), F32),
        compiler_params=_cparams(("arbitrary", "arbitrary")),
        name="slc_sample",
    )(page_table, *([pool] * n_pg), q, bias, new_pad, sm)


def _win_sample_kernel(q_ref, win_ref, new_ref, sm_ref, o_ref, *, past):
    tn = q_ref.shape[1]
    rows = GQA_GROUP * tn
    wb = win_ref.shape[1]
    kv = jnp.concatenate([win_ref[0], new_ref[0]], axis=0)
    nk = kv.shape[0]
    kb = kv[:, :LANES].astype(BF16)
    vb = kv[:, LANES:].astype(BF16)
    kidx = lax.broadcasted_iota(jnp.int32, (rows, nk), 1)
    qidx = lax.broadcasted_iota(jnp.int32, (rows, nk), 0) % tn
    kpos = past - wb + kidx
    dist = past + qidx - kpos
    ok = (dist >= 0) & (dist <= WINDOW) & (kpos >= 0) & (kidx < wb + tn)
    q = q_ref[0] * SCALE
    outs = []
    for kvh in range(KV_HEADS):
        qs = _stack_heads(q, kvh).astype(BF16)
        s = lax.dot_general(qs, kb, (((1,), (1,)), ((), ())), preferred_element_type=F32)
        s = jnp.where(ok, s, -jnp.inf)
        mx = jnp.max(s, axis=1, keepdims=True)
        e = jnp.exp(s - mx)
        p = e / jnp.sum(e, axis=1, keepdims=True)
        outs.append(jnp.dot(p.astype(BF16), vb, preferred_element_type=F32))
    o_ref[0] = _unstack_heads(outs, jax.nn.sigmoid(sm_ref[0]), 2)


def _win_sample(q, win_buf, new_pad, sm, past):
    bs, tn, _ = q.shape
    wb = win_buf.shape[1]
    npad = new_pad.shape[1]
    return pl.pallas_call(
        functools.partial(_win_sample_kernel, past=past),
        grid=(bs,),
        in_specs=[pl.BlockSpec((1, tn, ATT_WIDTH), lambda b: (b, 0, 0)),
                  pl.BlockSpec((1, wb, 2 * KV_WIDTH), lambda b: (b, 0, 0)),
                  pl.BlockSpec((1, npad, 2 * KV_WIDTH), lambda b: (b, 0, 0)),
                  pl.BlockSpec((1, tn, LANES), lambda b: (b, 0, 0))],
        out_specs=pl.BlockSpec((1, tn, ATT_WIDTH), lambda b: (b, 0, 0)),
        out_shape=jax.ShapeDtypeStruct((bs, tn, ATT_WIDTH), F32),
        compiler_params=_cparams(("arbitrary",)),
        name="win_sample",
    )(q, win_buf, new_pad, sm)


def _out_kernel(x_ref, oc_ref, os_ref, ow_ref, za_ref, ys_ref, gate_ref, ang_ref, wo_ref, fg_ref, y_ref):
    nbk, tt, d = x_ref.shape
    m = nbk * tt
    o = (oc_ref[...] + os_ref[...]) + ow_ref[...]
    t = o * _silu(za_ref[...])
    ms = jnp.mean(t * t, axis=-1, keepdims=True)
    y_att = (t * lax.rsqrt(ms + EPS)) * ang_ref[...].reshape(1, 1, ATT_WIDTH)
    ya = y_att.reshape(m, ATT_WIDTH).astype(BF16)
    ys = ys_ref[...].reshape(m, SSD_WIDTH)
    mix = (jnp.dot(ys, wo_ref[:SSD_WIDTH, :], preferred_element_type=F32)
           + jnp.dot(ya, wo_ref[SSD_WIDTH:, :], preferred_element_type=F32))
    xp = x_ref[...] + gate_ref[...] * mix.reshape(nbk, tt, d)
    ms2 = jnp.mean(xp * xp, axis=-1, keepdims=True)
    y_ref[...] = (xp * lax.rsqrt(ms2 + EPS)) * fg_ref[...].reshape(1, 1, d)


def _out(x3, o_c, o_s, o_w, za, y_ssd, gate, att_norm_g, w_out_b, final_g, *, nbk, groups_per_mod):
    g_total, tt, d = x3.shape
    steps = g_total // nbk
    if groups_per_mod is None:
        mod_spec = pl.BlockSpec((nbk, 1, d), lambda g: (g, 0, 0))
    else:
        mod_spec = pl.BlockSpec((1, 1, d), lambda g: (g // groups_per_mod, 0, 0))

    def tok_spec(c):
        return pl.BlockSpec((nbk, tt, c), lambda g: (g, 0, 0))

    return pl.pallas_call(
        _out_kernel,
        grid=(steps,),
        in_specs=[tok_spec(d), tok_spec(ATT_WIDTH), tok_spec(ATT_WIDTH), tok_spec(ATT_WIDTH),
                  tok_spec(ATT_WIDTH), tok_spec(SSD_WIDTH), mod_spec,
                  pl.BlockSpec((1, ATT_WIDTH), lambda g: (0, 0)),
                  pl.BlockSpec((SSD_WIDTH + ATT_WIDTH, d), lambda g: (0, 0)),
                  pl.BlockSpec((1, d), lambda g: (0, 0))],
        out_specs=tok_spec(d),
        out_shape=jax.ShapeDtypeStruct((g_total, tt, d), F32),
        compiler_params=_cparams(("arbitrary",)),
        name="outproj",
    )(x3, o_c, o_s, o_w, za, y_ssd, gate, att_norm_g.reshape(1, ATT_WIDTH), w_out_b, final_g.reshape(1, d))


def _perm_cmp_means(kcm, b):
    nc = kcm.shape[1]
    ns = nc // 2
    assert ns <= MAX_SEL_LANES
    eo = kcm.reshape(b, ns, 2, 2 * KV_WIDTH).transpose(0, 2, 1, 3)
    eo = jnp.pad(eo, ((0, 0), (0, 0), (0, MAX_SEL_LANES - ns), (0, 0)))
    return eo.reshape(b, 2 * MAX_SEL_LANES, 2 * KV_WIDTH)


def _prompt_layer(x, mod, lw, final_g, apply_final):
    b, t, d = x.shape
    shift, scale, gate = (mod[:, None, i * d:(i + 1) * d] for i in range(3))
    tt = 256
    assert t % tt == 0 and t % SSD_CHUNK == 0 and t >= WINDOW
    gpb = t // tt
    x3 = x.reshape(b * gpb, tt, d)
    zs, xbc, q, za, sm, kcm, kvt_c, kvt_s, kvt_w, ka_s, vat_s, kw, vat_w = _inproj_prompt(
        x, scale, shift, lw["norm_g"], lw["w_big"], lw["w_kvt"], tt)
    r = lambda a: a.reshape(b, t, a.shape[-1])
    zs, xbc, q, za, sm, ka_s, kw = map(r, (zs, xbc, q, za, sm, ka_s, kw))
    kcm = kcm.reshape(b, t // CMP_BLOCK, 2 * KV_WIDTH)

    conv0 = jnp.zeros((b, CONV_WIDTH - 1, CONV_CH), F32)
    h0 = jnp.zeros((b, SSD_HEADS, HEAD_DIM, D_STATE), F32)
    y_ssd, ssm_new = _ssd(xbc, sm, zs, conv0, h0, lw["conv_w"], lw["conv_b"], lw["dt_bias"], lw["a_log"],
                          lw["d_skip"], lw["ssd_norm_g"])

    o_c, bias = _cmp_prompt(q, _perm_cmp_means(kcm, b), sm, 128)
    tq = 256
    o_s = _slc_prompt(q, bias, ka_s, vat_s, sm, tq)
    o_w = _win_prompt(q, kw, vat_w, sm, tq)

    g3 = lambda a: a.reshape(b * gpb, tt, a.shape[-1])
    y3 = _out(x3, g3(o_c), g3(o_s), g3(o_w), g3(za), g3(y_ssd), gate, lw["att_norm_g"], lw["w_out_b"],
              final_g, nbk=1, groups_per_mod=gpb)
    assert apply_final
    kv6 = lambda a: a.reshape(b, 2, KV_HEADS, HEAD_DIM, a.shape[-1]).transpose(0, 4, 1, 2, 3)
    outs = (kv6(kvt_c), kv6(kvt_s), kv6(kvt_w[:, :, t - min(WINDOW, t):]), xbc[:, t - (CONV_WIDTH - 1):], ssm_new)
    return y3.reshape(b, t, d), outs


def _sample_layer(x, mod, lw, final_g, pool_c, pool_s, win_buf, conv_buf, ssm, page_table, apply_final):
    bs, tn, d = x.shape
    shift, scale, gate = (mod[:, None, i * d:(i + 1) * d] for i in range(3))
    n_pages = page_table.shape[1]
    past = n_pages * PAGE_SIZE
    nbk = 16
    n_pg = 16
    assert bs % nbk == 0 and tn % 8 == 0 and tn <= SEL_BLOCK and n_pages % n_pg == 0
    assert past // SEL_BLOCK <= MAX_SEL_LANES and past % SEL_BLOCK == 0
    pos = past + jnp.arange(tn, dtype=jnp.int32)
    tabs = tuple(jnp.tile(tb, (nbk, 1)) for tb in _rope_tables(pos))
    zs, xbc, q, za, sm, kvc, kvs, kvw = _inproj_sample(
        x, scale, shift, lw["norm_g"], lw["w_big"], tabs, nbk=nbk)

    y_ssd, ssm_new = _ssd(xbc, sm, zs, conv_buf, ssm, lw["conv_w"], lw["conv_b"], lw["dt_bias"], lw["a_log"],
                          lw["d_skip"], lw["ssd_norm_g"])

    npad = LANES
    to_rows = lambda p: p.transpose(0, 2, 3, 4, 1).reshape(p.shape[0], 2 * KV_WIDTH, PAGE_SIZE)
    o_c, bias = _cmp_sample(page_table, to_rows(pool_c), q, sm)
    kvs_pad = jnp.pad(kvs, ((0, 0), (0, npad - tn), (0, 0)))
    o_s = _slc_sample(page_table, to_rows(pool_s), q, bias, kvs_pad, sm, n_pg)
    wb = win_buf.shape[1]
    win2 = win_buf.reshape(bs, wb, 2 * KV_WIDTH)
    kvw_pad = jnp.pad(kvw, ((0, 0), (0, npad - tn), (0, 0)))
    o_w = _win_sample(q, win2, kvw_pad, sm, past)

    y = _out(x, o_c, o_s, o_w, za, y_ssd, gate, lw["att_norm_g"], lw["w_out_b"], final_g,
             nbk=nbk, groups_per_mod=None)
    assert apply_final
    kv6 = lambda a: a.reshape(bs, a.shape[1], 2, KV_HEADS, HEAD_DIM)
    kv_w_all = jnp.concatenate([win2, kvw], axis=1)
    win_new = kv_w_all[:, kv_w_all.shape[1] - min(WINDOW, past + tn):]
    conv_new = jnp.concatenate([conv_buf, xbc], axis=1)[:, tn:]
    outs = (kv6(kvc), kv6(kvs), kv6(win_new), conv_new, ssm_new)
    return y, outs


def kernel(x_prompt, x_sample, cache_cmp_kv, cache_slc_kv, state_win_kv, state_conv, state_ssm, page_table,
           c_prompt, c_sample, w_ada, b_ada, norm_g, w_in, conv_w, conv_b, dt_bias, a_log, d_skip,
           ssd_norm_g, att_norm_g, w_out, final_g):
    depth = w_ada.shape[0]
    assert depth == 1
    n_prompt = c_prompt.shape[0]
    xp, xs = x_prompt, x_sample
    out_p, out_s = [], []
    for l in range(depth):
        w_big, w_kvt = _rearrange_w_in(w_in[l])
        lw = dict(norm_g=norm_g[l], w_big=w_big, w_kvt=w_kvt, conv_w=conv_w[l], conv_b=conv_b[l],
                  dt_bias=dt_bias[l], a_log=a_log[l], d_skip=d_skip[l], ssd_norm_g=ssd_norm_g[l],
                  att_norm_g=att_norm_g[l], w_out_b=w_out[l].astype(BF16))
        mod = _mod(jnp.concatenate([c_prompt, c_sample], axis=0), w_ada[l], b_ada[l])
        last = l == depth - 1
        xp, op = _prompt_layer(xp, mod[:n_prompt], lw, final_g, last)
        xs, os_ = _sample_layer(xs, mod[n_prompt:], lw, final_g, cache_cmp_kv[l], cache_slc_kv[l],
                                state_win_kv[l], state_conv[l], state_ssm[l], page_table, last)
        out_p.append(op)
        out_s.append(os_)
    sp = [jnp.stack([o[k] for o in out_p]) for k in range(5)]
    sd = [jnp.stack([o[k] for o in out_s]) for k in range(5)]
    return (xp, xs, sp[0], sp[1], sp[2], sp[3], sp[4], sd[0], sd[1], sd[2], sd[3], sd[4])
```

```python
import functools

import jax
import jax.numpy as jnp
from jax import lax
from jax.experimental import pallas as pl
from jax.experimental.pallas import tpu as pltpu

F32 = jnp.float32
BF16 = jnp.bfloat16

HEAD_DIM = 64
SSD_HEADS = 8
SSD_WIDTH = SSD_HEADS * HEAD_DIM
SSD_GROUPS = 2
D_STATE = 128
CONV_WIDTH = 4
CONV_CH = SSD_WIDTH + 2 * SSD_GROUPS * D_STATE
SSD_CHUNK = 256
ATT_HEADS = 8
ATT_WIDTH = ATT_HEADS * HEAD_DIM
KV_HEADS = 2
GQA_GROUP = ATT_HEADS // KV_HEADS
KV_WIDTH = KV_HEADS * HEAD_DIM
CMP_BLOCK = 32
SEL_BLOCK = 64
TOP_N = 16
N_LOCAL_BLOCKS = 2
WINDOW = 512
N_BRANCH = 3
ROT_DIM = HEAD_DIM // 4
ROPE_THETA = 500000.0
PAGE_SIZE = 128
EPS = 1e-6
COL_SIZES = (SSD_WIDTH, CONV_CH, SSD_HEADS, ATT_WIDTH, 2 * KV_WIDTH, 2 * KV_WIDTH, 2 * KV_WIDTH,
             N_BRANCH * ATT_HEADS, ATT_WIDTH)

LANES = 128
MAX_SEL_LANES = LANES
NEG = -1e30
SCALE = HEAD_DIM ** -0.5
SCALE_LOG2 = SCALE * 1.4426950408889634
GATE_LANE0 = SSD_HEADS
BF16_SUBLANES = 16
VAT_ROWS = HEAD_DIM + BF16_SUBLANES
VMEM_LIMIT = 56 * 1024 * 1024

_O_ZS = 0
_O_XBC = _O_ZS + SSD_WIDTH
_O_Q = _O_XBC + CONV_CH
_O_KC = _O_Q + ATT_WIDTH
_O_KS = _O_KC + 2 * KV_WIDTH
_O_KW = _O_KS + 2 * KV_WIDTH
_O_ZA = _O_KW + 2 * KV_WIDTH
_O_SM = _O_ZA + ATT_WIDTH
_W_COLS = _O_SM + LANES


def _cparams(sem):
    return pltpu.CompilerParams(dimension_semantics=sem, vmem_limit_bytes=VMEM_LIMIT)


def _silu(v):
    return v * jax.nn.sigmoid(v)


def _mod_kernel(c_ref, w_ref, b_ref, o_ref):
    a = _silu(c_ref[...])
    o_ref[...] = jnp.dot(a, w_ref[...], preferred_element_type=F32,
                         precision=lax.Precision.HIGHEST) + b_ref[...]


def _mod(c, w_ada, b_ada):
    n, d = c.shape
    cols = w_ada.shape[1]
    tn = d
    assert cols % tn == 0
    return pl.pallas_call(
        _mod_kernel,
        grid=(cols // tn,),
        in_specs=[pl.BlockSpec((n, d), lambda j: (0, 0)),
                  pl.BlockSpec((d, tn), lambda j: (0, j)),
                  pl.BlockSpec((1, tn), lambda j: (0, j))],
        out_specs=pl.BlockSpec((n, tn), lambda j: (0, j)),
        out_shape=jax.ShapeDtypeStruct((n, cols), F32),
        compiler_params=_cparams(("arbitrary",)),
        name="mod",
    )(c, w_ada, b_ada.reshape(1, cols))


def _rope128(v, rc, ra, rb):
    half = ROT_DIM // 2
    return v * rc + pltpu.roll(v, LANES - half, 1) * ra + pltpu.roll(v, half, 1) * rb


def _modulated_norm(x_ref, sc_ref, sh_ref, g_ref):
    nbk, tt, d = x_ref.shape
    x = x_ref[...]
    ms = jnp.mean(x * x, axis=-1, keepdims=True)
    y = (x * lax.rsqrt(ms + EPS)) * g_ref[...].reshape(1, 1, d)
    h = y * (1.0 + sc_ref[...]) + sh_ref[...]
    return h.reshape(nbk * tt, d).astype(BF16)


def _inproj_common(hb, w_ref, rc, ra, rb, zs_ref, xbc_ref, q_ref, za_ref, sm_ref, consume_ssd_inputs=None):
    nbk, tt, _ = q_ref.shape

    def proj(lo, n):
        return jnp.dot(hb, w_ref[:, lo:lo + n], preferred_element_type=F32)

    zs = proj(_O_ZS, SSD_WIDTH)
    xbc = proj(_O_XBC, CONV_CH)
    sm = proj(_O_SM, LANES)
    if zs_ref is not None:
        zs_ref[...] = zs.reshape(nbk, tt, SSD_WIDTH)
    if xbc_ref is not None:
        xbc_ref[...] = xbc.reshape(nbk, tt, CONV_CH)
    if consume_ssd_inputs is not None:
        consume_ssd_inputs(zs, xbc, sm)
    za_ref[...] = proj(_O_ZA, ATT_WIDTH).reshape(nbk, tt, ATT_WIDTH)
    sm_ref[...] = sm.reshape(nbk, tt, LANES)
    qraw = proj(_O_Q, ATT_WIDTH)
    q = jnp.concatenate([_rope128(qraw[:, LANES * j:LANES * (j + 1)], rc, ra, rb)
                         for j in range(ATT_WIDTH // LANES)], axis=1)
    q_ref[...] = q.reshape(nbk, tt, ATT_WIDTH)
    return proj, zs, xbc, sm


def _inproj_sample_kernel(x_ref, sc_ref, sh_ref, g_ref, w_ref, rc_ref, ra_ref, rb_ref,
                          zs_ref, xbc_ref, q_ref, za_ref, sm_ref, kvc_ref, kvs_ref, kvw_ref):
    nbk, tt, _ = x_ref.shape
    hb = _modulated_norm(x_ref, sc_ref, sh_ref, g_ref)
    rc, ra, rb = rc_ref[...], ra_ref[...], rb_ref[...]
    proj = _inproj_common(hb, w_ref, rc, ra, rb, zs_ref, xbc_ref, q_ref, za_ref, sm_ref)[0]
    for off, ref in ((_O_KC, kvc_ref), (_O_KS, kvs_ref), (_O_KW, kvw_ref)):
        u = proj(off, 2 * KV_WIDTH)
        kv = jnp.concatenate([_rope128(u[:, :KV_WIDTH], rc, ra, rb), u[:, KV_WIDTH:]], axis=1)
        ref[...] = kv.reshape(nbk, tt, 2 * KV_WIDTH)


def _rope_rows(k, cos, sin):
    half = ROT_DIM // 2
    x1, x2 = k[:half], k[half:ROT_DIM]
    return jnp.concatenate([x1 * cos - x2 * sin, x2 * cos + x1 * sin, k[ROT_DIM:]], axis=0)


def _inproj_prompt_kernel(x_ref, sc_ref, sh_ref, g_ref, w_ref, wt_ref, rc_ref, ra_ref, rb_ref, cos_ref, sin_ref,
                          cw_ref, cb_ref, dtb_ref, dtbc_ref, alog_ref, alogc_ref, dsk_ref, ng_ref,
                          q_ref, za_ref, sm_ref, kcm_ref,
                          kvtc_ref, kvts_ref, kvtw_ref, kas_ref, vats_ref, kw_ref, vatw_ref,
                          y_ref, hout_ref, xlast_ref, xp_ref, h_ref, *, groups_per_seq):
    _, tt, _ = x_ref.shape
    hb = _modulated_norm(x_ref, sc_ref, sh_ref, g_ref)
    rc, ra, rb = rc_ref[...], ra_ref[...], rb_ref[...]
    w2 = 2 * KV_WIDTH

    def ssd_branch(zs, xbc, sm):
        c = pl.program_id(0) % groups_per_seq
        pad = xp_ref.shape[0] - tt
        xlast_ref[0] = xbc[tt - pad:]

        def init():
            xp_ref[0:pad, :] = jnp.zeros((pad, CONV_CH), F32)
            h_ref[...] = jnp.zeros(h_ref.shape, F32)

        smt = lax.dot_general(wt_ref[N_BRANCH * w2:, :], hb, (((1,), (1,)), ((), ())),
                              preferred_element_type=F32)
        _ssd_chunk(c == 0, c == groups_per_seq - 1, init, xbc, sm, smt, zs, cw_ref, cb_ref, dtb_ref, dtbc_ref,
                   alog_ref, alogc_ref, dsk_ref, ng_ref, y_ref, hout_ref, xp_ref, h_ref)

    proj = _inproj_common(hb, w_ref, rc, ra, rb, None, None, q_ref, za_ref, sm_ref, ssd_branch)[0]

    u = proj(_O_KC, 2 * KV_WIDTH)
    kv = jnp.concatenate([_rope128(u[:, :KV_WIDTH], rc, ra, rb), u[:, KV_WIDTH:]], axis=1)
    nblk = tt // CMP_BLOCK
    means = jnp.sum(kv.reshape(nblk, CMP_BLOCK, 2 * KV_WIDTH), axis=1) * (1.0 / CMP_BLOCK)
    kcm_ref[...] = means.reshape(1, nblk, 2 * KV_WIDTH)

    ti = pl.program_id(0) % groups_per_seq
    blk = (ti * tt + lax.broadcasted_iota(jnp.int32, (tt, MAX_SEL_LANES), 0)) // SEL_BLOCK
    onehot = jnp.where(blk == lax.broadcasted_iota(jnp.int32, (tt, MAX_SEL_LANES), 1), 1.0, 0.0)
    ks = _rope128(proj(_O_KS, KV_WIDTH), rc, ra, rb)
    kas_ref[0] = jnp.concatenate([onehot, ks], axis=1).astype(BF16)
    kw_ref[0] = _rope128(proj(_O_KW, KV_WIDTH), rc, ra, rb).astype(BF16)

    ut = lax.dot_general(wt_ref[:N_BRANCH * w2, :], hb, (((1,), (1,)), ((), ())), preferred_element_type=F32)
    cos, sin = cos_ref[...], sin_ref[...]
    slabs = []
    for br in range(N_BRANCH):
        s = ut[br * w2:(br + 1) * w2]
        ks = [_rope_rows(s[h * HEAD_DIM:(h + 1) * HEAD_DIM], cos, sin) for h in range(KV_HEADS)]
        slabs.append(jnp.concatenate(ks + [s[KV_WIDTH:]], axis=0))
    kvtc_ref[0] = slabs[0]
    kvts_ref[0] = slabs[1]
    kvtw_ref[0] = slabs[2]
    ones = jnp.ones((VAT_ROWS - HEAD_DIM, tt), F32)
    for slab, vat_ref in ((slabs[1], vats_ref), (slabs[2], vatw_ref)):
        for h in range(KV_HEADS):
            v = slab[KV_WIDTH + h * HEAD_DIM:KV_WIDTH + (h + 1) * HEAD_DIM]
            vat_ref[0, h] = jnp.concatenate([v, ones], axis=0).astype(BF16)


def _mod_spec(nbk, d, groups_per_mod):
    if groups_per_mod is None:
        return pl.BlockSpec((nbk, 1, d), lambda g: (g, 0, 0))
    return pl.BlockSpec((1, 1, d), lambda g: (g // groups_per_mod, 0, 0))


def _inproj_sample(x3, scale, shift, norm_g, w_big, rope_tabs, *, nbk):
    g_total, tt, d = x3.shape
    m = nbk * tt
    mod_spec = _mod_spec(nbk, d, None)
    tab_spec = pl.BlockSpec((m, LANES), lambda g: (0, 0))

    def tok_spec(c):
        return pl.BlockSpec((nbk, tt, c), lambda g: (g, 0, 0))

    widths = (SSD_WIDTH, CONV_CH, ATT_WIDTH, ATT_WIDTH, LANES, 2 * KV_WIDTH, 2 * KV_WIDTH, 2 * KV_WIDTH)
    return pl.pallas_call(
        _inproj_sample_kernel,
        grid=(g_total // nbk,),
        in_specs=[tok_spec(d), mod_spec, mod_spec,
                  pl.BlockSpec((1, d), lambda g: (0, 0)),
                  pl.BlockSpec((d, _W_COLS), lambda g: (0, 0)),
                  tab_spec, tab_spec, tab_spec],
        out_specs=[tok_spec(c) for c in widths],
        out_shape=[jax.ShapeDtypeStruct((g_total, tt, c), F32) for c in widths],
        compiler_params=_cparams(("arbitrary",)),
        name="inproj_sample",
    )(x3, scale, shift, norm_g.reshape(1, d), w_big, *rope_tabs)


def _inproj_prompt(x, scale, shift, norm_g, w_big, w_kvt, ssd_params, tt):
    b, t, d = x.shape
    assert tt == min(SSD_CHUNK, t)
    gps = t // tt
    pad = 8
    steps = b * gps
    x3 = x.reshape(steps, tt, d)
    pos = jnp.arange(t, dtype=jnp.int32)
    tabs = _rope_tables(pos)
    cos_t, sin_t = _rope_angles(pos)
    mod_spec = _mod_spec(1, d, gps)
    tab_spec = pl.BlockSpec((tt, LANES), lambda g: (g % gps, 0))
    ang_spec = pl.BlockSpec((ROT_DIM // 2, tt), lambda g: (0, g % gps))

    def tok_spec(c):
        return pl.BlockSpec((1, tt, c), lambda g: (g, 0, 0))

    def row_spec(r):
        return pl.BlockSpec((1, r, tt), lambda g: (g // gps, 0, g % gps))

    vat_spec = pl.BlockSpec((1, KV_HEADS, VAT_ROWS, tt), lambda g: (g // gps, 0, 0, g % gps))
    nblk = tt // CMP_BLOCK
    w2 = 2 * KV_WIDTH
    tok_widths = (ATT_WIDTH, ATT_WIDTH, LANES)
    state_spec = pl.BlockSpec((1, SSD_HEADS // 2, 2 * HEAD_DIM, D_STATE), lambda g: (g // gps, 0, 0, 0))
    out_specs = ([tok_spec(c) for c in tok_widths]
                 + [pl.BlockSpec((1, nblk, w2), lambda g: (g, 0, 0))]
                 + [row_spec(w2)] * 3
                 + [tok_spec(MAX_SEL_LANES + KV_WIDTH), vat_spec, tok_spec(KV_WIDTH), vat_spec]
                 + [tok_spec(SSD_WIDTH), state_spec, pl.BlockSpec((1, pad, CONV_CH), lambda g: (g // gps, 0, 0))])
    vat_shape = jax.ShapeDtypeStruct((b, KV_HEADS, VAT_ROWS, t), BF16)
    out_shape = ([jax.ShapeDtypeStruct((steps, tt, c), F32) for c in tok_widths]
                 + [jax.ShapeDtypeStruct((steps, nblk, w2), F32)]
                 + [jax.ShapeDtypeStruct((b, w2, t), F32)] * 3
                 + [jax.ShapeDtypeStruct((steps, tt, MAX_SEL_LANES + KV_WIDTH), BF16), vat_shape,
                    jax.ShapeDtypeStruct((steps, tt, KV_WIDTH), BF16), vat_shape]
                 + [jax.ShapeDtypeStruct((steps, tt, SSD_WIDTH), BF16),
                    jax.ShapeDtypeStruct((b, SSD_HEADS // 2, 2 * HEAD_DIM, D_STATE), F32),
                    jax.ShapeDtypeStruct((b, pad, CONV_CH), F32)])
    return pl.pallas_call(
        functools.partial(_inproj_prompt_kernel, groups_per_seq=gps),
        grid=(steps,),
        in_specs=[tok_spec(d), mod_spec, mod_spec,
                  pl.BlockSpec((1, d), lambda g: (0, 0)),
                  pl.BlockSpec((d, _W_COLS), lambda g: (0, 0)),
                  pl.BlockSpec(w_kvt.shape, lambda g: (0, 0)),
                  tab_spec, tab_spec, tab_spec, ang_spec, ang_spec]
                 + [pl.BlockSpec(p.shape, lambda g: (0, 0)) for p in ssd_params],
        out_specs=out_specs,
        out_shape=out_shape,
        scratch_shapes=[pltpu.VMEM((tt + pad, CONV_CH), F32),
                        pltpu.VMEM((SSD_HEADS // 2, 2 * HEAD_DIM, D_STATE), F32)],
        compiler_params=_cparams(("arbitrary",)),
        name="inproj_ssd_prompt",
    )(x3, scale, shift, norm_g.reshape(1, d), w_big, w_kvt, *tabs, cos_t, sin_t, *ssd_params)


def _rope_angles(pos):
    half = ROT_DIM // 2
    inv_freq = ROPE_THETA ** (-jnp.arange(half, dtype=F32) * 2.0 / ROT_DIM)
    ang = pos.astype(F32)[:, None] * inv_freq[None, :]
    return jnp.cos(ang).T, jnp.sin(ang).T


def _rope_tables(pos):
    half = ROT_DIM // 2
    inv_freq = ROPE_THETA ** (-jnp.arange(half, dtype=F32) * 2.0 / ROT_DIM)
    ang = pos.astype(F32)[:, None] * inv_freq[None, :]
    cos, sin = jnp.cos(ang), jnp.sin(ang)
    n = pos.shape[0]
    one = jnp.ones((n, HEAD_DIM - ROT_DIM), F32)
    zero_h = jnp.zeros((n, half), F32)
    zero_r = jnp.zeros((n, HEAD_DIM - ROT_DIM), F32)
    rc = jnp.concatenate([cos, cos, one], axis=1)
    ra = jnp.concatenate([-sin, zero_h, zero_r], axis=1)
    rb = jnp.concatenate([zero_h, sin, zero_r], axis=1)
    rep = LANES // HEAD_DIM
    return tuple(jnp.tile(t, (1, rep)) for t in (rc, ra, rb))


def _rearrange_w_in(w_in):
    parts, o = [], 0
    for n in COL_SIZES:
        parts.append(w_in[:, o:o + n])
        o += n
    z_s, xbc, dt, q, kc, ks, kw, g, z_a = parts
    d = w_in.shape[0]
    small = jnp.concatenate([dt, g, jnp.zeros((d, LANES - SSD_HEADS - N_BRANCH * ATT_HEADS), w_in.dtype)], axis=1)
    w_big = jnp.concatenate([z_s, xbc, q, kc, ks, kw, z_a, small], axis=1).astype(BF16)
    w_kvt = jnp.concatenate([kc, ks, kw, small], axis=1).T.astype(BF16)
    return w_big, w_kvt


def _pair_cols(mat, p, shape):
    lane = lax.broadcasted_iota(jnp.int32, shape, 1)
    a = jnp.broadcast_to(mat[:, 2 * p:2 * p + 1], shape)
    b = jnp.broadcast_to(mat[:, 2 * p + 1:2 * p + 2], shape)
    return jnp.where(lane < HEAD_DIM, a, b)


def _ssd_kernel(xbc_ref, sm_ref, smt_ref, zs_ref, conv0_ref, h0_ref, *rest):
    c = pl.program_id(1)
    xp_ref, h_ref = rest[-2:]
    pad = xp_ref.shape[0] - xbc_ref.shape[1]

    def init():
        xp_ref[0:pad, :] = conv0_ref[0]
        h_ref[...] = h0_ref[0]

    _ssd_chunk(c == 0, c == pl.num_programs(1) - 1, init, xbc_ref[0], sm_ref[0], smt_ref[0], zs_ref[0], *rest)


def _ssd_chunk(first, last, init, xbc, sm, smt, zs, cw_ref, cb_ref, dtb_ref, dtbc_ref,
               alog_ref, alogc_ref, dsk_ref, ng_ref, y_ref, hout_ref, xp_ref, h_ref):
    L = xbc.shape[0]
    pad = xp_ref.shape[0] - L
    hp = jnp.float32

    @pl.when(first)
    def _():
        init()

    xp_ref[pad:pad + L, :] = xbc
    cw = cw_ref[...]
    conv = cb_ref[...]
    for w in range(CONV_WIDTH):
        o = pad - (CONV_WIDTH - 1) + w
        conv = conv + xp_ref[o:o + L, :] * cw[w:w + 1, :]
    halo = xp_ref[L:L + pad, :]
    xp_ref[0:pad, :] = halo

    u = _silu(conv)
    xs = u[:, :SSD_WIDTH]
    gw = SSD_GROUPS * D_STATE
    bm = u[:, SSD_WIDTH:SSD_WIDTH + gw]
    cm = u[:, SSD_WIDTH + gw:]

    dt = jax.nn.softplus(sm + dtb_ref[...])
    dta = dt * (-jnp.exp(alog_ref[...]))
    row = lax.broadcasted_iota(jnp.int32, (L, L), 0)
    col = lax.broadcasted_iota(jnp.int32, (L, L), 1)
    causal = row >= col
    la = jnp.dot(causal.astype(hp), dta, preferred_element_type=hp,
                 precision=lax.Precision.HIGHEST)
    dtt = jax.nn.softplus(smt + dtbc_ref[...])
    dtat = dtt * (-jnp.exp(alogc_ref[...]))
    lat = jnp.dot(dtat, (row <= col).astype(hp), preferred_element_type=hp,
                  precision=lax.Precision.HIGHEST)
    la_last = la[L - 1:L, :]
    ela = jnp.exp(la)
    te = jnp.exp(la_last - la)
    cdec = jnp.exp(la_last)

    lane = lax.broadcasted_iota(jnp.int32, (L, LANES), 1)
    srow = lax.broadcasted_iota(jnp.int32, (LANES, LANES), 0)
    hpg = SSD_HEADS // SSD_GROUPS
    ys = []
    for g in range(SSD_GROUPS):
        bm_g = bm[:, g * D_STATE:(g + 1) * D_STATE]
        cm_g = cm[:, g * D_STATE:(g + 1) * D_STATE].astype(BF16)
        bm_gb = bm_g.astype(BF16)
        cb = lax.dot_general(cm_g, bm_gb, (((1,), (1,)), ((), ())), preferred_element_type=hp)
        for pp in range(hpg // 2):
            p = g * (hpg // 2) + pp
            xs_p = xs[:, LANES * p:LANES * (p + 1)]
            xdt = xs_p * _pair_cols(dt, p, (L, LANES))
            xdt_b = xdt.astype(BF16)
            yd = []
            for r in (2 * p, 2 * p + 1):
                seg = la[:, r:r + 1] - lat[r:r + 1, :]
                dec = jnp.where(causal, jnp.exp(jnp.where(causal, seg, 0.0)), 0.0)
                yd.append(jnp.dot((cb * dec).astype(BF16), xdt_b, preferred_element_type=hp))
            y_diag = jnp.where(lane < HEAD_DIM, yd[0], yd[1])
            h_p = h_ref[p]
            y_off = lax.dot_general(cm_g, h_p.astype(BF16), (((1,), (1,)), ((), ())),
                                    preferred_element_type=hp) * _pair_cols(ela, p, (L, LANES))
            ys.append(y_diag + y_off + dsk_ref[:, LANES * p:LANES * (p + 1)] * xs_p)
            xw = (xdt * _pair_cols(te, p, (L, LANES))).astype(BF16)
            st = lax.dot_general(xw, bm_gb, (((0,), (0,)), ((), ())), preferred_element_type=hp)
            cd = jnp.where(srow < HEAD_DIM,
                           jnp.broadcast_to(cdec[:, 2 * p:2 * p + 1], (LANES, LANES)),
                           jnp.broadcast_to(cdec[:, 2 * p + 1:2 * p + 2], (LANES, LANES)))
            h_ref[p] = h_p * cd + st

    y = jnp.concatenate(ys, axis=1)
    t = y * _silu(zs)
    ms = jnp.mean(t * t, axis=-1, keepdims=True)
    y_ref[0] = ((t * lax.rsqrt(ms + EPS)) * ng_ref[...]).astype(y_ref.dtype)

    del last
    hout_ref[0] = h_ref[...]


def _ssd(xbc, sm, zs, conv_state, h0, conv_w, conv_b, dt_bias, a_log, d_skip, norm_g):
    bn, t, _ = xbc.shape
    L = min(SSD_CHUNK, t)
    assert t % L == 0
    nc = t // L
    pad = 8
    smt = jnp.swapaxes(sm, 1, 2)
    conv0 = jnp.pad(conv_state, ((0, 0), (pad - (CONV_WIDTH - 1), 0), (0, 0)))
    hp2 = h0.reshape(bn, SSD_HEADS // 2, 2 * HEAD_DIM, D_STATE)
    params = _ssd_params(conv_w, conv_b, dt_bias, a_log, d_skip, norm_g)

    def full(shape):
        return pl.BlockSpec(shape, lambda b, c: tuple(0 for _ in shape))

    y, hout = pl.pallas_call(
        _ssd_kernel,
        grid=(bn, nc),
        in_specs=[pl.BlockSpec((1, L, CONV_CH), lambda b, c: (b, c, 0)),
                  pl.BlockSpec((1, L, LANES), lambda b, c: (b, c, 0)),
                  pl.BlockSpec((1, LANES, L), lambda b, c: (b, 0, c)),
                  pl.BlockSpec((1, L, SSD_WIDTH), lambda b, c: (b, c, 0)),
                  pl.BlockSpec((1, pad, CONV_CH), lambda b, c: (b, 0, 0)),
                  pl.BlockSpec((1, SSD_HEADS // 2, 2 * HEAD_DIM, D_STATE), lambda b, c: (b, 0, 0, 0)),
                  ] + [full(p.shape) for p in params],
        out_specs=[pl.BlockSpec((1, L, SSD_WIDTH), lambda b, c: (b, c, 0)),
                   pl.BlockSpec((1, SSD_HEADS // 2, 2 * HEAD_DIM, D_STATE), lambda b, c: (b, 0, 0, 0))],
        out_shape=[jax.ShapeDtypeStruct((bn, t, SSD_WIDTH), BF16),
                   jax.ShapeDtypeStruct((bn, SSD_HEADS // 2, 2 * HEAD_DIM, D_STATE), F32)],
        scratch_shapes=[pltpu.VMEM((L + pad, CONV_CH), F32),
                        pltpu.VMEM((SSD_HEADS // 2, 2 * HEAD_DIM, D_STATE), F32)],
        compiler_params=_cparams(("arbitrary", "arbitrary")),
        name="ssd",
    )(xbc, sm, smt, zs, conv0, hp2, *params)
    return y, hout.reshape(bn, SSD_HEADS, HEAD_DIM, D_STATE)


def _ssd_params(conv_w, conv_b, dt_bias, a_log, d_skip, norm_g):
    zpad = jnp.zeros((LANES - SSD_HEADS,), F32)
    dtb = jnp.concatenate([dt_bias.astype(F32), zpad])
    alog = jnp.concatenate([a_log.astype(F32), zpad])
    dsk = jnp.repeat(d_skip.astype(F32), HEAD_DIM).reshape(1, SSD_WIDTH)
    return (conv_w, conv_b.reshape(1, CONV_CH), dtb.reshape(1, LANES), dtb.reshape(LANES, 1),
            alog.reshape(1, LANES), alog.reshape(LANES, 1), dsk, norm_g.reshape(1, SSD_WIDTH))


def _half_mask(shape, kvh):
    lane = lax.broadcasted_iota(jnp.int32, shape, 1)
    return (lane >= HEAD_DIM) if kvh else (lane < HEAD_DIM)


def _stack_heads(q, kvh):
    tq = q.shape[0]
    keep = _half_mask((tq, LANES), kvh)
    blocks = []
    for g in range(GQA_GROUP):
        h = kvh * GQA_GROUP + g
        v = q[:, LANES * (h // 2):LANES * (h // 2 + 1)]
        if (h % 2) != kvh:
            v = pltpu.roll(v, HEAD_DIM, 1)
        blocks.append(jnp.where(keep, v, 0.0))
    return jnp.concatenate(blocks, axis=0)


def _unstack_heads(o_by_kvh, gates, branch, src_half_is_kvh=True):
    tq = gates.shape[0]
    lane = lax.broadcasted_iota(jnp.int32, (tq, LANES), 1)
    blocks = []
    for h in range(ATT_HEADS):
        kvh, g = divmod(h, GQA_GROUP)
        v = o_by_kvh[kvh][g * tq:(g + 1) * tq, :]
        gl = GATE_LANE0 + branch * ATT_HEADS + h
        v = v * gates[:, gl:gl + 1]
        src_half = kvh if src_half_is_kvh else 0
        if (h % 2) != src_half:
            v = pltpu.roll(v, HEAD_DIM, 1)
        blocks.append(v)
    outs = [jnp.where(lane < HEAD_DIM, blocks[2 * j], blocks[2 * j + 1]) for j in range(ATT_HEADS // 2)]
    return jnp.concatenate(outs, axis=1)


def _tile_rows(v, n):
    return jnp.concatenate([v] * n, axis=0)


def _flash_update(s, vaug, m_ref, acc_ref, kvh):
    m_prev = m_ref[kvh]
    m_new = jnp.maximum(m_prev, jnp.max(s, axis=1, keepdims=True))
    alpha = jnp.exp(m_prev - m_new)
    p = jnp.exp(s - m_new[:, 0:1])
    pv = jnp.dot(p.astype(BF16), vaug, preferred_element_type=F32)
    acc_ref[kvh] = acc_ref[kvh] * jnp.concatenate([alpha, alpha], axis=1) + pv
    m_ref[kvh] = m_new


def _flash_update_t(s, vaug_t, m_ref, acc_ref, kvh):
    m_prev = m_ref[kvh]
    m_new = jnp.maximum(m_prev, jnp.max(s, axis=1, keepdims=True))
    alpha = jnp.exp(m_prev - m_new)
    p = jnp.exp(s - m_new[:, 0:1])
    pv = lax.dot_general(p.astype(BF16), vaug_t, (((1,), (1,)), ((), ())), preferred_element_type=F32)
    acc_ref[kvh] = acc_ref[kvh] * jnp.concatenate([alpha, alpha], axis=1) + pv
    m_ref[kvh] = m_new


def _flash_out(acc_ref, kvh):
    acc = acc_ref[kvh]
    return acc[:, :LANES] / acc[:, LANES:]


def _vaug(v01):
    return jnp.concatenate([v01.astype(BF16), jnp.ones(v01.shape, BF16)], axis=1)


def _topk_cols(score_t, rounds):
    nb = score_t.shape[0]
    ridx = lax.broadcasted_iota(jnp.int32, score_t.shape, 0)
    sel = jnp.zeros(score_t.shape, F32)
    cur = score_t
    for _ in range(rounds):
        mx = jnp.max(cur, axis=0, keepdims=True)
        idx = jnp.min(jnp.where(cur == mx, ridx, nb), axis=0, keepdims=True)
        hit = ridx == idx
        sel = jnp.where(hit & (mx > -jnp.inf), 1.0, sel)
        cur = jnp.where(hit, -jnp.inf, cur)
    return sel


def _topk_rows(score, rounds):
    nb = score.shape[1]
    lidx = lax.broadcasted_iota(jnp.int32, score.shape, 1)
    ahead = jnp.zeros(score.shape, jnp.int32)
    for r in range(1, nb):
        other = pltpu.roll(score, r, 1)
        wins = (other > score) | ((other == score) & (lidx >= r))
        ahead = ahead + jnp.where(wins, 1, 0)
    return jnp.where((ahead < rounds) & (score > -jnp.inf), 1.0, 0.0)


def _cmp_attend(q, kc01, vc01, pos):
    tq = q.shape[0]
    ncl = kc01.shape[0]
    rows = GQA_GROUP * tq
    lane = lax.broadcasted_iota(jnp.int32, (rows, ncl), 1)
    cblk = 2 * (lane % MAX_SEL_LANES) + lane // MAX_SEL_LANES
    c_end = (cblk + 1) * CMP_BLOCK - 1
    mask = c_end <= _tile_rows(pos, GQA_GROUP)
    kcb = kc01.astype(BF16)
    vcb = vc01.astype(BF16)
    outs, imps = [], []
    for kvh in range(KV_HEADS):
        qs = _stack_heads(q, kvh).astype(BF16)
        s = lax.dot_general(qs, kcb, (((1,), (1,)), ((), ())), preferred_element_type=F32)
        s = jnp.where(mask, s, -jnp.inf)
        mx = jnp.max(s, axis=1, keepdims=True)
        mx = jnp.where(mx > -jnp.inf, mx, 0.0)
        e = jnp.exp(s - mx)
        dsum = jnp.sum(e, axis=1, keepdims=True)
        p = e / jnp.where(dsum > 0, dsum, 1.0)
        outs.append(jnp.dot(p.astype(BF16), vcb, preferred_element_type=F32))
        imp = p[0:tq]
        for g in range(1, GQA_GROUP):
            imp = imp + p[g * tq:(g + 1) * tq]
        imps.append(imp[:, :MAX_SEL_LANES] + imp[:, MAX_SEL_LANES:])
    return outs, imps


def _cmp_prompt_kernel(q_ref, kc_ref, vc_ref, sm_ref, o_ref, bias_ref):
    tq = q_ref.shape[1]
    i = pl.program_id(1)
    pos = i * tq + lax.broadcasted_iota(jnp.int32, (tq, 1), 0)
    q = q_ref[0] * SCALE
    outs, imps = _cmp_attend(q, kc_ref[0], vc_ref[0], pos)
    gates = jax.nn.sigmoid(sm_ref[0])
    o_ref[0] = _unstack_heads(outs, gates, 0)
    blk = lax.broadcasted_iota(jnp.int32, (tq, MAX_SEL_LANES), 1)
    cur = pos // SEL_BLOCK
    valid = blk <= cur
    forced = (blk == 0) | ((cur - blk >= 0) & (cur - blk < N_LOCAL_BLOCKS))
    for kvh in range(KV_HEADS):
        score = jnp.where(valid & jnp.logical_not(forced), imps[kvh], -jnp.inf)
        sel_t = _topk_cols(score.T, TOP_N - 1 - N_LOCAL_BLOCKS)
        keep_t = jnp.where(forced & valid, 1.0, 0.0).T
        bias_ref[0, kvh] = jnp.where((sel_t > 0.5) | (keep_t > 0.5), 0.0, NEG).astype(BF16)


def _cmp_prompt(q, kcm_perm, sm, tq):
    b, t, _ = q.shape
    ncl = kcm_perm.shape[1]
    return pl.pallas_call(
        _cmp_prompt_kernel,
        grid=(b, t // tq),
        in_specs=[pl.BlockSpec((1, tq, ATT_WIDTH), lambda bb, i: (bb, i, 0)),
                  pl.BlockSpec((1, ncl, LANES), lambda bb, i: (bb, 0, 0)),
                  pl.BlockSpec((1, ncl, LANES), lambda bb, i: (bb, 0, 1)),
                  pl.BlockSpec((1, tq, LANES), lambda bb, i: (bb, i, 0))],
        out_specs=[pl.BlockSpec((1, tq, ATT_WIDTH), lambda bb, i: (bb, i, 0)),
                   pl.BlockSpec((1, KV_HEADS, MAX_SEL_LANES, tq), lambda bb, i: (bb, 0, 0, i))],
        out_shape=[jax.ShapeDtypeStruct((b, t, ATT_WIDTH), F32),
                   jax.ShapeDtypeStruct((b, KV_HEADS, MAX_SEL_LANES, t), BF16)],
        compiler_params=_cparams(("arbitrary", "arbitrary")),
        name="cmp_prompt",
    )(q, kcm_perm, kcm_perm, sm)


def _build_qaug(q, bias_ref, qaug_ref):
    for kvh in range(KV_HEADS):
        qs = _stack_heads(q, kvh).astype(BF16)
        bias = _tile_rows(bias_ref[0, kvh], GQA_GROUP)
        qaug_ref[kvh] = jnp.concatenate([qs, bias], axis=1)


def _head_rows(qt, h):
    blk = qt[h * HEAD_DIM:(h + 1) * HEAD_DIM]
    z = jnp.zeros_like(blk)
    return jnp.concatenate([blk, z] if h // GQA_GROUP == 0 else [z, blk], axis=0)


def _emit_heads(o_by_head, sm_ref, branch, o_ref):
    gates_t = jax.nn.sigmoid(sm_ref[0]).T
    cols = []
    for j in range(ATT_HEADS // 2):
        pair = []
        for h in (2 * j, 2 * j + 1):
            gl = GATE_LANE0 + branch * ATT_HEADS + h
            pair.append(o_by_head[h] * gates_t[gl:gl + 1, :])
        cols.append(jnp.concatenate(pair, axis=0).T)
    o_ref[0] = jnp.concatenate(cols, axis=1)


def _pipeline_order(n, lead):
    order = [("qk", h) for h in range(min(lead, n))]
    for h in range(n):
        order.append(("sm", h))
        if h + lead < n:
            order.append(("qk", h + lead))
        if h >= 1:
            order.append(("pv", h - 1))
    order.append(("pv", n - 1))
    return order


_SLC_ORDER = _pipeline_order(ATT_HEADS, 8)
_SLC_TILES_PER_ITER = 2


def _slc_prompt_kernel(q_ref, bias_ref, ka_ref, vat_ref, sm_ref, o_ref, qa_ref, m_ref, acc_ref):
    tq = q_ref.shape[1]
    tk = tq
    i = pl.program_id(1)
    qt = (q_ref[0] * SCALE_LOG2).T
    for h in range(ATT_HEADS):
        qa_ref[h] = jnp.concatenate([bias_ref[0, h // GQA_GROUP], _head_rows(qt, h).astype(BF16)], axis=0)
    m_ref[...] = jnp.full(m_ref.shape, NEG, F32)
    acc_ref[...] = jnp.zeros(acc_ref.shape, F32)

    def tiles(j0, n, masked):
        k0s = [pl.multiple_of((j0 + t) * tk, tk) for t in range(n)]
        kas = [ka_ref[0, pl.ds(k0, tk), :] for k0 in k0s]
        ss, ps, alphas = {}, {}, {}

        def qk(h):
            ss[h] = []
            for ka in kas:
                s = jnp.dot(ka, qa_ref[h], preferred_element_type=F32)
                if masked:
                    kofs = lax.broadcasted_iota(jnp.int32, (tk, tq), 0)
                    qofs = lax.broadcasted_iota(jnp.int32, (tk, tq), 1)
                    s = jnp.where(kofs <= qofs, s, NEG)
                ss[h].append(s)

        def sm(h):
            m_prev = m_ref[h]
            m_new = m_prev
            for s in ss[h]:
                m_new = jnp.maximum(m_new, jnp.max(s, axis=0, keepdims=True))
            alphas[h] = jnp.exp2(m_prev - m_new)
            ps[h] = [jnp.exp2(s - m_new).astype(BF16) for s in ss[h]]
            m_ref[h] = m_new

        def pv(h):
            acc = acc_ref[h] * alphas[h]
            for k0, p in zip(k0s, ps[h]):
                acc = acc + jnp.dot(vat_ref[0, h // GQA_GROUP, :, pl.ds(k0, tk)], p, preferred_element_type=F32)
            acc_ref[h] = acc

        for step in _SLC_ORDER:
            {"qk": qk, "sm": sm, "pv": pv}[step[0]](step[1])

    def body(jq, carry):
        tiles(_SLC_TILES_PER_ITER * jq, _SLC_TILES_PER_ITER, False)
        return carry

    lax.fori_loop(0, i // _SLC_TILES_PER_ITER, body, 0)
    done = (i // _SLC_TILES_PER_ITER) * _SLC_TILES_PER_ITER
    n = _SLC_TILES_PER_ITER // 2
    while n >= 1:
        take = ((i - done) // n) % 2 == 1

        @pl.when(take)
        def _(done=done, n=n):
            tiles(done, n, False)

        done = done + jnp.where(take, n, 0)
        n //= 2

    tiles(i, 1, True)
    outs = []
    for h in range(ATT_HEADS):
        acc = acc_ref[h]
        outs.append(acc[:HEAD_DIM] / acc[HEAD_DIM:HEAD_DIM + 1])
    _emit_heads(outs, sm_ref, 1, o_ref)


def _slc_prompt(q, bias_t, ka, vat, sm, tq):
    b, t, _ = q.shape
    kc = ka.shape[2]
    return pl.pallas_call(
        _slc_prompt_kernel,
        grid=(b, t // tq),
        in_specs=[pl.BlockSpec((1, tq, ATT_WIDTH), lambda bb, i: (bb, i, 0)),
                  pl.BlockSpec((1, KV_HEADS, MAX_SEL_LANES, tq), lambda bb, i: (bb, 0, 0, i)),
                  pl.BlockSpec((1, t, kc), lambda bb, i: (bb, 0, 0)),
                  pl.BlockSpec((1, KV_HEADS, VAT_ROWS, t), lambda bb, i: (bb, 0, 0, 0)),
                  pl.BlockSpec((1, tq, LANES), lambda bb, i: (bb, i, 0))],
        out_specs=pl.BlockSpec((1, tq, ATT_WIDTH), lambda bb, i: (bb, i, 0)),
        out_shape=jax.ShapeDtypeStruct((b, t, ATT_WIDTH), F32),
        scratch_shapes=[pltpu.VMEM((ATT_HEADS, kc, tq), BF16),
                        pltpu.VMEM((ATT_HEADS, 1, tq), F32),
                        pltpu.VMEM((ATT_HEADS, VAT_ROWS, tq), F32)],
        compiler_params=_cparams(("arbitrary", "arbitrary")),
        name="slc_prompt",
    )(q, bias_t, ka, vat, sm)


def _win_prompt_kernel(q_ref, sm_ref, *refs, n_prev):
    nt = n_prev + 1
    k_refs, vat_refs, o_ref = refs[:nt], refs[nt:2 * nt], refs[2 * nt]
    tq = q_ref.shape[1]
    i = pl.program_id(1)
    kofs = lax.broadcasted_iota(jnp.int32, (tq, tq), 0)
    qofs = lax.broadcasted_iota(jnp.int32, (tq, tq), 1)
    qt = (q_ref[0] * SCALE_LOG2).T
    scores = []
    for h in range(ATT_HEADS):
        qh = _head_rows(qt, h).astype(BF16)
        ss = []
        for n in range(nt):
            s = jnp.dot(k_refs[n][0], qh, preferred_element_type=F32)
            back = n_prev - n
            ok = i >= back
            if n == 0:
                ok = ok & (kofs >= qofs)
            if back == 0:
                ok = kofs <= qofs
            ss.append(jnp.where(ok, s, NEG))
        scores.append(ss)
    probs = []
    for ss in scores:
        mx = ss[0].max(axis=0, keepdims=True)
        for s in ss[1:]:
            mx = jnp.maximum(mx, s.max(axis=0, keepdims=True))
        probs.append([jnp.exp2(s - mx).astype(BF16) for s in ss])
    outs = []
    for h in range(ATT_HEADS):
        acc = None
        for n in range(nt):
            pv = jnp.dot(vat_refs[n][0, h // GQA_GROUP], probs[h][n], preferred_element_type=F32)
            acc = pv if acc is None else acc + pv
        outs.append(acc[:HEAD_DIM] / acc[HEAD_DIM:HEAD_DIM + 1])
    _emit_heads(outs, sm_ref, 2, o_ref)


def _win_prompt(q, kw, vat, sm, tq):
    b, t, _ = q.shape
    assert WINDOW % tq == 0 and WINDOW >= tq
    n_prev = WINDOW // tq

    def k_spec(back):
        return pl.BlockSpec((1, tq, KV_WIDTH), lambda bb, i: (bb, jnp.maximum(i - back, 0), 0))

    def v_spec(back):
        return pl.BlockSpec((1, KV_HEADS, VAT_ROWS, tq), lambda bb, i: (bb, 0, 0, jnp.maximum(i - back, 0)))

    backs = [n_prev - n for n in range(n_prev + 1)]
    nt = n_prev + 1
    return pl.pallas_call(
        functools.partial(_win_prompt_kernel, n_prev=n_prev),
        grid=(b, t // tq),
        in_specs=([pl.BlockSpec((1, tq, ATT_WIDTH), lambda bb, i: (bb, i, 0)),
                   pl.BlockSpec((1, tq, LANES), lambda bb, i: (bb, i, 0))]
                  + [k_spec(bk) for bk in backs] + [v_spec(bk) for bk in backs]),
        out_specs=pl.BlockSpec((1, tq, ATT_WIDTH), lambda bb, i: (bb, i, 0)),
        out_shape=jax.ShapeDtypeStruct((b, t, ATT_WIDTH), F32),
        compiler_params=_cparams(("arbitrary", "arbitrary")),
        name="win_prompt",
    )(q, sm, *([kw] * nt), *([vat] * nt))


CMP_PAGES = LANES * CMP_BLOCK // PAGE_SIZE


def _cmp_local_block(lane):
    half = LANES // 2
    return 2 * (lane % half) + lane // half


def _block_mean_matrix(n_pages):
    tok = jnp.arange(n_pages * PAGE_SIZE, dtype=jnp.int32)[:, None] // CMP_BLOCK
    col = jnp.arange(n_pages * PAGE_SIZE // CMP_BLOCK, dtype=jnp.int32)[None, :]
    blk = LANES * (col // LANES) + _cmp_local_block(col % LANES)
    return jnp.where(tok == blk, 1.0 / CMP_BLOCK, 0.0).astype(BF16)


def _cmp_sample_kernel(pt_ref, *refs, n_pg, past):
    page_refs = refs[:n_pg]
    a_ref, q_ref, sm_ref, o_ref, bias_ref, kct_ref = refs[n_pg:]
    s = pl.program_id(1)
    ns = pl.num_programs(1)
    n_chunks = kct_ref.shape[0]
    tn = q_ref.shape[1]
    rows = GQA_GROUP * tn

    x = jnp.concatenate([r[0] for r in page_refs], axis=1)
    hi = x.astype(BF16)
    lo = (x - hi.astype(F32)).astype(BF16)
    a = a_ref[...]
    means = jnp.dot(hi, a, preferred_element_type=F32) + jnp.dot(lo, a, preferred_element_type=F32)
    for c in range(n_chunks):
        kct_ref[c] = means[:, c * LANES:(c + 1) * LANES]

    @pl.when(s == ns - 1)
    def _():
        pos = past + lax.broadcasted_iota(jnp.int32, (tn, 1), 0)
        lane = lax.broadcasted_iota(jnp.int32, (rows, LANES), 1)
        q = q_ref[0] * SCALE
        outs, imps = [], []
        for kvh in range(KV_HEADS):
            qs = _stack_heads(q, kvh).astype(BF16)
            ss = []
            for c in range(n_chunks):
                sc = jnp.dot(qs, kct_ref[c, :KV_WIDTH, :].astype(BF16), preferred_element_type=F32)
                c_end = (c * LANES + _cmp_local_block(lane) + 1) * CMP_BLOCK - 1
                ss.append(jnp.where(c_end <= _tile_rows(pos, GQA_GROUP), sc, -jnp.inf))
            mx = ss[0].max(axis=1, keepdims=True)
            for sc in ss[1:]:
                mx = jnp.maximum(mx, sc.max(axis=1, keepdims=True))
            mx = jnp.where(mx > -jnp.inf, mx, 0.0)
            es = [jnp.exp(sc - mx) for sc in ss]
            dsum = es[0].sum(axis=1, keepdims=True)
            for e in es[1:]:
                dsum = dsum + e.sum(axis=1, keepdims=True)
            inv = 1.0 / jnp.where(dsum > 0, dsum, 1.0)
            o = None
            imp_blocks = []
            for c in range(n_chunks):
                p = es[c] * inv
                vc = kct_ref[c, KV_WIDTH:, :].astype(BF16)
                pv = lax.dot_general(p.astype(BF16), vc, (((1,), (1,)), ((), ())), preferred_element_type=F32)
                o = pv if o is None else o + pv
                imp = p[0:tn]
                for g in range(1, GQA_GROUP):
                    imp = imp + p[g * tn:(g + 1) * tn]
                imp_blocks.append(imp + pltpu.roll(imp, LANES // 2, 1))
            outs.append(o)
            if n_chunks == 1:
                imps.append(imp_blocks[0])
            else:
                lane_t = lax.broadcasted_iota(jnp.int32, (tn, LANES), 1)
                imps.append(jnp.where(lane_t < LANES // 2, imp_blocks[0], pltpu.roll(imp_blocks[1], LANES // 2, 1)))
        o_ref[0] = _unstack_heads(outs, jax.nn.sigmoid(sm_ref[0]), 0)
        n_past = past // SEL_BLOCK
        blk = lax.broadcasted_iota(jnp.int32, (tn, MAX_SEL_LANES), 1)
        forced = (blk == 0) | (blk == n_past - 1)
        rounds = min(TOP_N, n_past + 1) - 1
        for kvh in range(KV_HEADS):
            score = jnp.where(forced, jnp.inf, imps[kvh])
            score = jnp.where(blk < n_past, score, -jnp.inf)
            sel = _topk_rows(score, rounds)
            bias_ref[0, kvh] = jnp.where(sel > 0.5, 0.0, NEG).astype(BF16)


def _page_specs(n_pg):
    def spec(k):
        return pl.BlockSpec((1, 2 * KV_WIDTH, PAGE_SIZE), lambda b, s, pt: (pt[b, s * n_pg + k], 0, 0))
    return [spec(k) for k in range(n_pg)]


def _cmp_sample(page_table, pool_t, q, sm):
    bs, tn, _ = q.shape
    n_pages = page_table.shape[1]
    past = n_pages * PAGE_SIZE
    n_pg = n_pages
    assert n_pages % CMP_PAGES == 0 and n_pages // CMP_PAGES <= 2
    n_chunks = n_pages // CMP_PAGES
    grid_spec = pltpu.PrefetchScalarGridSpec(
        num_scalar_prefetch=1,
        grid=(bs, 1),
        in_specs=_page_specs(n_pg) + [
            pl.BlockSpec((n_pg * PAGE_SIZE, n_chunks * LANES), lambda b, s, pt: (0, 0)),
            pl.BlockSpec((1, tn, ATT_WIDTH), lambda b, s, pt: (b, 0, 0)),
            pl.BlockSpec((1, tn, LANES), lambda b, s, pt: (b, 0, 0))],
        out_specs=[pl.BlockSpec((1, tn, ATT_WIDTH), lambda b, s, pt: (b, 0, 0)),
                   pl.BlockSpec((1, KV_HEADS, tn, LANES), lambda b, s, pt: (b, 0, 0, 0))],
        scratch_shapes=[pltpu.VMEM((n_chunks, 2 * KV_WIDTH, LANES), F32)],
    )
    return pl.pallas_call(
        functools.partial(_cmp_sample_kernel, n_pg=n_pg, past=past),
        grid_spec=grid_spec,
        out_shape=[jax.ShapeDtypeStruct((bs, tn, ATT_WIDTH), F32),
                   jax.ShapeDtypeStruct((bs, KV_HEADS, tn, LANES), BF16)],
        compiler_params=_cparams(("arbitrary", "arbitrary")),
        name="cmp_sample",
    )(page_table, *([pool_t] * n_pg), _block_mean_matrix(n_pages), q, sm)


def _slc_sample_kernel(pt_ref, *refs, n_pg, past):
    page_refs = refs[:n_pg]
    q_ref, bias_ref, new_ref, sm_ref, o_ref, qaug_ref, m_ref, acc_ref = refs[n_pg:]
    s = pl.program_id(1)
    ns = pl.num_programs(1)
    tn = q_ref.shape[1]
    rows = GQA_GROUP * tn

    nq = KV_HEADS * rows

    @pl.when(s == 0)
    def _():
        q = q_ref[0] * SCALE
        blocks = [jnp.concatenate([_stack_heads(q, kvh).astype(BF16), _tile_rows(bias_ref[0, kvh], GQA_GROUP)],
                                  axis=1) for kvh in range(KV_HEADS)]
        qaug_ref[...] = jnp.concatenate(blocks + [jnp.zeros((LANES - nq, 2 * LANES), BF16)], axis=0)
        m_ref[...] = jnp.full(m_ref.shape, NEG, F32)
        acc_ref[...] = jnp.zeros(acc_ref.shape, F32)

    def update(sts, vaugs):
        m_prev = m_ref[...]
        m_new = m_prev
        for st in sts:
            m_new = jnp.maximum(m_new, jnp.max(st, axis=0, keepdims=True))
        acc = acc_ref[...] * jnp.exp(m_prev - m_new)
        for st, va in zip(sts, vaugs):
            acc = acc + jnp.dot(va, jnp.exp(st - m_new).astype(BF16), preferred_element_type=F32)
        acc_ref[...] = acc
        m_ref[...] = m_new

    x = jnp.concatenate([r[0] for r in page_refs], axis=1)
    tk = n_pg * PAGE_SIZE
    kblk = (s * tk + lax.broadcasted_iota(jnp.int32, (MAX_SEL_LANES, tk), 1)) // SEL_BLOCK
    onehot = jnp.where(kblk == lax.broadcasted_iota(jnp.int32, (MAX_SEL_LANES, tk), 0), 1.0, 0.0).astype(BF16)
    kaug_t = jnp.concatenate([x[:KV_WIDTH].astype(BF16), onehot], axis=0)
    vaug_t = jnp.concatenate([x[KV_WIDTH:].astype(BF16), jnp.ones((BF16_SUBLANES, tk), BF16)], axis=0)
    n_piece = 2
    w = tk // n_piece
    qa = qaug_ref[...]
    sts = [lax.dot_general(kaug_t[:, i * w:(i + 1) * w], qa, (((0,), (1,)), ((), ())), preferred_element_type=F32)
           for i in range(n_piece)]
    update(sts, [vaug_t[:, i * w:(i + 1) * w] for i in range(n_piece)])

    @pl.when(s == ns - 1)
    def _():
        xn = new_ref[0]
        nk = xn.shape[0]
        st = lax.dot_general(xn[:, :LANES].astype(BF16), qaug_ref[:, :LANES], (((1,), (1,)), ((), ())),
                             preferred_element_type=F32)
        kidx = lax.broadcasted_iota(jnp.int32, (nk, LANES), 0)
        qidx = lax.broadcasted_iota(jnp.int32, (nk, LANES), 1) % tn
        vn_t = jnp.concatenate([xn[:, LANES:].T.astype(BF16), jnp.ones((BF16_SUBLANES, nk), BF16)], axis=0)
        update([jnp.where(kidx <= qidx, st, NEG)], [vn_t])
        acc = acc_ref[...]
        o_t = (acc[:KV_WIDTH] / acc[KV_WIDTH:KV_WIDTH + 1]).T
        outs = [o_t[kvh * rows:(kvh + 1) * rows] for kvh in range(KV_HEADS)]
        o_ref[0] = _unstack_heads(outs, jax.nn.sigmoid(sm_ref[0]), 1)


def _slc_sample(page_table, pool, q, bias, new_pad, sm, n_pg):
    bs, tn, _ = q.shape
    n_pages = page_table.shape[1]
    past = n_pages * PAGE_SIZE
    rows = GQA_GROUP * tn
    npad = new_pad.shape[1]
    grid_spec = pltpu.PrefetchScalarGridSpec(
        num_scalar_prefetch=1,
        grid=(bs, n_pages // n_pg),
        in_specs=_page_specs(n_pg) + [
            pl.BlockSpec((1, tn, ATT_WIDTH), lambda b, s, pt: (b, 0, 0)),
            pl.BlockSpec((1, KV_HEADS, tn, LANES), lambda b, s, pt: (b, 0, 0, 0)),
            pl.BlockSpec((1, npad, 2 * KV_WIDTH), lambda b, s, pt: (b, 0, 0)),
            pl.BlockSpec((1, tn, LANES), lambda b, s, pt: (b, 0, 0))],
        out_specs=pl.BlockSpec((1, tn, ATT_WIDTH), lambda b, s, pt: (b, 0, 0)),
        scratch_shapes=[pltpu.VMEM((LANES, 2 * LANES), BF16),
                        pltpu.VMEM((1, LANES), F32),
                        pltpu.VMEM((KV_WIDTH + BF16_SUBLANES, LANES), F32)],
    )
    assert KV_HEADS * rows <= LANES
    return pl.pallas_call(
        functools.partial(_slc_sample_kernel, n_pg=n_pg, past=past),
        grid_spec=grid_spec,
        out_shape=jax.ShapeDtypeStruct((bs, tn, ATT_WIDTH), F32),
        compiler_params=_cparams(("arbitrary", "arbitrary")),
        name="slc_sample",
    )(page_table, *([pool] * n_pg), q, bias, new_pad, sm)


def _win_sample_kernel(q_ref, win_ref, new_ref, sm_ref, o_ref, *, past):
    tn = q_ref.shape[1]
    rows = GQA_GROUP * tn
    wb = win_ref.shape[1]
    kv = jnp.concatenate([win_ref[0], new_ref[0]], axis=0)
    nk = kv.shape[0]
    kb = kv[:, :LANES].astype(BF16)
    vb = kv[:, LANES:].astype(BF16)
    kidx = lax.broadcasted_iota(jnp.int32, (rows, nk), 1)
    qidx = lax.broadcasted_iota(jnp.int32, (rows, nk), 0) % tn
    kpos = past - wb + kidx
    dist = past + qidx - kpos
    ok = (dist >= 0) & (dist <= WINDOW) & (kpos >= 0) & (kidx < wb + tn)
    q = q_ref[0] * SCALE
    outs = []
    for kvh in range(KV_HEADS):
        qs = _stack_heads(q, kvh).astype(BF16)
        s = lax.dot_general(qs, kb, (((1,), (1,)), ((), ())), preferred_element_type=F32)
        s = jnp.where(ok, s, -jnp.inf)
        mx = jnp.max(s, axis=1, keepdims=True)
        e = jnp.exp(s - mx)
        p = e / jnp.sum(e, axis=1, keepdims=True)
        outs.append(jnp.dot(p.astype(BF16), vb, preferred_element_type=F32))
    o_ref[0] = _unstack_heads(outs, jax.nn.sigmoid(sm_ref[0]), 2)


def _win_sample(q, win_buf, new_pad, sm, past):
    bs, tn, _ = q.shape
    wb = win_buf.shape[1]
    npad = new_pad.shape[1]
    return pl.pallas_call(
        functools.partial(_win_sample_kernel, past=past),
        grid=(bs,),
        in_specs=[pl.BlockSpec((1, tn, ATT_WIDTH), lambda b: (b, 0, 0)),
                  pl.BlockSpec((1, wb, 2 * KV_WIDTH), lambda b: (b, 0, 0)),
                  pl.BlockSpec((1, npad, 2 * KV_WIDTH), lambda b: (b, 0, 0)),
                  pl.BlockSpec((1, tn, LANES), lambda b: (b, 0, 0))],
        out_specs=pl.BlockSpec((1, tn, ATT_WIDTH), lambda b: (b, 0, 0)),
        out_shape=jax.ShapeDtypeStruct((bs, tn, ATT_WIDTH), F32),
        compiler_params=_cparams(("arbitrary",)),
        name="win_sample",
    )(q, win_buf, new_pad, sm)


def _out_kernel(x_ref, oc_ref, os_ref, ow_ref, za_ref, ys_ref, gate_ref, ang_ref, wo_ref, fg_ref, y_ref):
    nbk, tt, d = x_ref.shape
    m = nbk * tt
    o = (oc_ref[...] + os_ref[...]) + ow_ref[...]
    t = o * _silu(za_ref[...])
    ms = jnp.mean(t * t, axis=-1, keepdims=True)
    y_att = (t * lax.rsqrt(ms + EPS)) * ang_ref[...].reshape(1, 1, ATT_WIDTH)
    ya = y_att.reshape(m, ATT_WIDTH).astype(BF16)
    ys = ys_ref[...].reshape(m, SSD_WIDTH)
    mix = (jnp.dot(ys, wo_ref[:SSD_WIDTH, :], preferred_element_type=F32)
           + jnp.dot(ya, wo_ref[SSD_WIDTH:, :], preferred_element_type=F32))
    xp = x_ref[...] + gate_ref[...] * mix.reshape(nbk, tt, d)
    ms2 = jnp.mean(xp * xp, axis=-1, keepdims=True)
    y_ref[...] = (xp * lax.rsqrt(ms2 + EPS)) * fg_ref[...].reshape(1, 1, d)


def _out(x3, o_c, o_s, o_w, za, y_ssd, gate, att_norm_g, w_out_b, final_g, *, nbk, groups_per_mod):
    g_total, tt, d = x3.shape
    steps = g_total // nbk
    if groups_per_mod is None:
        mod_spec = pl.BlockSpec((nbk, 1, d), lambda g: (g, 0, 0))
    else:
        mod_spec = pl.BlockSpec((1, 1, d), lambda g: (g // groups_per_mod, 0, 0))

    def tok_spec(c):
        return pl.BlockSpec((nbk, tt, c), lambda g: (g, 0, 0))

    return pl.pallas_call(
        _out_kernel,
        grid=(steps,),
        in_specs=[tok_spec(d), tok_spec(ATT_WIDTH), tok_spec(ATT_WIDTH), tok_spec(ATT_WIDTH),
                  tok_spec(ATT_WIDTH), tok_spec(SSD_WIDTH), mod_spec,
                  pl.BlockSpec((1, ATT_WIDTH), lambda g: (0, 0)),
                  pl.BlockSpec((SSD_WIDTH + ATT_WIDTH, d), lambda g: (0, 0)),
                  pl.BlockSpec((1, d), lambda g: (0, 0))],
        out_specs=tok_spec(d),
        out_shape=jax.ShapeDtypeStruct((g_total, tt, d), F32),
        compiler_params=_cparams(("arbitrary",)),
        name="outproj",
    )(x3, o_c, o_s, o_w, za, y_ssd, gate, att_norm_g.reshape(1, ATT_WIDTH), w_out_b, final_g.reshape(1, d))


def _perm_cmp_means(kcm, b):
    nc = kcm.shape[1]
    ns = nc // 2
    assert ns <= MAX_SEL_LANES
    eo = kcm.reshape(b, ns, 2, 2 * KV_WIDTH).transpose(0, 2, 1, 3)
    eo = jnp.pad(eo, ((0, 0), (0, 0), (0, MAX_SEL_LANES - ns), (0, 0)))
    return eo.reshape(b, 2 * MAX_SEL_LANES, 2 * KV_WIDTH)


def _prompt_layer(x, mod, lw, final_g, apply_final):
    b, t, d = x.shape
    shift, scale, gate = (mod[:, None, i * d:(i + 1) * d] for i in range(3))
    tt = min(SSD_CHUNK, t)
    assert t % tt == 0 and tt % CMP_BLOCK == 0 and t >= WINDOW
    gpb = t // tt
    x3 = x.reshape(b * gpb, tt, d)
    ssd_params = _ssd_params(lw["conv_w"], lw["conv_b"], lw["dt_bias"], lw["a_log"], lw["d_skip"],
                             lw["ssd_norm_g"])
    q, za, sm, kcm, kvt_c, kvt_s, kvt_w, ka_s, vat_s, kw, vat_w, y_ssd, ssm_new, xlast = _inproj_prompt(
        x, scale, shift, lw["norm_g"], lw["w_big"], lw["w_kvt"], ssd_params, tt)
    r = lambda a: a.reshape(b, t, a.shape[-1])
    q, za, sm, ka_s, kw, y_ssd = map(r, (q, za, sm, ka_s, kw, y_ssd))
    kcm = kcm.reshape(b, t // CMP_BLOCK, 2 * KV_WIDTH)
    ssm_new = ssm_new.reshape(b, SSD_HEADS, HEAD_DIM, D_STATE)

    o_c, bias = _cmp_prompt(q, _perm_cmp_means(kcm, b), sm, 128)
    tq = 256
    o_s = _slc_prompt(q, bias, ka_s, vat_s, sm, tq)
    o_w = _win_prompt(q, kw, vat_w, sm, tq)

    g3 = lambda a: a.reshape(b * gpb, tt, a.shape[-1])
    y3 = _out(x3, g3(o_c), g3(o_s), g3(o_w), g3(za), g3(y_ssd), gate, lw["att_norm_g"], lw["w_out_b"],
              final_g, nbk=1, groups_per_mod=gpb)
    assert apply_final
    kv6 = lambda a: a.reshape(b, 2, KV_HEADS, HEAD_DIM, a.shape[-1]).transpose(0, 4, 1, 2, 3)
    conv_new = xlast[:, xlast.shape[1] - (CONV_WIDTH - 1):]
    outs = (kv6(kvt_c), kv6(kvt_s), kv6(kvt_w[:, :, t - min(WINDOW, t):]), conv_new, ssm_new)
    return y3.reshape(b, t, d), outs


def _sample_layer(x, mod, lw, final_g, pool_c, pool_s, win_buf, conv_buf, ssm, page_table, apply_final):
    bs, tn, d = x.shape
    shift, scale, gate = (mod[:, None, i * d:(i + 1) * d] for i in range(3))
    n_pages = page_table.shape[1]
    past = n_pages * PAGE_SIZE
    nbk = 16
    n_pg = 16
    assert bs % nbk == 0 and tn % 8 == 0 and tn <= SEL_BLOCK and n_pages % n_pg == 0
    assert past // SEL_BLOCK <= MAX_SEL_LANES and past % SEL_BLOCK == 0
    pos = past + jnp.arange(tn, dtype=jnp.int32)
    tabs = tuple(jnp.tile(tb, (nbk, 1)) for tb in _rope_tables(pos))
    zs, xbc, q, za, sm, kvc, kvs, kvw = _inproj_sample(
        x, scale, shift, lw["norm_g"], lw["w_big"], tabs, nbk=nbk)

    y_ssd, ssm_new = _ssd(xbc, sm, zs, conv_buf, ssm, lw["conv_w"], lw["conv_b"], lw["dt_bias"], lw["a_log"],
                          lw["d_skip"], lw["ssd_norm_g"])

    npad = LANES
    to_rows = lambda p: p.transpose(0, 2, 3, 4, 1).reshape(p.shape[0], 2 * KV_WIDTH, PAGE_SIZE)
    o_c, bias = _cmp_sample(page_table, to_rows(pool_c), q, sm)
    kvs_pad = jnp.pad(kvs, ((0, 0), (0, npad - tn), (0, 0)))
    o_s = _slc_sample(page_table, to_rows(pool_s), q, bias, kvs_pad, sm, n_pg)
    wb = win_buf.shape[1]
    win2 = win_buf.reshape(bs, wb, 2 * KV_WIDTH)
    kvw_pad = jnp.pad(kvw, ((0, 0), (0, npad - tn), (0, 0)))
    o_w = _win_sample(q, win2, kvw_pad, sm, past)

    y = _out(x, o_c, o_s, o_w, za, y_ssd, gate, lw["att_norm_g"], lw["w_out_b"], final_g,
             nbk=nbk, groups_per_mod=None)
    assert apply_final
    kv6 = lambda a: a.reshape(bs, a.shape[1], 2, KV_HEADS, HEAD_DIM)
    kv_w_all = jnp.concatenate([win2, kvw], axis=1)
    win_new = kv_w_all[:, kv_w_all.shape[1] - min(WINDOW, past + tn):]
    conv_new = jnp.concatenate([conv_buf, xbc], axis=1)[:, tn:]
    outs = (kv6(kvc), kv6(kvs), kv6(win_new), conv_new, ssm_new)
    return y, outs


def kernel(x_prompt, x_sample, cache_cmp_kv, cache_slc_kv, state_win_kv, state_conv, state_ssm, page_table,
           c_prompt, c_sample, w_ada, b_ada, norm_g, w_in, conv_w, conv_b, dt_bias, a_log, d_skip,
           ssd_norm_g, att_norm_g, w_out, final_g):
    depth = w_ada.shape[0]
    assert depth == 1
    n_prompt = c_prompt.shape[0]
    xp, xs = x_prompt, x_sample
    out_p, out_s = [], []
    for l in range(depth):
        w_big, w_kvt = _rearrange_w_in(w_in[l])
        lw = dict(norm_g=norm_g[l], w_big=w_big, w_kvt=w_kvt, conv_w=conv_w[l], conv_b=conv_b[l],
                  dt_bias=dt_bias[l], a_log=a_log[l], d_skip=d_skip[l], ssd_norm_g=ssd_norm_g[l],
                  att_norm_g=att_norm_g[l], w_out_b=w_out[l].astype(BF16))
        mod = _mod(jnp.concatenate([c_prompt, c_sample], axis=0), w_ada[l], b_ada[l])
        last = l == depth - 1
        xp, op = _prompt_layer(xp, mod[:n_prompt], lw, final_g, last)
        xs, os_ = _sample_layer(xs, mod[n_prompt:], lw, final_g, cache_cmp_kv[l], cache_slc_kv[l],
                                state_win_kv[l], state_conv[l], state_ssm[l], page_table, last)
        out_p.append(op)
        out_s.append(os_)
    sp = [jnp.stack([o[k] for o in out_p]) for k in range(5)]
    sd = [jnp.stack([o[k] for o in out_s]) for k in range(5)]
    return (xp, xs, sp[0], sp[1], sp[2], sp[3], sp[4], sd[0], sd[1], sd[2], sd[3], sd[4])
```

```python
import functools

import jax
import jax.numpy as jnp
from jax import lax
from jax.experimental import pallas as pl
from jax.experimental.pallas import tpu as pltpu

F32 = jnp.float32
BF16 = jnp.bfloat16

HEAD_DIM = 64
SSD_HEADS = 8
SSD_WIDTH = SSD_HEADS * HEAD_DIM
SSD_GROUPS = 2
D_STATE = 128
CONV_WIDTH = 4
CONV_CH = SSD_WIDTH + 2 * SSD_GROUPS * D_STATE
SSD_CHUNK = 256
ATT_HEADS = 8
ATT_WIDTH = ATT_HEADS * HEAD_DIM
KV_HEADS = 2
GQA_GROUP = ATT_HEADS // KV_HEADS
KV_WIDTH = KV_HEADS * HEAD_DIM
CMP_BLOCK = 32
SEL_BLOCK = 64
TOP_N = 16
N_LOCAL_BLOCKS = 2
WINDOW = 512
N_BRANCH = 3
ROT_DIM = HEAD_DIM // 4
ROPE_THETA = 500000.0
PAGE_SIZE = 128
EPS = 1e-6
COL_SIZES = (SSD_WIDTH, CONV_CH, SSD_HEADS, ATT_WIDTH, 2 * KV_WIDTH, 2 * KV_WIDTH, 2 * KV_WIDTH,
             N_BRANCH * ATT_HEADS, ATT_WIDTH)

LANES = 128
MAX_SEL_LANES = LANES
NEG = -1e30
SCALE = HEAD_DIM ** -0.5
SCALE_LOG2 = SCALE * 1.4426950408889634
GATE_LANE0 = SSD_HEADS
BF16_SUBLANES = 16
VAT_ROWS = HEAD_DIM + BF16_SUBLANES
VMEM_LIMIT = 56 * 1024 * 1024

_O_ZS = 0
_O_XBC = _O_ZS + SSD_WIDTH
_O_Q = _O_XBC + CONV_CH
_O_KC = _O_Q + ATT_WIDTH
_O_KS = _O_KC + 2 * KV_WIDTH
_O_KW = _O_KS + 2 * KV_WIDTH
_O_ZA = _O_KW + 2 * KV_WIDTH
_O_SM = _O_ZA + ATT_WIDTH
_W_COLS = _O_SM + LANES


def _cparams(sem):
    return pltpu.CompilerParams(dimension_semantics=sem, vmem_limit_bytes=VMEM_LIMIT)


def _silu(v):
    return v * jax.nn.sigmoid(v)


def _mod_kernel(c_ref, w_ref, b_ref, o_ref):
    a = _silu(c_ref[...])
    o_ref[...] = jnp.dot(a, w_ref[...], preferred_element_type=F32,
                         precision=lax.Precision.HIGHEST) + b_ref[...]


def _mod(c, w_ada, b_ada):
    n, d = c.shape
    cols = w_ada.shape[1]
    tn = d
    assert cols % tn == 0
    return pl.pallas_call(
        _mod_kernel,
        grid=(cols // tn,),
        in_specs=[pl.BlockSpec((n, d), lambda j: (0, 0)),
                  pl.BlockSpec((d, tn), lambda j: (0, j)),
                  pl.BlockSpec((1, tn), lambda j: (0, j))],
        out_specs=pl.BlockSpec((n, tn), lambda j: (0, j)),
        out_shape=jax.ShapeDtypeStruct((n, cols), F32),
        compiler_params=_cparams(("arbitrary",)),
        name="mod",
    )(c, w_ada, b_ada.reshape(1, cols))


def _rope128(v, rc, ra, rb):
    half = ROT_DIM // 2
    return v * rc + pltpu.roll(v, LANES - half, 1) * ra + pltpu.roll(v, half, 1) * rb


def _modulated_norm(x_ref, sc_ref, sh_ref, g_ref):
    nbk, tt, d = x_ref.shape
    x = x_ref[...]
    ms = jnp.mean(x * x, axis=-1, keepdims=True)
    y = (x * lax.rsqrt(ms + EPS)) * g_ref[...].reshape(1, 1, d)
    h = y * (1.0 + sc_ref[...]) + sh_ref[...]
    return h.reshape(nbk * tt, d).astype(BF16)


def _inproj_common(hb, w_ref, rc, ra, rb, zs_ref, xbc_ref, q_ref, za_ref, sm_ref, consume_ssd_inputs=None):
    nbk, tt, _ = q_ref.shape

    def proj(lo, n):
        return jnp.dot(hb, w_ref[:, lo:lo + n], preferred_element_type=F32)

    zs = proj(_O_ZS, SSD_WIDTH)
    xbc = proj(_O_XBC, CONV_CH)
    sm = proj(_O_SM, LANES)
    if zs_ref is not None:
        zs_ref[...] = zs.reshape(nbk, tt, SSD_WIDTH)
    if xbc_ref is not None:
        xbc_ref[...] = xbc.reshape(nbk, tt, CONV_CH)
    if consume_ssd_inputs is not None:
        consume_ssd_inputs(zs, xbc, sm)
    za_ref[...] = proj(_O_ZA, ATT_WIDTH).reshape(nbk, tt, ATT_WIDTH)
    sm_ref[...] = sm.reshape(nbk, tt, LANES)
    qraw = proj(_O_Q, ATT_WIDTH)
    q = jnp.concatenate([_rope128(qraw[:, LANES * j:LANES * (j + 1)], rc, ra, rb)
                         for j in range(ATT_WIDTH // LANES)], axis=1)
    q_ref[...] = q.reshape(nbk, tt, ATT_WIDTH)
    return proj, zs, xbc, sm


def _inproj_sample_kernel(x_ref, sc_ref, sh_ref, g_ref, w_ref, rc_ref, ra_ref, rb_ref,
                          zs_ref, xbc_ref, q_ref, za_ref, sm_ref, kvc_ref, kvs_ref, kvw_ref):
    nbk, tt, _ = x_ref.shape
    hb = _modulated_norm(x_ref, sc_ref, sh_ref, g_ref)
    rc, ra, rb = rc_ref[...], ra_ref[...], rb_ref[...]
    proj = _inproj_common(hb, w_ref, rc, ra, rb, zs_ref, xbc_ref, q_ref, za_ref, sm_ref)[0]
    for off, ref in ((_O_KC, kvc_ref), (_O_KS, kvs_ref), (_O_KW, kvw_ref)):
        u = proj(off, 2 * KV_WIDTH)
        kv = jnp.concatenate([_rope128(u[:, :KV_WIDTH], rc, ra, rb), u[:, KV_WIDTH:]], axis=1)
        ref[...] = kv.reshape(nbk, tt, 2 * KV_WIDTH)


def _rope_rows(k, cos, sin):
    half = ROT_DIM // 2
    x1, x2 = k[:half], k[half:ROT_DIM]
    return jnp.concatenate([x1 * cos - x2 * sin, x2 * cos + x1 * sin, k[ROT_DIM:]], axis=0)


def _inproj_prompt_kernel(x_ref, sc_ref, sh_ref, g_ref, w_ref, wt_ref, rc_ref, ra_ref, rb_ref, cos_ref, sin_ref,
                          cw_ref, cb_ref, dtb_ref, dtbc_ref, alog_ref, alogc_ref, dsk_ref, ng_ref,
                          q_ref, za_ref, sm_ref, kcm_ref,
                          kvtc_ref, kvts_ref, kvtw_ref, kas_ref, vats_ref, kw_ref, vatw_ref,
                          y_ref, hout_ref, xlast_ref, xp_ref, h_ref, *, groups_per_seq):
    _, tt, _ = x_ref.shape
    hb = _modulated_norm(x_ref, sc_ref, sh_ref, g_ref)
    rc, ra, rb = rc_ref[...], ra_ref[...], rb_ref[...]
    w2 = 2 * KV_WIDTH

    def ssd_branch(zs, xbc, sm):
        c = pl.program_id(0) % groups_per_seq
        pad = xp_ref.shape[0] - tt
        xlast_ref[0] = xbc[tt - pad:]

        def init():
            xp_ref[0:pad, :] = jnp.zeros((pad, CONV_CH), F32)
            h_ref[...] = jnp.zeros(h_ref.shape, F32)

        smt = lax.dot_general(wt_ref[N_BRANCH * w2:, :], hb, (((1,), (1,)), ((), ())),
                              preferred_element_type=F32)
        _ssd_chunk(c == 0, c == groups_per_seq - 1, init, xbc, sm, smt, zs, cw_ref, cb_ref, dtb_ref, dtbc_ref,
                   alog_ref, alogc_ref, dsk_ref, ng_ref, y_ref, hout_ref, xp_ref, h_ref)

    proj = _inproj_common(hb, w_ref, rc, ra, rb, None, None, q_ref, za_ref, sm_ref, ssd_branch)[0]

    u = proj(_O_KC, 2 * KV_WIDTH)
    kv = jnp.concatenate([_rope128(u[:, :KV_WIDTH], rc, ra, rb), u[:, KV_WIDTH:]], axis=1)
    nblk = tt // CMP_BLOCK
    means = jnp.sum(kv.reshape(nblk, CMP_BLOCK, 2 * KV_WIDTH), axis=1) * (1.0 / CMP_BLOCK)
    kcm_ref[...] = means.reshape(1, nblk, 2 * KV_WIDTH)

    ti = pl.program_id(0) % groups_per_seq
    blk = (ti * tt + lax.broadcasted_iota(jnp.int32, (tt, MAX_SEL_LANES), 0)) // SEL_BLOCK
    onehot = jnp.where(blk == lax.broadcasted_iota(jnp.int32, (tt, MAX_SEL_LANES), 1), 1.0, 0.0)
    ks = _rope128(proj(_O_KS, KV_WIDTH), rc, ra, rb)
    kas_ref[0] = jnp.concatenate([onehot, ks], axis=1).astype(BF16)
    kw_ref[0] = _rope128(proj(_O_KW, KV_WIDTH), rc, ra, rb).astype(BF16)

    ut = lax.dot_general(wt_ref[:N_BRANCH * w2, :], hb, (((1,), (1,)), ((), ())), preferred_element_type=F32)
    cos, sin = cos_ref[...], sin_ref[...]
    slabs = []
    for br in range(N_BRANCH):
        s = ut[br * w2:(br + 1) * w2]
        ks = [_rope_rows(s[h * HEAD_DIM:(h + 1) * HEAD_DIM], cos, sin) for h in range(KV_HEADS)]
        slabs.append(jnp.concatenate(ks + [s[KV_WIDTH:]], axis=0))
    kvtc_ref[0] = slabs[0]
    kvts_ref[0] = slabs[1]
    kvtw_ref[0] = slabs[2]
    ones = jnp.ones((VAT_ROWS - HEAD_DIM, tt), F32)
    for slab, vat_ref in ((slabs[1], vats_ref), (slabs[2], vatw_ref)):
        for h in range(KV_HEADS):
            v = slab[KV_WIDTH + h * HEAD_DIM:KV_WIDTH + (h + 1) * HEAD_DIM]
            vat_ref[0, h] = jnp.concatenate([v, ones], axis=0).astype(BF16)


def _mod_spec(nbk, d, groups_per_mod):
    if groups_per_mod is None:
        return pl.BlockSpec((nbk, 1, d), lambda g: (g, 0, 0))
    return pl.BlockSpec((1, 1, d), lambda g: (g // groups_per_mod, 0, 0))


def _inproj_sample(x3, scale, shift, norm_g, w_big, rope_tabs, *, nbk):
    g_total, tt, d = x3.shape
    m = nbk * tt
    mod_spec = _mod_spec(nbk, d, None)
    tab_spec = pl.BlockSpec((m, LANES), lambda g: (0, 0))

    def tok_spec(c):
        return pl.BlockSpec((nbk, tt, c), lambda g: (g, 0, 0))

    widths = (SSD_WIDTH, CONV_CH, ATT_WIDTH, ATT_WIDTH, LANES, 2 * KV_WIDTH, 2 * KV_WIDTH, 2 * KV_WIDTH)
    return pl.pallas_call(
        _inproj_sample_kernel,
        grid=(g_total // nbk,),
        in_specs=[tok_spec(d), mod_spec, mod_spec,
                  pl.BlockSpec((1, d), lambda g: (0, 0)),
                  pl.BlockSpec((d, _W_COLS), lambda g: (0, 0)),
                  tab_spec, tab_spec, tab_spec],
        out_specs=[tok_spec(c) for c in widths],
        out_shape=[jax.ShapeDtypeStruct((g_total, tt, c), F32) for c in widths],
        compiler_params=_cparams(("arbitrary",)),
        name="inproj_sample",
    )(x3, scale, shift, norm_g.reshape(1, d), w_big, *rope_tabs)


def _inproj_prompt(x, scale, shift, norm_g, w_big, w_kvt, ssd_params, tt):
    b, t, d = x.shape
    assert tt == min(SSD_CHUNK, t)
    gps = t // tt
    pad = 8
    steps = b * gps
    x3 = x.reshape(steps, tt, d)
    pos = jnp.arange(t, dtype=jnp.int32)
    tabs = _rope_tables(pos)
    cos_t, sin_t = _rope_angles(pos)
    mod_spec = _mod_spec(1, d, gps)
    tab_spec = pl.BlockSpec((tt, LANES), lambda g: (g % gps, 0))
    ang_spec = pl.BlockSpec((ROT_DIM // 2, tt), lambda g: (0, g % gps))

    def tok_spec(c):
        return pl.BlockSpec((1, tt, c), lambda g: (g, 0, 0))

    def row_spec(r):
        return pl.BlockSpec((1, r, tt), lambda g: (g // gps, 0, g % gps))

    vat_spec = pl.BlockSpec((1, KV_HEADS, VAT_ROWS, tt), lambda g: (g // gps, 0, 0, g % gps))
    nblk = tt // CMP_BLOCK
    w2 = 2 * KV_WIDTH
    tok_widths = (ATT_WIDTH, ATT_WIDTH, LANES)
    state_spec = pl.BlockSpec((1, SSD_HEADS // 2, 2 * HEAD_DIM, D_STATE), lambda g: (g // gps, 0, 0, 0))
    out_specs = ([tok_spec(c) for c in tok_widths]
                 + [pl.BlockSpec((1, nblk, w2), lambda g: (g, 0, 0))]
                 + [row_spec(w2)] * 3
                 + [tok_spec(MAX_SEL_LANES + KV_WIDTH), vat_spec, tok_spec(KV_WIDTH), vat_spec]
                 + [tok_spec(SSD_WIDTH), state_spec, pl.BlockSpec((1, pad, CONV_CH), lambda g: (g // gps, 0, 0))])
    vat_shape = jax.ShapeDtypeStruct((b, KV_HEADS, VAT_ROWS, t), BF16)
    out_shape = ([jax.ShapeDtypeStruct((steps, tt, c), F32) for c in tok_widths]
                 + [jax.ShapeDtypeStruct((steps, nblk, w2), F32)]
                 + [jax.ShapeDtypeStruct((b, w2, t), F32)] * 3
                 + [jax.ShapeDtypeStruct((steps, tt, MAX_SEL_LANES + KV_WIDTH), BF16), vat_shape,
                    jax.ShapeDtypeStruct((steps, tt, KV_WIDTH), BF16), vat_shape]
                 + [jax.ShapeDtypeStruct((steps, tt, SSD_WIDTH), BF16),
                    jax.ShapeDtypeStruct((b, SSD_HEADS // 2, 2 * HEAD_DIM, D_STATE), F32),
                    jax.ShapeDtypeStruct((b, pad, CONV_CH), F32)])
    return pl.pallas_call(
        functools.partial(_inproj_prompt_kernel, groups_per_seq=gps),
        grid=(steps,),
        in_specs=[tok_spec(d), mod_spec, mod_spec,
                  pl.BlockSpec((1, d), lambda g: (0, 0)),
                  pl.BlockSpec((d, _W_COLS), lambda g: (0, 0)),
                  pl.BlockSpec(w_kvt.shape, lambda g: (0, 0)),
                  tab_spec, tab_spec, tab_spec, ang_spec, ang_spec]
                 + [pl.BlockSpec(p.shape, lambda g: (0, 0)) for p in ssd_params],
        out_specs=out_specs,
        out_shape=out_shape,
        scratch_shapes=[pltpu.VMEM((tt + pad, CONV_CH), F32),
                        pltpu.VMEM((SSD_HEADS // 2, 2 * HEAD_DIM, D_STATE), F32)],
        compiler_params=_cparams(("arbitrary",)),
        name="inproj_ssd_prompt",
    )(x3, scale, shift, norm_g.reshape(1, d), w_big, w_kvt, *tabs, cos_t, sin_t, *ssd_params)


def _rope_angles(pos):
    half = ROT_DIM // 2
    inv_freq = ROPE_THETA ** (-jnp.arange(half, dtype=F32) * 2.0 / ROT_DIM)
    ang = pos.astype(F32)[:, None] * inv_freq[None, :]
    return jnp.cos(ang).T, jnp.sin(ang).T


def _rope_tables(pos):
    half = ROT_DIM // 2
    inv_freq = ROPE_THETA ** (-jnp.arange(half, dtype=F32) * 2.0 / ROT_DIM)
    ang = pos.astype(F32)[:, None] * inv_freq[None, :]
    cos, sin = jnp.cos(ang), jnp.sin(ang)
    n = pos.shape[0]
    one = jnp.ones((n, HEAD_DIM - ROT_DIM), F32)
    zero_h = jnp.zeros((n, half), F32)
    zero_r = jnp.zeros((n, HEAD_DIM - ROT_DIM), F32)
    rc = jnp.concatenate([cos, cos, one], axis=1)
    ra = jnp.concatenate([-sin, zero_h, zero_r], axis=1)
    rb = jnp.concatenate([zero_h, sin, zero_r], axis=1)
    rep = LANES // HEAD_DIM
    return tuple(jnp.tile(t, (1, rep)) for t in (rc, ra, rb))


def _rearrange_w_in(w_in):
    parts, o = [], 0
    for n in COL_SIZES:
        parts.append(w_in[:, o:o + n])
        o += n
    z_s, xbc, dt, q, kc, ks, kw, g, z_a = parts
    d = w_in.shape[0]
    small = jnp.concatenate([dt, g, jnp.zeros((d, LANES - SSD_HEADS - N_BRANCH * ATT_HEADS), w_in.dtype)], axis=1)
    w_big = jnp.concatenate([z_s, xbc, q, kc, ks, kw, z_a, small], axis=1).astype(BF16)
    w_kvt = jnp.concatenate([kc, ks, kw, small], axis=1).T.astype(BF16)
    return w_big, w_kvt


def _pair_cols(mat, p, shape):
    lane = lax.broadcasted_iota(jnp.int32, shape, 1)
    a = jnp.broadcast_to(mat[:, 2 * p:2 * p + 1], shape)
    b = jnp.broadcast_to(mat[:, 2 * p + 1:2 * p + 2], shape)
    return jnp.where(lane < HEAD_DIM, a, b)


def _ssd_kernel(xbc_ref, sm_ref, smt_ref, zs_ref, conv0_ref, h0_ref, *rest):
    c = pl.program_id(1)
    xp_ref, h_ref = rest[-2:]
    pad = xp_ref.shape[0] - xbc_ref.shape[1]

    def init():
        xp_ref[0:pad, :] = conv0_ref[0]
        h_ref[...] = h0_ref[0]

    _ssd_chunk(c == 0, c == pl.num_programs(1) - 1, init, xbc_ref[0], sm_ref[0], smt_ref[0], zs_ref[0], *rest)


def _ssd_chunk(first, last, init, xbc, sm, smt, zs, cw_ref, cb_ref, dtb_ref, dtbc_ref,
               alog_ref, alogc_ref, dsk_ref, ng_ref, y_ref, hout_ref, xp_ref, h_ref):
    L = xbc.shape[0]
    pad = xp_ref.shape[0] - L
    hp = jnp.float32

    @pl.when(first)
    def _():
        init()

    xp_ref[pad:pad + L, :] = xbc
    cw = cw_ref[...]
    conv = cb_ref[...]
    for w in range(CONV_WIDTH):
        o = pad - (CONV_WIDTH - 1) + w
        conv = conv + xp_ref[o:o + L, :] * cw[w:w + 1, :]
    halo = xp_ref[L:L + pad, :]
    xp_ref[0:pad, :] = halo

    u = _silu(conv)
    xs = u[:, :SSD_WIDTH]
    gw = SSD_GROUPS * D_STATE
    bm = u[:, SSD_WIDTH:SSD_WIDTH + gw]
    cm = u[:, SSD_WIDTH + gw:]

    dt = jax.nn.softplus(sm + dtb_ref[...])
    dta = dt * (-jnp.exp(alog_ref[...]))
    row = lax.broadcasted_iota(jnp.int32, (L, L), 0)
    col = lax.broadcasted_iota(jnp.int32, (L, L), 1)
    causal = row >= col
    la = jnp.dot(causal.astype(hp), dta, preferred_element_type=hp,
                 precision=lax.Precision.HIGHEST)
    dtt = jax.nn.softplus(smt + dtbc_ref[...])
    dtat = dtt * (-jnp.exp(alogc_ref[...]))
    lat = jnp.dot(dtat, (row <= col).astype(hp), preferred_element_type=hp,
                  precision=lax.Precision.HIGHEST)
    la_last = la[L - 1:L, :]
    ela = jnp.exp(la)
    te = jnp.exp(la_last - la)
    cdec = jnp.exp(la_last)

    lane = lax.broadcasted_iota(jnp.int32, (L, LANES), 1)
    srow = lax.broadcasted_iota(jnp.int32, (LANES, LANES), 0)
    hpg = SSD_HEADS // SSD_GROUPS
    ys = []
    for g in range(SSD_GROUPS):
        bm_g = bm[:, g * D_STATE:(g + 1) * D_STATE]
        cm_g = cm[:, g * D_STATE:(g + 1) * D_STATE].astype(BF16)
        bm_gb = bm_g.astype(BF16)
        cb = lax.dot_general(cm_g, bm_gb, (((1,), (1,)), ((), ())), preferred_element_type=hp)
        for pp in range(hpg // 2):
            p = g * (hpg // 2) + pp
            xs_p = xs[:, LANES * p:LANES * (p + 1)]
            xdt = xs_p * _pair_cols(dt, p, (L, LANES))
            xdt_b = xdt.astype(BF16)
            yd = []
            for r in (2 * p, 2 * p + 1):
                seg = la[:, r:r + 1] - lat[r:r + 1, :]
                dec = jnp.where(causal, jnp.exp(jnp.where(causal, seg, 0.0)), 0.0)
                yd.append(jnp.dot((cb * dec).astype(BF16), xdt_b, preferred_element_type=hp))
            y_diag = jnp.where(lane < HEAD_DIM, yd[0], yd[1])
            h_p = h_ref[p]
            y_off = lax.dot_general(cm_g, h_p.astype(BF16), (((1,), (1,)), ((), ())),
                                    preferred_element_type=hp) * _pair_cols(ela, p, (L, LANES))
            ys.append(y_diag + y_off + dsk_ref[:, LANES * p:LANES * (p + 1)] * xs_p)
            xw = (xdt * _pair_cols(te, p, (L, LANES))).astype(BF16)
            st = lax.dot_general(xw, bm_gb, (((0,), (0,)), ((), ())), preferred_element_type=hp)
            cd = jnp.where(srow < HEAD_DIM,
                           jnp.broadcast_to(cdec[:, 2 * p:2 * p + 1], (LANES, LANES)),
                           jnp.broadcast_to(cdec[:, 2 * p + 1:2 * p + 2], (LANES, LANES)))
            h_ref[p] = h_p * cd + st

    y = jnp.concatenate(ys, axis=1)
    t = y * _silu(zs)
    ms = jnp.mean(t * t, axis=-1, keepdims=True)
    y_ref[0] = ((t * lax.rsqrt(ms + EPS)) * ng_ref[...]).astype(y_ref.dtype)

    del last
    hout_ref[0] = h_ref[...]


def _ssd(xbc, sm, zs, conv_state, h0, conv_w, conv_b, dt_bias, a_log, d_skip, norm_g):
    bn, t, _ = xbc.shape
    L = min(SSD_CHUNK, t)
    assert t % L == 0
    nc = t // L
    pad = 8
    smt = jnp.swapaxes(sm, 1, 2)
    conv0 = jnp.pad(conv_state, ((0, 0), (pad - (CONV_WIDTH - 1), 0), (0, 0)))
    hp2 = h0.reshape(bn, SSD_HEADS // 2, 2 * HEAD_DIM, D_STATE)
    params = _ssd_params(conv_w, conv_b, dt_bias, a_log, d_skip, norm_g)

    def full(shape):
        return pl.BlockSpec(shape, lambda b, c: tuple(0 for _ in shape))

    y, hout = pl.pallas_call(
        _ssd_kernel,
        grid=(bn, nc),
        in_specs=[pl.BlockSpec((1, L, CONV_CH), lambda b, c: (b, c, 0)),
                  pl.BlockSpec((1, L, LANES), lambda b, c: (b, c, 0)),
                  pl.BlockSpec((1, LANES, L), lambda b, c: (b, 0, c)),
                  pl.BlockSpec((1, L, SSD_WIDTH), lambda b, c: (b, c, 0)),
                  pl.BlockSpec((1, pad, CONV_CH), lambda b, c: (b, 0, 0)),
                  pl.BlockSpec((1, SSD_HEADS // 2, 2 * HEAD_DIM, D_STATE), lambda b, c: (b, 0, 0, 0)),
                  ] + [full(p.shape) for p in params],
        out_specs=[pl.BlockSpec((1, L, SSD_WIDTH), lambda b, c: (b, c, 0)),
                   pl.BlockSpec((1, SSD_HEADS // 2, 2 * HEAD_DIM, D_STATE), lambda b, c: (b, 0, 0, 0))],
        out_shape=[jax.ShapeDtypeStruct((bn, t, SSD_WIDTH), BF16),
                   jax.ShapeDtypeStruct((bn, SSD_HEADS // 2, 2 * HEAD_DIM, D_STATE), F32)],
        scratch_shapes=[pltpu.VMEM((L + pad, CONV_CH), F32),
                        pltpu.VMEM((SSD_HEADS // 2, 2 * HEAD_DIM, D_STATE), F32)],
        compiler_params=_cparams(("arbitrary", "arbitrary")),
        name="ssd",
    )(xbc, sm, smt, zs, conv0, hp2, *params)
    return y, hout.reshape(bn, SSD_HEADS, HEAD_DIM, D_STATE)


def _ssd_params(conv_w, conv_b, dt_bias, a_log, d_skip, norm_g):
    zpad = jnp.zeros((LANES - SSD_HEADS,), F32)
    dtb = jnp.concatenate([dt_bias.astype(F32), zpad])
    alog = jnp.concatenate([a_log.astype(F32), zpad])
    dsk = jnp.repeat(d_skip.astype(F32), HEAD_DIM).reshape(1, SSD_WIDTH)
    return (conv_w, conv_b.reshape(1, CONV_CH), dtb.reshape(1, LANES), dtb.reshape(LANES, 1),
            alog.reshape(1, LANES), alog.reshape(LANES, 1), dsk, norm_g.reshape(1, SSD_WIDTH))


def _half_mask(shape, kvh):
    lane = lax.broadcasted_iota(jnp.int32, shape, 1)
    return (lane >= HEAD_DIM) if kvh else (lane < HEAD_DIM)


def _stack_heads(q, kvh):
    tq = q.shape[0]
    keep = _half_mask((tq, LANES), kvh)
    blocks = []
    for g in range(GQA_GROUP):
        h = kvh * GQA_GROUP + g
        v = q[:, LANES * (h // 2):LANES * (h // 2 + 1)]
        if (h % 2) != kvh:
            v = pltpu.roll(v, HEAD_DIM, 1)
        blocks.append(jnp.where(keep, v, 0.0))
    return jnp.concatenate(blocks, axis=0)


def _unstack_heads(o_by_kvh, gates, branch, src_half_is_kvh=True):
    tq = gates.shape[0]
    lane = lax.broadcasted_iota(jnp.int32, (tq, LANES), 1)
    blocks = []
    for h in range(ATT_HEADS):
        kvh, g = divmod(h, GQA_GROUP)
        v = o_by_kvh[kvh][g * tq:(g + 1) * tq, :]
        gl = GATE_LANE0 + branch * ATT_HEADS + h
        v = v * gates[:, gl:gl + 1]
        src_half = kvh if src_half_is_kvh else 0
        if (h % 2) != src_half:
            v = pltpu.roll(v, HEAD_DIM, 1)
        blocks.append(v)
    outs = [jnp.where(lane < HEAD_DIM, blocks[2 * j], blocks[2 * j + 1]) for j in range(ATT_HEADS // 2)]
    return jnp.concatenate(outs, axis=1)


def _tile_rows(v, n):
    return jnp.concatenate([v] * n, axis=0)


def _flash_update(s, vaug, m_ref, acc_ref, kvh):
    m_prev = m_ref[kvh]
    m_new = jnp.maximum(m_prev, jnp.max(s, axis=1, keepdims=True))
    alpha = jnp.exp(m_prev - m_new)
    p = jnp.exp(s - m_new[:, 0:1])
    pv = jnp.dot(p.astype(BF16), vaug, preferred_element_type=F32)
    acc_ref[kvh] = acc_ref[kvh] * jnp.concatenate([alpha, alpha], axis=1) + pv
    m_ref[kvh] = m_new


def _flash_update_t(s, vaug_t, m_ref, acc_ref, kvh):
    m_prev = m_ref[kvh]
    m_new = jnp.maximum(m_prev, jnp.max(s, axis=1, keepdims=True))
    alpha = jnp.exp(m_prev - m_new)
    p = jnp.exp(s - m_new[:, 0:1])
    pv = lax.dot_general(p.astype(BF16), vaug_t, (((1,), (1,)), ((), ())), preferred_element_type=F32)
    acc_ref[kvh] = acc_ref[kvh] * jnp.concatenate([alpha, alpha], axis=1) + pv
    m_ref[kvh] = m_new


def _flash_out(acc_ref, kvh):
    acc = acc_ref[kvh]
    return acc[:, :LANES] / acc[:, LANES:]


def _vaug(v01):
    return jnp.concatenate([v01.astype(BF16), jnp.ones(v01.shape, BF16)], axis=1)


def _topk_cols(score_t, rounds):
    nb = score_t.shape[0]
    ridx = lax.broadcasted_iota(jnp.int32, score_t.shape, 0)
    sel = jnp.zeros(score_t.shape, F32)
    cur = score_t
    for _ in range(rounds):
        mx = jnp.max(cur, axis=0, keepdims=True)
        idx = jnp.min(jnp.where(cur == mx, ridx, nb), axis=0, keepdims=True)
        hit = ridx == idx
        sel = jnp.where(hit & (mx > -jnp.inf), 1.0, sel)
        cur = jnp.where(hit, -jnp.inf, cur)
    return sel


def _topk_rows(score, rounds):
    nb = score.shape[1]
    lidx = lax.broadcasted_iota(jnp.int32, score.shape, 1)
    ahead = jnp.zeros(score.shape, jnp.int32)
    for r in range(1, nb):
        other = pltpu.roll(score, r, 1)
        wins = (other > score) | ((other == score) & (lidx >= r))
        ahead = ahead + jnp.where(wins, 1, 0)
    return jnp.where((ahead < rounds) & (score > -jnp.inf), 1.0, 0.0)


def _cmp_attend(q, kc01, vc01, pos):
    tq = q.shape[0]
    ncl = kc01.shape[0]
    rows = GQA_GROUP * tq
    lane = lax.broadcasted_iota(jnp.int32, (rows, ncl), 1)
    cblk = 2 * (lane % MAX_SEL_LANES) + lane // MAX_SEL_LANES
    c_end = (cblk + 1) * CMP_BLOCK - 1
    mask = c_end <= _tile_rows(pos, GQA_GROUP)
    kcb = kc01.astype(BF16)
    vcb = vc01.astype(BF16)
    outs, imps = [], []
    for kvh in range(KV_HEADS):
        qs = _stack_heads(q, kvh).astype(BF16)
        s = lax.dot_general(qs, kcb, (((1,), (1,)), ((), ())), preferred_element_type=F32)
        s = jnp.where(mask, s, -jnp.inf)
        mx = jnp.max(s, axis=1, keepdims=True)
        mx = jnp.where(mx > -jnp.inf, mx, 0.0)
        e = jnp.exp(s - mx)
        dsum = jnp.sum(e, axis=1, keepdims=True)
        p = e / jnp.where(dsum > 0, dsum, 1.0)
        outs.append(jnp.dot(p.astype(BF16), vcb, preferred_element_type=F32))
        imp = p[0:tq]
        for g in range(1, GQA_GROUP):
            imp = imp + p[g * tq:(g + 1) * tq]
        imps.append(imp[:, :MAX_SEL_LANES] + imp[:, MAX_SEL_LANES:])
    return outs, imps


def _cmp_prompt_kernel(q_ref, kc_ref, vc_ref, sm_ref, o_ref, bias_ref):
    tq = q_ref.shape[1]
    i = pl.program_id(1)
    pos = i * tq + lax.broadcasted_iota(jnp.int32, (tq, 1), 0)
    q = q_ref[0] * SCALE
    outs, imps = _cmp_attend(q, kc_ref[0], vc_ref[0], pos)
    gates = jax.nn.sigmoid(sm_ref[0])
    o_ref[0] = _unstack_heads(outs, gates, 0)
    blk = lax.broadcasted_iota(jnp.int32, (tq, MAX_SEL_LANES), 1)
    cur = pos // SEL_BLOCK
    valid = blk <= cur
    forced = (blk == 0) | ((cur - blk >= 0) & (cur - blk < N_LOCAL_BLOCKS))
    for kvh in range(KV_HEADS):
        score = jnp.where(valid & jnp.logical_not(forced), imps[kvh], -jnp.inf)
        sel_t = _topk_cols(score.T, TOP_N - 1 - N_LOCAL_BLOCKS)
        keep_t = jnp.where(forced & valid, 1.0, 0.0).T
        bias_ref[0, kvh] = jnp.where((sel_t > 0.5) | (keep_t > 0.5), 0.0, NEG).astype(BF16)


def _cmp_prompt(q, kcm_perm, sm, tq):
    b, t, _ = q.shape
    ncl = kcm_perm.shape[1]
    return pl.pallas_call(
        _cmp_prompt_kernel,
        grid=(b, t // tq),
        in_specs=[pl.BlockSpec((1, tq, ATT_WIDTH), lambda bb, i: (bb, i, 0)),
                  pl.BlockSpec((1, ncl, LANES), lambda bb, i: (bb, 0, 0)),
                  pl.BlockSpec((1, ncl, LANES), lambda bb, i: (bb, 0, 1)),
                  pl.BlockSpec((1, tq, LANES), lambda bb, i: (bb, i, 0))],
        out_specs=[pl.BlockSpec((1, tq, ATT_WIDTH), lambda bb, i: (bb, i, 0)),
                   pl.BlockSpec((1, KV_HEADS, MAX_SEL_LANES, tq), lambda bb, i: (bb, 0, 0, i))],
        out_shape=[jax.ShapeDtypeStruct((b, t, ATT_WIDTH), F32),
                   jax.ShapeDtypeStruct((b, KV_HEADS, MAX_SEL_LANES, t), BF16)],
        compiler_params=_cparams(("arbitrary", "arbitrary")),
        name="cmp_prompt",
    )(q, kcm_perm, kcm_perm, sm)


def _build_qaug(q, bias_ref, qaug_ref):
    for kvh in range(KV_HEADS):
        qs = _stack_heads(q, kvh).astype(BF16)
        bias = _tile_rows(bias_ref[0, kvh], GQA_GROUP)
        qaug_ref[kvh] = jnp.concatenate([qs, bias], axis=1)


def _head_rows(qt, h):
    blk = qt[h * HEAD_DIM:(h + 1) * HEAD_DIM]
    z = jnp.zeros_like(blk)
    return jnp.concatenate([blk, z] if h // GQA_GROUP == 0 else [z, blk], axis=0)


def _emit_heads(o_by_head, sm_ref, branch, o_ref):
    gates_t = jax.nn.sigmoid(sm_ref[0]).T
    cols = []
    for j in range(ATT_HEADS // 2):
        pair = []
        for h in (2 * j, 2 * j + 1):
            gl = GATE_LANE0 + branch * ATT_HEADS + h
            pair.append(o_by_head[h] * gates_t[gl:gl + 1, :])
        cols.append(jnp.concatenate(pair, axis=0).T)
    o_ref[0] = jnp.concatenate(cols, axis=1)


def _pipeline_order(n, lead):
    order = [("qk", h) for h in range(min(lead, n))]
    for h in range(n):
        order.append(("sm", h))
        if h + lead < n:
            order.append(("qk", h + lead))
        if h >= 1:
            order.append(("pv", h - 1))
    order.append(("pv", n - 1))
    return order


_SLC_ORDER = _pipeline_order(ATT_HEADS, 8)
_SLC_TILES_PER_ITER = 2


def _slc_prompt_kernel(q_ref, bias_ref, ka_ref, vat_ref, sm_ref, o_ref, qa_ref, m_ref, acc_ref):
    tq = q_ref.shape[1]
    tk = tq
    i = pl.program_id(1)
    qt = (q_ref[0] * SCALE_LOG2).T
    for h in range(ATT_HEADS):
        qa_ref[h] = jnp.concatenate([bias_ref[0, h // GQA_GROUP], _head_rows(qt, h).astype(BF16)], axis=0)
    m_ref[...] = jnp.full(m_ref.shape, NEG, F32)
    acc_ref[...] = jnp.zeros(acc_ref.shape, F32)

    def tiles(j0, n, masked):
        k0s = [pl.multiple_of((j0 + t) * tk, tk) for t in range(n)]
        kas = [ka_ref[0, pl.ds(k0, tk), :] for k0 in k0s]
        ss, ps, alphas = {}, {}, {}

        def qk(h):
            ss[h] = []
            for ka in kas:
                s = jnp.dot(ka, qa_ref[h], preferred_element_type=F32)
                if masked:
                    kofs = lax.broadcasted_iota(jnp.int32, (tk, tq), 0)
                    qofs = lax.broadcasted_iota(jnp.int32, (tk, tq), 1)
                    s = jnp.where(kofs <= qofs, s, NEG)
                ss[h].append(s)

        def sm(h):
            m_prev = m_ref[h]
            m_new = m_prev
            for s in ss[h]:
                m_new = jnp.maximum(m_new, jnp.max(s, axis=0, keepdims=True))
            alphas[h] = jnp.exp2(m_prev - m_new)
            ps[h] = [jnp.exp2(s - m_new).astype(BF16) for s in ss[h]]
            m_ref[h] = m_new

        def pv(h):
            acc = acc_ref[h] * alphas[h]
            for k0, p in zip(k0s, ps[h]):
                acc = acc + jnp.dot(vat_ref[0, h // GQA_GROUP, :, pl.ds(k0, tk)], p, preferred_element_type=F32)
            acc_ref[h] = acc

        for step in _SLC_ORDER:
            {"qk": qk, "sm": sm, "pv": pv}[step[0]](step[1])

    def body(jq, carry):
        tiles(_SLC_TILES_PER_ITER * jq, _SLC_TILES_PER_ITER, False)
        return carry

    lax.fori_loop(0, i // _SLC_TILES_PER_ITER, body, 0)
    done = (i // _SLC_TILES_PER_ITER) * _SLC_TILES_PER_ITER
    n = _SLC_TILES_PER_ITER // 2
    while n >= 1:
        take = ((i - done) // n) % 2 == 1

        @pl.when(take)
        def _(done=done, n=n):
            tiles(done, n, False)

        done = done + jnp.where(take, n, 0)
        n //= 2

    tiles(i, 1, True)
    outs = []
    for h in range(ATT_HEADS):
        acc = acc_ref[h]
        outs.append(acc[:HEAD_DIM] / acc[HEAD_DIM:HEAD_DIM + 1])
    _emit_heads(outs, sm_ref, 1, o_ref)


def _slc_prompt(q, bias_t, ka, vat, sm, tq):
    b, t, _ = q.shape
    kc = ka.shape[2]
    return pl.pallas_call(
        _slc_prompt_kernel,
        grid=(b, t // tq),
        in_specs=[pl.BlockSpec((1, tq, ATT_WIDTH), lambda bb, i: (bb, i, 0)),
                  pl.BlockSpec((1, KV_HEADS, MAX_SEL_LANES, tq), lambda bb, i: (bb, 0, 0, i)),
                  pl.BlockSpec((1, t, kc), lambda bb, i: (bb, 0, 0)),
                  pl.BlockSpec((1, KV_HEADS, VAT_ROWS, t), lambda bb, i: (bb, 0, 0, 0)),
                  pl.BlockSpec((1, tq, LANES), lambda bb, i: (bb, i, 0))],
        out_specs=pl.BlockSpec((1, tq, ATT_WIDTH), lambda bb, i: (bb, i, 0)),
        out_shape=jax.ShapeDtypeStruct((b, t, ATT_WIDTH), F32),
        scratch_shapes=[pltpu.VMEM((ATT_HEADS, kc, tq), BF16),
                        pltpu.VMEM((ATT_HEADS, 1, tq), F32),
                        pltpu.VMEM((ATT_HEADS, VAT_ROWS, tq), F32)],
        compiler_params=_cparams(("arbitrary", "arbitrary")),
        name="slc_prompt",
    )(q, bias_t, ka, vat, sm)


def _win_prompt_kernel(q_ref, sm_ref, *refs, n_prev):
    nt = n_prev + 1
    k_refs, vat_refs, o_ref = refs[:nt], refs[nt:2 * nt], refs[2 * nt]
    tq = q_ref.shape[1]
    i = pl.program_id(1)
    kofs = lax.broadcasted_iota(jnp.int32, (tq, tq), 0)
    qofs = lax.broadcasted_iota(jnp.int32, (tq, tq), 1)
    qt = (q_ref[0] * SCALE_LOG2).T
    scores = []
    for h in range(ATT_HEADS):
        qh = _head_rows(qt, h).astype(BF16)
        ss = []
        for n in range(nt):
            s = jnp.dot(k_refs[n][0], qh, preferred_element_type=F32)
            back = n_prev - n
            ok = i >= back
            if n == 0:
                ok = ok & (kofs >= qofs)
            if back == 0:
                ok = kofs <= qofs
            ss.append(jnp.where(ok, s, NEG))
        scores.append(ss)
    probs = []
    for ss in scores:
        mx = ss[0].max(axis=0, keepdims=True)
        for s in ss[1:]:
            mx = jnp.maximum(mx, s.max(axis=0, keepdims=True))
        probs.append([jnp.exp2(s - mx).astype(BF16) for s in ss])
    outs = []
    for h in range(ATT_HEADS):
        acc = None
        for n in range(nt):
            pv = jnp.dot(vat_refs[n][0, h // GQA_GROUP], probs[h][n], preferred_element_type=F32)
            acc = pv if acc is None else acc + pv
        outs.append(acc[:HEAD_DIM] / acc[HEAD_DIM:HEAD_DIM + 1])
    _emit_heads(outs, sm_ref, 2, o_ref)


def _win_prompt(q, kw, vat, sm, tq):
    b, t, _ = q.shape
    assert WINDOW % tq == 0 and WINDOW >= tq
    n_prev = WINDOW // tq

    def k_spec(back):
        return pl.BlockSpec((1, tq, KV_WIDTH), lambda bb, i: (bb, jnp.maximum(i - back, 0), 0))

    def v_spec(back):
        return pl.BlockSpec((1, KV_HEADS, VAT_ROWS, tq), lambda bb, i: (bb, 0, 0, jnp.maximum(i - back, 0)))

    backs = [n_prev - n for n in range(n_prev + 1)]
    nt = n_prev + 1
    return pl.pallas_call(
        functools.partial(_win_prompt_kernel, n_prev=n_prev),
        grid=(b, t // tq),
        in_specs=([pl.BlockSpec((1, tq, ATT_WIDTH), lambda bb, i: (bb, i, 0)),
                   pl.BlockSpec((1, tq, LANES), lambda bb, i: (bb, i, 0))]
                  + [k_spec(bk) for bk in backs] + [v_spec(bk) for bk in backs]),
        out_specs=pl.BlockSpec((1, tq, ATT_WIDTH), lambda bb, i: (bb, i, 0)),
        out_shape=jax.ShapeDtypeStruct((b, t, ATT_WIDTH), F32),
        compiler_params=_cparams(("arbitrary", "arbitrary")),
        name="win_prompt",
    )(q, sm, *([kw] * nt), *([vat] * nt))


CMP_PAGES = LANES * CMP_BLOCK // PAGE_SIZE


def _cmp_local_block(lane):
    half = LANES // 2
    return 2 * (lane % half) + lane // half


def _block_mean_matrix(n_pages):
    tok = jnp.arange(n_pages * PAGE_SIZE, dtype=jnp.int32)[:, None] // CMP_BLOCK
    col = jnp.arange(n_pages * PAGE_SIZE // CMP_BLOCK, dtype=jnp.int32)[None, :]
    blk = LANES * (col // LANES) + _cmp_local_block(col % LANES)
    return jnp.where(tok == blk, 1.0 / CMP_BLOCK, 0.0).astype(BF16)


def _cmp_sample_kernel(pt_ref, *refs, n_pg, past):
    page_refs = refs[:n_pg]
    a_ref, q_ref, sm_ref, o_ref, bias_ref, kct_ref = refs[n_pg:]
    s = pl.program_id(1)
    ns = pl.num_programs(1)
    n_chunks = kct_ref.shape[0]
    tn = q_ref.shape[1]
    rows = GQA_GROUP * tn

    x = jnp.concatenate([r[0] for r in page_refs], axis=1)
    hi = x.astype(BF16)
    lo = (x - hi.astype(F32)).astype(BF16)
    a = a_ref[...]
    means = jnp.dot(hi, a, preferred_element_type=F32) + jnp.dot(lo, a, preferred_element_type=F32)
    for c in range(n_chunks):
        kct_ref[c] = means[:, c * LANES:(c + 1) * LANES]

    @pl.when(s == ns - 1)
    def _():
        pos = past + lax.broadcasted_iota(jnp.int32, (tn, 1), 0)
        lane = lax.broadcasted_iota(jnp.int32, (rows, LANES), 1)
        q = q_ref[0] * SCALE
        outs, imps = [], []
        for kvh in range(KV_HEADS):
            qs = _stack_heads(q, kvh).astype(BF16)
            ss = []
            for c in range(n_chunks):
                sc = jnp.dot(qs, kct_ref[c, :KV_WIDTH, :].astype(BF16), preferred_element_type=F32)
                c_end = (c * LANES + _cmp_local_block(lane) + 1) * CMP_BLOCK - 1
                ss.append(jnp.where(c_end <= _tile_rows(pos, GQA_GROUP), sc, -jnp.inf))
            mx = ss[0].max(axis=1, keepdims=True)
            for sc in ss[1:]:
                mx = jnp.maximum(mx, sc.max(axis=1, keepdims=True))
            mx = jnp.where(mx > -jnp.inf, mx, 0.0)
            es = [jnp.exp(sc - mx) for sc in ss]
            dsum = es[0].sum(axis=1, keepdims=True)
            for e in es[1:]:
                dsum = dsum + e.sum(axis=1, keepdims=True)
            inv = 1.0 / jnp.where(dsum > 0, dsum, 1.0)
            o = None
            imp_blocks = []
            for c in range(n_chunks):
                p = es[c] * inv
                vc = kct_ref[c, KV_WIDTH:, :].astype(BF16)
                pv = lax.dot_general(p.astype(BF16), vc, (((1,), (1,)), ((), ())), preferred_element_type=F32)
                o = pv if o is None else o + pv
                imp = p[0:tn]
                for g in range(1, GQA_GROUP):
                    imp = imp + p[g * tn:(g + 1) * tn]
                imp_blocks.append(imp + pltpu.roll(imp, LANES // 2, 1))
            outs.append(o)
            if n_chunks == 1:
                imps.append(imp_blocks[0])
            else:
                lane_t = lax.broadcasted_iota(jnp.int32, (tn, LANES), 1)
                imps.append(jnp.where(lane_t < LANES // 2, imp_blocks[0], pltpu.roll(imp_blocks[1], LANES // 2, 1)))
        o_ref[0] = _unstack_heads(outs, jax.nn.sigmoid(sm_ref[0]), 0)
        n_past = past // SEL_BLOCK
        blk = lax.broadcasted_iota(jnp.int32, (tn, MAX_SEL_LANES), 1)
        forced = (blk == 0) | (blk == n_past - 1)
        rounds = min(TOP_N, n_past + 1) - 1
        for kvh in range(KV_HEADS):
            score = jnp.where(forced, jnp.inf, imps[kvh])
            score = jnp.where(blk < n_past, score, -jnp.inf)
            sel = _topk_rows(score, rounds)
            bias_ref[0, kvh] = jnp.where(sel > 0.5, 0.0, NEG).astype(BF16)


def _page_specs(n_pg):
    def spec(k):
        return pl.BlockSpec((1, 2 * KV_WIDTH, PAGE_SIZE), lambda b, s, pt: (pt[b, s * n_pg + k], 0, 0))
    return [spec(k) for k in range(n_pg)]


def _cmp_sample(page_table, pool_t, q, sm):
    bs, tn, _ = q.shape
    n_pages = page_table.shape[1]
    past = n_pages * PAGE_SIZE
    n_pg = n_pages
    assert n_pages % CMP_PAGES == 0 and n_pages // CMP_PAGES <= 2
    n_chunks = n_pages // CMP_PAGES
    grid_spec = pltpu.PrefetchScalarGridSpec(
        num_scalar_prefetch=1,
        grid=(bs, 1),
        in_specs=_page_specs(n_pg) + [
            pl.BlockSpec((n_pg * PAGE_SIZE, n_chunks * LANES), lambda b, s, pt: (0, 0)),
            pl.BlockSpec((1, tn, ATT_WIDTH), lambda b, s, pt: (b, 0, 0)),
            pl.BlockSpec((1, tn, LANES), lambda b, s, pt: (b, 0, 0))],
        out_specs=[pl.BlockSpec((1, tn, ATT_WIDTH), lambda b, s, pt: (b, 0, 0)),
                   pl.BlockSpec((1, KV_HEADS, tn, LANES), lambda b, s, pt: (b, 0, 0, 0))],
        scratch_shapes=[pltpu.VMEM((n_chunks, 2 * KV_WIDTH, LANES), F32)],
    )
    return pl.pallas_call(
        functools.partial(_cmp_sample_kernel, n_pg=n_pg, past=past),
        grid_spec=grid_spec,
        out_shape=[jax.ShapeDtypeStruct((bs, tn, ATT_WIDTH), F32),
                   jax.ShapeDtypeStruct((bs, KV_HEADS, tn, LANES), BF16)],
        compiler_params=_cparams(("arbitrary", "arbitrary")),
        name="cmp_sample",
    )(page_table, *([pool_t] * n_pg), _block_mean_matrix(n_pages), q, sm)


def _slc_sample_kernel(pt_ref, *refs, n_pg, n_seq, past):
    page_refs = [refs[u * n_pg:(u + 1) * n_pg] for u in range(n_seq)]
    q_ref, bias_ref, new_ref, sm_ref, o_ref, qaug_ref, m_ref, acc_ref = refs[n_seq * n_pg:]
    s = pl.program_id(1)
    ns = pl.num_programs(1)
    tn = q_ref.shape[1]
    rows = GQA_GROUP * tn

    nq = KV_HEADS * rows

    @pl.when(s == 0)
    def _():
        for u in range(n_seq):
            q = q_ref[u] * SCALE
            blocks = [jnp.concatenate([_stack_heads(q, kvh).astype(BF16),
                                       _tile_rows(bias_ref[u, kvh], GQA_GROUP)], axis=1)
                      for kvh in range(KV_HEADS)]
            qaug_ref[u] = jnp.concatenate(blocks + [jnp.zeros((LANES - nq, 2 * LANES), BF16)], axis=0)
        m_ref[...] = jnp.full(m_ref.shape, NEG, F32)
        acc_ref[...] = jnp.zeros(acc_ref.shape, F32)

    def update(sts, vaugs):
        m_news, alphas, ps = [], [], []
        for u in range(n_seq):
            m_prev = m_ref[u]
            m_new = m_prev
            for st in sts[u]:
                m_new = jnp.maximum(m_new, jnp.max(st, axis=0, keepdims=True))
            m_news.append(m_new)
            alphas.append(jnp.exp(m_prev - m_new))
            ps.append([jnp.exp(st - m_new).astype(BF16) for st in sts[u]])
        for u in range(n_seq):
            acc = acc_ref[u] * alphas[u]
            for p, va in zip(ps[u], vaugs[u]):
                acc = acc + jnp.dot(va, p, preferred_element_type=F32)
            acc_ref[u] = acc
            m_ref[u] = m_news[u]

    tk = n_pg * PAGE_SIZE
    kblk = (s * tk + lax.broadcasted_iota(jnp.int32, (MAX_SEL_LANES, tk), 1)) // SEL_BLOCK
    onehot = jnp.where(kblk == lax.broadcasted_iota(jnp.int32, (MAX_SEL_LANES, tk), 0), 1.0, 0.0).astype(BF16)
    n_piece = 2
    w = tk // n_piece
    sts, vaugs = [], []
    for u in range(n_seq):
        x = jnp.concatenate([r[0] for r in page_refs[u]], axis=1)
        kaug_t = jnp.concatenate([x[:KV_WIDTH].astype(BF16), onehot], axis=0)
        vaug_t = jnp.concatenate([x[KV_WIDTH:].astype(BF16), jnp.ones((BF16_SUBLANES, tk), BF16)], axis=0)
        qa = qaug_ref[u]
        sts.append([lax.dot_general(kaug_t[:, i * w:(i + 1) * w], qa, (((0,), (1,)), ((), ())),
                                    preferred_element_type=F32) for i in range(n_piece)])
        vaugs.append([vaug_t[:, i * w:(i + 1) * w] for i in range(n_piece)])
    update(sts, vaugs)

    @pl.when(s == ns - 1)
    def _():
        nk = new_ref.shape[1]
        kidx = lax.broadcasted_iota(jnp.int32, (nk, LANES), 0)
        qidx = lax.broadcasted_iota(jnp.int32, (nk, LANES), 1) % tn
        sts, vaugs = [], []
        for u in range(n_seq):
            xn = new_ref[u]
            st = lax.dot_general(xn[:, :LANES].astype(BF16), qaug_ref[u, :, :LANES], (((1,), (1,)), ((), ())),
                                 preferred_element_type=F32)
            sts.append([jnp.where(kidx <= qidx, st, NEG)])
            vaugs.append([jnp.concatenate([xn[:, LANES:].T.astype(BF16), jnp.ones((BF16_SUBLANES, nk), BF16)],
                                          axis=0)])
        update(sts, vaugs)
        for u in range(n_seq):
            acc = acc_ref[u]
            o_t = (acc[:KV_WIDTH] / acc[KV_WIDTH:KV_WIDTH + 1]).T
            outs = [o_t[kvh * rows:(kvh + 1) * rows] for kvh in range(KV_HEADS)]
            o_ref[u] = _unstack_heads(outs, jax.nn.sigmoid(sm_ref[u]), 1)


def _slc_sample(page_table, pool, q, bias, new_pad, sm, n_pg):
    bs, tn, _ = q.shape
    n_pages = page_table.shape[1]
    past = n_pages * PAGE_SIZE
    rows = GQA_GROUP * tn
    npad = new_pad.shape[1]
    n_seq = 2
    assert bs % n_seq == 0 and KV_HEADS * rows <= LANES

    def page_spec(u, k):
        return pl.BlockSpec((1, 2 * KV_WIDTH, PAGE_SIZE),
                            lambda b, s, pt: (pt[n_seq * b + u, s * n_pg + k], 0, 0))

    grid_spec = pltpu.PrefetchScalarGridSpec(
        num_scalar_prefetch=1,
        grid=(bs // n_seq, n_pages // n_pg),
        in_specs=[page_spec(u, k) for u in range(n_seq) for k in range(n_pg)] + [
            pl.BlockSpec((n_seq, tn, ATT_WIDTH), lambda b, s, pt: (b, 0, 0)),
            pl.BlockSpec((n_seq, KV_HEADS, tn, LANES), lambda b, s, pt: (b, 0, 0, 0)),
            pl.BlockSpec((n_seq, npad, 2 * KV_WIDTH), lambda b, s, pt: (b, 0, 0)),
            pl.BlockSpec((n_seq, tn, LANES), lambda b, s, pt: (b, 0, 0))],
        out_specs=pl.BlockSpec((n_seq, tn, ATT_WIDTH), lambda b, s, pt: (b, 0, 0)),
        scratch_shapes=[pltpu.VMEM((n_seq, LANES, 2 * LANES), BF16),
                        pltpu.VMEM((n_seq, 1, LANES), F32),
                        pltpu.VMEM((n_seq, KV_WIDTH + BF16_SUBLANES, LANES), F32)],
    )
    return pl.pallas_call(
        functools.partial(_slc_sample_kernel, n_pg=n_pg, n_seq=n_seq, past=past),
        grid_spec=grid_spec,
        out_shape=jax.ShapeDtypeStruct((bs, tn, ATT_WIDTH), F32),
        compiler_params=_cparams(("arbitrary", "arbitrary")),
        name="slc_sample",
    )(page_table, *([pool] * (n_seq * n_pg)), q, bias, new_pad, sm)


def _win_sample_kernel(q_ref, wint_ref, new_ref, sm_ref, o_ref, *, past):
    nb, tn = q_ref.shape[0], q_ref.shape[1]
    rows = GQA_GROUP * tn
    wb = wint_ref.shape[2]
    npad = new_ref.shape[1]
    kidx = lax.broadcasted_iota(jnp.int32, (rows, wb), 1)
    t_w = lax.broadcasted_iota(jnp.int32, (rows, wb), 0) % tn
    ok_w = (wb + t_w - kidx <= WINDOW) & (past - wb + kidx >= 0)
    ok_n = (lax.broadcasted_iota(jnp.int32, (rows, npad), 1)
            <= lax.broadcasted_iota(jnp.int32, (rows, npad), 0) % tn)
    chains = [(sq, kvh) for sq in range(nb) for kvh in range(KV_HEADS)]
    scores, probs = {}, {}
    for sq, kvh in chains:
        qs = _stack_heads(q_ref[sq] * SCALE, kvh).astype(BF16)
        kt = wint_ref[sq, :KV_WIDTH, :].astype(BF16)
        kn = new_ref[sq, :, :KV_WIDTH].astype(BF16)
        s_w = jnp.where(ok_w, jnp.dot(qs, kt, preferred_element_type=F32), -jnp.inf)
        s_n = jnp.where(ok_n, lax.dot_general(qs, kn, (((1,), (1,)), ((), ())), preferred_element_type=F32),
                        -jnp.inf)
        scores[sq, kvh] = (s_w, s_n)
    for key in chains:
        s_w, s_n = scores[key]
        mx = jnp.maximum(jnp.max(s_w, axis=1, keepdims=True), jnp.max(s_n, axis=1, keepdims=True))
        e_w, e_n = jnp.exp(s_w - mx), jnp.exp(s_n - mx)
        inv = 1.0 / (jnp.sum(e_w, axis=1, keepdims=True) + jnp.sum(e_n, axis=1, keepdims=True))
        probs[key] = ((e_w * inv).astype(BF16), (e_n * inv).astype(BF16))
    for sq in range(nb):
        vt = wint_ref[sq, KV_WIDTH:, :].astype(BF16)
        vn = new_ref[sq, :, KV_WIDTH:].astype(BF16)
        outs = []
        for kvh in range(KV_HEADS):
            p_w, p_n = probs[sq, kvh]
            outs.append(lax.dot_general(p_w, vt, (((1,), (1,)), ((), ())), preferred_element_type=F32)
                        + jnp.dot(p_n, vn, preferred_element_type=F32))
        o_ref[sq] = _unstack_heads(outs, jax.nn.sigmoid(sm_ref[sq]), 2)


def _win_sample(q, win_t, new_pad, sm, past, nb):
    bs, tn, _ = q.shape
    wb = win_t.shape[2]
    npad = new_pad.shape[1]
    assert bs % nb == 0 and tn <= WINDOW
    return pl.pallas_call(
        functools.partial(_win_sample_kernel, past=past),
        grid=(bs // nb,),
        in_specs=[pl.BlockSpec((nb, tn, ATT_WIDTH), lambda b: (b, 0, 0)),
                  pl.BlockSpec((nb, 2 * KV_WIDTH, wb), lambda b: (b, 0, 0)),
                  pl.BlockSpec((nb, npad, 2 * KV_WIDTH), lambda b: (b, 0, 0)),
                  pl.BlockSpec((nb, tn, LANES), lambda b: (b, 0, 0))],
        out_specs=pl.BlockSpec((nb, tn, ATT_WIDTH), lambda b: (b, 0, 0)),
        out_shape=jax.ShapeDtypeStruct((bs, tn, ATT_WIDTH), F32),
        compiler_params=_cparams(("arbitrary",)),
        name="win_sample",
    )(q, win_t, new_pad, sm)


def _out_kernel(x_ref, oc_ref, os_ref, ow_ref, za_ref, ys_ref, gate_ref, ang_ref, wo_ref, fg_ref, y_ref):
    nbk, tt, d = x_ref.shape
    m = nbk * tt
    o = (oc_ref[...] + os_ref[...]) + ow_ref[...]
    t = o * _silu(za_ref[...])
    ms = jnp.mean(t * t, axis=-1, keepdims=True)
    y_att = (t * lax.rsqrt(ms + EPS)) * ang_ref[...].reshape(1, 1, ATT_WIDTH)
    ya = y_att.reshape(m, ATT_WIDTH).astype(BF16)
    ys = ys_ref[...].reshape(m, SSD_WIDTH)
    mix = (jnp.dot(ys, wo_ref[:SSD_WIDTH, :], preferred_element_type=F32)
           + jnp.dot(ya, wo_ref[SSD_WIDTH:, :], preferred_element_type=F32))
    xp = x_ref[...] + gate_ref[...] * mix.reshape(nbk, tt, d)
    ms2 = jnp.mean(xp * xp, axis=-1, keepdims=True)
    y_ref[...] = (xp * lax.rsqrt(ms2 + EPS)) * fg_ref[...].reshape(1, 1, d)


def _out(x3, o_c, o_s, o_w, za, y_ssd, gate, att_norm_g, w_out_b, final_g, *, nbk, groups_per_mod):
    g_total, tt, d = x3.shape
    steps = g_total // nbk
    if groups_per_mod is None:
        mod_spec = pl.BlockSpec((nbk, 1, d), lambda g: (g, 0, 0))
    else:
        mod_spec = pl.BlockSpec((1, 1, d), lambda g: (g // groups_per_mod, 0, 0))

    def tok_spec(c):
        return pl.BlockSpec((nbk, tt, c), lambda g: (g, 0, 0))

    return pl.pallas_call(
        _out_kernel,
        grid=(steps,),
        in_specs=[tok_spec(d), tok_spec(ATT_WIDTH), tok_spec(ATT_WIDTH), tok_spec(ATT_WIDTH),
                  tok_spec(ATT_WIDTH), tok_spec(SSD_WIDTH), mod_spec,
                  pl.BlockSpec((1, ATT_WIDTH), lambda g: (0, 0)),
                  pl.BlockSpec((SSD_WIDTH + ATT_WIDTH, d), lambda g: (0, 0)),
                  pl.BlockSpec((1, d), lambda g: (0, 0))],
        out_specs=tok_spec(d),
        out_shape=jax.ShapeDtypeStruct((g_total, tt, d), F32),
        compiler_params=_cparams(("arbitrary",)),
        name="outproj",
    )(x3, o_c, o_s, o_w, za, y_ssd, gate, att_norm_g.reshape(1, ATT_WIDTH), w_out_b, final_g.reshape(1, d))


def _perm_cmp_means(kcm, b):
    nc = kcm.shape[1]
    ns = nc // 2
    assert ns <= MAX_SEL_LANES
    eo = kcm.reshape(b, ns, 2, 2 * KV_WIDTH).transpose(0, 2, 1, 3)
    eo = jnp.pad(eo, ((0, 0), (0, 0), (0, MAX_SEL_LANES - ns), (0, 0)))
    return eo.reshape(b, 2 * MAX_SEL_LANES, 2 * KV_WIDTH)


def _prompt_layer(x, mod, lw, final_g, apply_final):
    b, t, d = x.shape
    shift, scale, gate = (mod[:, None, i * d:(i + 1) * d] for i in range(3))
    tt = min(SSD_CHUNK, t)
    assert t % tt == 0 and tt % CMP_BLOCK == 0 and t >= WINDOW
    gpb = t // tt
    x3 = x.reshape(b * gpb, tt, d)
    ssd_params = _ssd_params(lw["conv_w"], lw["conv_b"], lw["dt_bias"], lw["a_log"], lw["d_skip"],
                             lw["ssd_norm_g"])
    q, za, sm, kcm, kvt_c, kvt_s, kvt_w, ka_s, vat_s, kw, vat_w, y_ssd, ssm_new, xlast = _inproj_prompt(
        x, scale, shift, lw["norm_g"], lw["w_big"], lw["w_kvt"], ssd_params, tt)
    r = lambda a: a.reshape(b, t, a.shape[-1])
    q, za, sm, ka_s, kw, y_ssd = map(r, (q, za, sm, ka_s, kw, y_ssd))
    kcm = kcm.reshape(b, t // CMP_BLOCK, 2 * KV_WIDTH)
    ssm_new = ssm_new.reshape(b, SSD_HEADS, HEAD_DIM, D_STATE)

    o_c, bias = _cmp_prompt(q, _perm_cmp_means(kcm, b), sm, 128)
    tq = 256
    o_s = _slc_prompt(q, bias, ka_s, vat_s, sm, tq)
    o_w = _win_prompt(q, kw, vat_w, sm, tq)

    g3 = lambda a: a.reshape(b * gpb, tt, a.shape[-1])
    y3 = _out(x3, g3(o_c), g3(o_s), g3(o_w), g3(za), g3(y_ssd), gate, lw["att_norm_g"], lw["w_out_b"],
              final_g, nbk=1, groups_per_mod=gpb)
    assert apply_final
    kv6 = lambda a: a.reshape(b, 2, KV_HEADS, HEAD_DIM, a.shape[-1]).transpose(0, 4, 1, 2, 3)
    conv_new = xlast[:, xlast.shape[1] - (CONV_WIDTH - 1):]
    outs = (kv6(kvt_c), kv6(kvt_s), kv6(kvt_w[:, :, t - min(WINDOW, t):]), conv_new, ssm_new)
    return y3.reshape(b, t, d), outs


def _sample_layer(x, mod, lw, final_g, pool_c, pool_s, win_buf, conv_buf, ssm, page_table, apply_final):
    bs, tn, d = x.shape
    shift, scale, gate = (mod[:, None, i * d:(i + 1) * d] for i in range(3))
    n_pages = page_table.shape[1]
    past = n_pages * PAGE_SIZE
    nbk = 16
    n_pg = 16
    assert bs % nbk == 0 and tn % 8 == 0 and tn <= SEL_BLOCK and n_pages % n_pg == 0
    assert past // SEL_BLOCK <= MAX_SEL_LANES and past % SEL_BLOCK == 0
    pos = past + jnp.arange(tn, dtype=jnp.int32)
    tabs = tuple(jnp.tile(tb, (nbk, 1)) for tb in _rope_tables(pos))
    zs, xbc, q, za, sm, kvc, kvs, kvw = _inproj_sample(
        x, scale, shift, lw["norm_g"], lw["w_big"], tabs, nbk=nbk)

    y_ssd, ssm_new = _ssd(xbc, sm, zs, conv_buf, ssm, lw["conv_w"], lw["conv_b"], lw["dt_bias"], lw["a_log"],
                          lw["d_skip"], lw["ssd_norm_g"])

    npad = LANES
    to_rows = lambda p: p.transpose(0, 2, 3, 4, 1).reshape(p.shape[0], 2 * KV_WIDTH, p.shape[1])
    o_c, bias = _cmp_sample(page_table, to_rows(pool_c), q, sm)
    kvs_pad = jnp.pad(kvs, ((0, 0), (0, npad - tn), (0, 0)))
    o_s = _slc_sample(page_table, to_rows(pool_s), q, bias, kvs_pad, sm, n_pg)
    wb = win_buf.shape[1]
    win_t = to_rows(win_buf)
    kvw_pad = jnp.pad(kvw, ((0, 0), (0, npad - tn), (0, 0)))
    o_w = _win_sample(q, win_t, kvw_pad, sm, past, 8)

    y = _out(x, o_c, o_s, o_w, za, y_ssd, gate, lw["att_norm_g"], lw["w_out_b"], final_g,
             nbk=nbk, groups_per_mod=None)
    assert apply_final
    kv6 = lambda a: a.reshape(bs, a.shape[1], 2, KV_HEADS, HEAD_DIM)
    kv_w_all = jnp.concatenate([win_t, jnp.swapaxes(kvw, 1, 2)], axis=2)
    win_new_t = kv_w_all[:, :, kv_w_all.shape[2] - min(WINDOW, past + tn):]
    win_new = win_new_t.reshape(bs, 2, KV_HEADS, HEAD_DIM, win_new_t.shape[2]).transpose(0, 4, 1, 2, 3)
    conv_new = jnp.concatenate([conv_buf, xbc], axis=1)[:, tn:]
    outs = (kv6(kvc), kv6(kvs), win_new, conv_new, ssm_new)
    return y, outs


def kernel(x_prompt, x_sample, cache_cmp_kv, cache_slc_kv, state_win_kv, state_conv, state_ssm, page_table,
           c_prompt, c_sample, w_ada, b_ada, norm_g, w_in, conv_w, conv_b, dt_bias, a_log, d_skip,
           ssd_norm_g, att_norm_g, w_out, final_g):
    depth = w_ada.shape[0]
    assert depth == 1
    n_prompt = c_prompt.shape[0]
    xp, xs = x_prompt, x_sample
    out_p, out_s = [], []
    for l in range(depth):
        w_big, w_kvt = _rearrange_w_in(w_in[l])
        lw = dict(norm_g=norm_g[l], w_big=w_big, w_kvt=w_kvt, conv_w=conv_w[l], conv_b=conv_b[l],
                  dt_bias=dt_bias[l], a_log=a_log[l], d_skip=d_skip[l], ssd_norm_g=ssd_norm_g[l],
                  att_norm_g=att_norm_g[l], w_out_b=w_out[l].astype(BF16))
        mod = _mod(jnp.concatenate([c_prompt, c_sample], axis=0), w_ada[l], b_ada[l])
        last = l == depth - 1
        xp, op = _prompt_layer(xp, mod[:n_prompt], lw, final_g, last)
        xs, os_ = _sample_layer(xs, mod[n_prompt:], lw, final_g, cache_cmp_kv[l], cache_slc_kv[l],
                                state_win_kv[l], state_conv[l], state_ssm[l], page_table, last)
        out_p.append(op)
        out_s.append(os_)
    sp = [jnp.stack([o[k] for o in out_p]) for k in range(5)]
    sd = [jnp.stack([o[k] for o in out_s]) for k in range(5)]
    return (xp, xs, sp[0], sp[1], sp[2], sp[3], sp[4], sd[0], sd[1], sd[2], sd[3], sd[4])
```

```python
import functools

import jax
import jax.numpy as jnp
from jax import lax
from jax.experimental import pallas as pl
from jax.experimental.pallas import tpu as pltpu

F32 = jnp.float32
BF16 = jnp.bfloat16

HEAD_DIM = 64
SSD_HEADS = 8
SSD_WIDTH = SSD_HEADS * HEAD_DIM
SSD_GROUPS = 2
D_STATE = 128
CONV_WIDTH = 4
CONV_CH = SSD_WIDTH + 2 * SSD_GROUPS * D_STATE
SSD_CHUNK = 256
ATT_HEADS = 8
ATT_WIDTH = ATT_HEADS * HEAD_DIM
KV_HEADS = 2
GQA_GROUP = ATT_HEADS // KV_HEADS
KV_WIDTH = KV_HEADS * HEAD_DIM
CMP_BLOCK = 32
SEL_BLOCK = 64
TOP_N = 16
N_LOCAL_BLOCKS = 2
WINDOW = 512
N_BRANCH = 3
ROT_DIM = HEAD_DIM // 4
ROPE_THETA = 500000.0
PAGE_SIZE = 128
EPS = 1e-6
COL_SIZES = (SSD_WIDTH, CONV_CH, SSD_HEADS, ATT_WIDTH, 2 * KV_WIDTH, 2 * KV_WIDTH, 2 * KV_WIDTH,
             N_BRANCH * ATT_HEADS, ATT_WIDTH)

LANES = 128
MAX_SEL_LANES = LANES
NEG = -1e30
SCALE = HEAD_DIM ** -0.5
SCALE_LOG2 = SCALE * 1.4426950408889634
GATE_LANE0 = SSD_HEADS
BF16_SUBLANES = 16
VAT_ROWS = HEAD_DIM + BF16_SUBLANES
VMEM_LIMIT = 56 * 1024 * 1024

_O_ZS = 0
_O_XBC = _O_ZS + SSD_WIDTH
_O_Q = _O_XBC + CONV_CH
_O_KC = _O_Q + ATT_WIDTH
_O_KS = _O_KC + 2 * KV_WIDTH
_O_KW = _O_KS + 2 * KV_WIDTH
_O_ZA = _O_KW + 2 * KV_WIDTH
_O_SM = _O_ZA + ATT_WIDTH
_W_COLS = _O_SM + LANES


def _cparams(sem):
    return pltpu.CompilerParams(dimension_semantics=sem, vmem_limit_bytes=VMEM_LIMIT)


def _silu(v):
    return v * jax.nn.sigmoid(v)


def _mod_kernel(c_ref, w_ref, b_ref, o_ref):
    a = _silu(c_ref[...])
    o_ref[...] = jnp.dot(a, w_ref[...], preferred_element_type=F32,
                         precision=lax.Precision.HIGHEST) + b_ref[...]


def _mod(c, w_ada, b_ada):
    n, d = c.shape
    cols = w_ada.shape[1]
    tn = d
    assert cols % tn == 0
    return pl.pallas_call(
        _mod_kernel,
        grid=(cols // tn,),
        in_specs=[pl.BlockSpec((n, d), lambda j: (0, 0)),
                  pl.BlockSpec((d, tn), lambda j: (0, j)),
                  pl.BlockSpec((1, tn), lambda j: (0, j))],
        out_specs=pl.BlockSpec((n, tn), lambda j: (0, j)),
        out_shape=jax.ShapeDtypeStruct((n, cols), F32),
        compiler_params=_cparams(("arbitrary",)),
        name="mod",
    )(c, w_ada, b_ada.reshape(1, cols))


def _rope128(v, rc, ra, rb):
    half = ROT_DIM // 2
    return v * rc + pltpu.roll(v, LANES - half, 1) * ra + pltpu.roll(v, half, 1) * rb


def _modulated_norm(x_ref, sc_ref, sh_ref, g_ref):
    nbk, tt, d = x_ref.shape
    x = x_ref[...]
    ms = jnp.mean(x * x, axis=-1, keepdims=True)
    y = (x * lax.rsqrt(ms + EPS)) * g_ref[...].reshape(1, 1, d)
    h = y * (1.0 + sc_ref[...]) + sh_ref[...]
    return h.reshape(nbk * tt, d).astype(BF16)


def _inproj_common(hb, w_ref, rc, ra, rb, zs_ref, xbc_ref, q_ref, za_ref, sm_ref, consume_ssd_inputs=None):
    nbk, tt, _ = q_ref.shape

    def proj(lo, n):
        return jnp.dot(hb, w_ref[:, lo:lo + n], preferred_element_type=F32)

    zs = proj(_O_ZS, SSD_WIDTH)
    xbc = proj(_O_XBC, CONV_CH)
    sm = proj(_O_SM, LANES)
    if zs_ref is not None:
        zs_ref[...] = zs.reshape(nbk, tt, SSD_WIDTH)
    if xbc_ref is not None:
        xbc_ref[...] = xbc.reshape(nbk, tt, CONV_CH)
    if consume_ssd_inputs is not None:
        consume_ssd_inputs(zs, xbc, sm)
    za_ref[...] = proj(_O_ZA, ATT_WIDTH).reshape(nbk, tt, ATT_WIDTH)
    sm_ref[...] = sm.reshape(nbk, tt, LANES)
    qraw = proj(_O_Q, ATT_WIDTH)
    q = jnp.concatenate([_rope128(qraw[:, LANES * j:LANES * (j + 1)], rc, ra, rb)
                         for j in range(ATT_WIDTH // LANES)], axis=1)
    q_ref[...] = q.reshape(nbk, tt, ATT_WIDTH)
    return proj, zs, xbc, sm


def _inproj_sample_kernel(x_ref, sc_ref, sh_ref, g_ref, w_ref, rc_ref, ra_ref, rb_ref,
                          zs_ref, xbc_ref, q_ref, za_ref, sm_ref, kvc_ref, kvs_ref, kvw_ref):
    nbk, tt, _ = x_ref.shape
    hb = _modulated_norm(x_ref, sc_ref, sh_ref, g_ref)
    rc, ra, rb = rc_ref[...], ra_ref[...], rb_ref[...]
    proj = _inproj_common(hb, w_ref, rc, ra, rb, zs_ref, xbc_ref, q_ref, za_ref, sm_ref)[0]
    for off, ref in ((_O_KC, kvc_ref), (_O_KS, kvs_ref), (_O_KW, kvw_ref)):
        u = proj(off, 2 * KV_WIDTH)
        kv = jnp.concatenate([_rope128(u[:, :KV_WIDTH], rc, ra, rb), u[:, KV_WIDTH:]], axis=1)
        ref[...] = kv.reshape(nbk, tt, 2 * KV_WIDTH)


def _rope_rows(k, cos, sin):
    half = ROT_DIM // 2
    x1, x2 = k[:half], k[half:ROT_DIM]
    return jnp.concatenate([x1 * cos - x2 * sin, x2 * cos + x1 * sin, k[ROT_DIM:]], axis=0)


def _inproj_prompt_kernel(x_ref, sc_ref, sh_ref, g_ref, w_ref, wt_ref, rc_ref, ra_ref, rb_ref, cos_ref, sin_ref,
                          cw_ref, cb_ref, dtb_ref, dtbc_ref, alog_ref, alogc_ref, dsk_ref, ng_ref,
                          q_ref, za_ref, sm_ref, kcm_ref,
                          kvtc_ref, kvts_ref, kvtw_ref, kas_ref, vats_ref, kw_ref, vatw_ref,
                          y_ref, hout_ref, xlast_ref, xp_ref, h_ref, *, groups_per_seq):
    _, tt, _ = x_ref.shape
    hb = _modulated_norm(x_ref, sc_ref, sh_ref, g_ref)
    rc, ra, rb = rc_ref[...], ra_ref[...], rb_ref[...]
    w2 = 2 * KV_WIDTH

    def ssd_branch(zs, xbc, sm):
        c = pl.program_id(0) % groups_per_seq
        pad = xp_ref.shape[0] - tt
        xlast_ref[0] = xbc[tt - pad:]

        def init():
            xp_ref[0:pad, :] = jnp.zeros((pad, CONV_CH), F32)
            h_ref[...] = jnp.zeros(h_ref.shape, F32)

        smt = lax.dot_general(wt_ref[N_BRANCH * w2:, :], hb, (((1,), (1,)), ((), ())),
                              preferred_element_type=F32)
        _ssd_chunk(c == 0, c == groups_per_seq - 1, init, xbc, sm, smt, zs, cw_ref, cb_ref, dtb_ref, dtbc_ref,
                   alog_ref, alogc_ref, dsk_ref, ng_ref, y_ref, hout_ref, xp_ref, h_ref)

    proj = _inproj_common(hb, w_ref, rc, ra, rb, None, None, q_ref, za_ref, sm_ref, ssd_branch)[0]

    u = proj(_O_KC, 2 * KV_WIDTH)
    kv = jnp.concatenate([_rope128(u[:, :KV_WIDTH], rc, ra, rb), u[:, KV_WIDTH:]], axis=1)
    nblk = tt // CMP_BLOCK
    means = jnp.sum(kv.reshape(nblk, CMP_BLOCK, 2 * KV_WIDTH), axis=1) * (1.0 / CMP_BLOCK)
    kcm_ref[...] = means.reshape(1, nblk, 2 * KV_WIDTH)

    ti = pl.program_id(0) % groups_per_seq
    blk = (ti * tt + lax.broadcasted_iota(jnp.int32, (tt, MAX_SEL_LANES), 0)) // SEL_BLOCK
    onehot = jnp.where(blk == lax.broadcasted_iota(jnp.int32, (tt, MAX_SEL_LANES), 1), 1.0, 0.0)
    ks = _rope128(proj(_O_KS, KV_WIDTH), rc, ra, rb)
    kas_ref[0] = jnp.concatenate([onehot, ks], axis=1).astype(BF16)
    kw_ref[0] = _rope128(proj(_O_KW, KV_WIDTH), rc, ra, rb).astype(BF16)

    ut = lax.dot_general(wt_ref[:N_BRANCH * w2, :], hb, (((1,), (1,)), ((), ())), preferred_element_type=F32)
    cos, sin = cos_ref[...], sin_ref[...]
    slabs = []
    for br in range(N_BRANCH):
        s = ut[br * w2:(br + 1) * w2]
        ks = [_rope_rows(s[h * HEAD_DIM:(h + 1) * HEAD_DIM], cos, sin) for h in range(KV_HEADS)]
        slabs.append(jnp.concatenate(ks + [s[KV_WIDTH:]], axis=0))
    kvtc_ref[0] = slabs[0]
    kvts_ref[0] = slabs[1]
    kvtw_ref[0] = slabs[2]
    ones = jnp.ones((VAT_ROWS - HEAD_DIM, tt), F32)
    for slab, vat_ref in ((slabs[1], vats_ref), (slabs[2], vatw_ref)):
        for h in range(KV_HEADS):
            v = slab[KV_WIDTH + h * HEAD_DIM:KV_WIDTH + (h + 1) * HEAD_DIM]
            vat_ref[0, h] = jnp.concatenate([v, ones], axis=0).astype(BF16)


def _mod_spec(nbk, d, groups_per_mod):
    if groups_per_mod is None:
        return pl.BlockSpec((nbk, 1, d), lambda g: (g, 0, 0))
    return pl.BlockSpec((1, 1, d), lambda g: (g // groups_per_mod, 0, 0))


def _inproj_sample(x3, scale, shift, norm_g, w_big, rope_tabs, *, nbk):
    g_total, tt, d = x3.shape
    m = nbk * tt
    mod_spec = _mod_spec(nbk, d, None)
    tab_spec = pl.BlockSpec((m, LANES), lambda g: (0, 0))

    def tok_spec(c):
        return pl.BlockSpec((nbk, tt, c), lambda g: (g, 0, 0))

    widths = (SSD_WIDTH, CONV_CH, ATT_WIDTH, ATT_WIDTH, LANES, 2 * KV_WIDTH, 2 * KV_WIDTH, 2 * KV_WIDTH)
    return pl.pallas_call(
        _inproj_sample_kernel,
        grid=(g_total // nbk,),
        in_specs=[tok_spec(d), mod_spec, mod_spec,
                  pl.BlockSpec((1, d), lambda g: (0, 0)),
                  pl.BlockSpec((d, _W_COLS), lambda g: (0, 0)),
                  tab_spec, tab_spec, tab_spec],
        out_specs=[tok_spec(c) for c in widths],
        out_shape=[jax.ShapeDtypeStruct((g_total, tt, c), F32) for c in widths],
        compiler_params=_cparams(("arbitrary",)),
        name="inproj_sample",
    )(x3, scale, shift, norm_g.reshape(1, d), w_big, *rope_tabs)


def _inproj_prompt(x, scale, shift, norm_g, w_big, w_kvt, ssd_params, tt):
    b, t, d = x.shape
    assert tt == min(SSD_CHUNK, t)
    gps = t // tt
    pad = 8
    steps = b * gps
    x3 = x.reshape(steps, tt, d)
    pos = jnp.arange(t, dtype=jnp.int32)
    tabs = _rope_tables(pos)
    cos_t, sin_t = _rope_angles(pos)
    mod_spec = _mod_spec(1, d, gps)
    tab_spec = pl.BlockSpec((tt, LANES), lambda g: (g % gps, 0))
    ang_spec = pl.BlockSpec((ROT_DIM // 2, tt), lambda g: (0, g % gps))

    def tok_spec(c):
        return pl.BlockSpec((1, tt, c), lambda g: (g, 0, 0))

    def row_spec(r):
        return pl.BlockSpec((1, r, tt), lambda g: (g // gps, 0, g % gps))

    vat_spec = pl.BlockSpec((1, KV_HEADS, VAT_ROWS, tt), lambda g: (g // gps, 0, 0, g % gps))
    nblk = tt // CMP_BLOCK
    w2 = 2 * KV_WIDTH
    tok_widths = (ATT_WIDTH, ATT_WIDTH, LANES)
    state_spec = pl.BlockSpec((1, SSD_HEADS // 2, 2 * HEAD_DIM, D_STATE), lambda g: (g // gps, 0, 0, 0))
    out_specs = ([tok_spec(c) for c in tok_widths]
                 + [pl.BlockSpec((1, nblk, w2), lambda g: (g, 0, 0))]
                 + [row_spec(w2)] * 3
                 + [tok_spec(MAX_SEL_LANES + KV_WIDTH), vat_spec, tok_spec(KV_WIDTH), vat_spec]
                 + [tok_spec(SSD_WIDTH), state_spec, pl.BlockSpec((1, pad, CONV_CH), lambda g: (g // gps, 0, 0))])
    vat_shape = jax.ShapeDtypeStruct((b, KV_HEADS, VAT_ROWS, t), BF16)
    out_shape = ([jax.ShapeDtypeStruct((steps, tt, c), F32) for c in tok_widths]
                 + [jax.ShapeDtypeStruct((steps, nblk, w2), F32)]
                 + [jax.ShapeDtypeStruct((b, w2, t), F32)] * 3
                 + [jax.ShapeDtypeStruct((steps, tt, MAX_SEL_LANES + KV_WIDTH), BF16), vat_shape,
                    jax.ShapeDtypeStruct((steps, tt, KV_WIDTH), BF16), vat_shape]
                 + [jax.ShapeDtypeStruct((steps, tt, SSD_WIDTH), BF16),
                    jax.ShapeDtypeStruct((b, SSD_HEADS // 2, 2 * HEAD_DIM, D_STATE), F32),
                    jax.ShapeDtypeStruct((b, pad, CONV_CH), F32)])
    return pl.pallas_call(
        functools.partial(_inproj_prompt_kernel, groups_per_seq=gps),
        grid=(steps,),
        in_specs=[tok_spec(d), mod_spec, mod_spec,
                  pl.BlockSpec((1, d), lambda g: (0, 0)),
                  pl.BlockSpec((d, _W_COLS), lambda g: (0, 0)),
                  pl.BlockSpec(w_kvt.shape, lambda g: (0, 0)),
                  tab_spec, tab_spec, tab_spec, ang_spec, ang_spec]
                 + [pl.BlockSpec(p.shape, lambda g: (0, 0)) for p in ssd_params],
        out_specs=out_specs,
        out_shape=out_shape,
        scratch_shapes=[pltpu.VMEM((tt + pad, CONV_CH), F32),
                        pltpu.VMEM((SSD_HEADS // 2, 2 * HEAD_DIM, D_STATE), F32)],
        compiler_params=_cparams(("arbitrary",)),
        name="inproj_ssd_prompt",
    )(x3, scale, shift, norm_g.reshape(1, d), w_big, w_kvt, *tabs, cos_t, sin_t, *ssd_params)


def _rope_angles(pos):
    half = ROT_DIM // 2
    inv_freq = ROPE_THETA ** (-jnp.arange(half, dtype=F32) * 2.0 / ROT_DIM)
    ang = pos.astype(F32)[:, None] * inv_freq[None, :]
    return jnp.cos(ang).T, jnp.sin(ang).T


def _rope_tables(pos):
    half = ROT_DIM // 2
    inv_freq = ROPE_THETA ** (-jnp.arange(half, dtype=F32) * 2.0 / ROT_DIM)
    ang = pos.astype(F32)[:, None] * inv_freq[None, :]
    cos, sin = jnp.cos(ang), jnp.sin(ang)
    n = pos.shape[0]
    one = jnp.ones((n, HEAD_DIM - ROT_DIM), F32)
    zero_h = jnp.zeros((n, half), F32)
    zero_r = jnp.zeros((n, HEAD_DIM - ROT_DIM), F32)
    rc = jnp.concatenate([cos, cos, one], axis=1)
    ra = jnp.concatenate([-sin, zero_h, zero_r], axis=1)
    rb = jnp.concatenate([zero_h, sin, zero_r], axis=1)
    rep = LANES // HEAD_DIM
    return tuple(jnp.tile(t, (1, rep)) for t in (rc, ra, rb))


def _rearrange_w_in(w_in):
    parts, o = [], 0
    for n in COL_SIZES:
        parts.append(w_in[:, o:o + n])
        o += n
    z_s, xbc, dt, q, kc, ks, kw, g, z_a = parts
    d = w_in.shape[0]
    small = jnp.concatenate([dt, g, jnp.zeros((d, LANES - SSD_HEADS - N_BRANCH * ATT_HEADS), w_in.dtype)], axis=1)
    w_big = jnp.concatenate([z_s, xbc, q, kc, ks, kw, z_a, small], axis=1).astype(BF16)
    w_kvt = jnp.concatenate([kc, ks, kw, small], axis=1).T.astype(BF16)
    return w_big, w_kvt


def _pair_cols(mat, p, shape):
    lane = lax.broadcasted_iota(jnp.int32, shape, 1)
    a = jnp.broadcast_to(mat[:, 2 * p:2 * p + 1], shape)
    b = jnp.broadcast_to(mat[:, 2 * p + 1:2 * p + 2], shape)
    return jnp.where(lane < HEAD_DIM, a, b)


def _ssd_kernel(xbc_ref, sm_ref, smt_ref, zs_ref, conv0_ref, h0_ref, *rest):
    c = pl.program_id(1)
    xp_ref, h_ref = rest[-2:]
    pad = xp_ref.shape[0] - xbc_ref.shape[1]

    def init():
        xp_ref[0:pad, :] = conv0_ref[0]
        h_ref[...] = h0_ref[0]

    _ssd_chunk(c == 0, c == pl.num_programs(1) - 1, init, xbc_ref[0], sm_ref[0], smt_ref[0], zs_ref[0], *rest)


def _ssd_chunk(first, last, init, xbc, sm, smt, zs, cw_ref, cb_ref, dtb_ref, dtbc_ref,
               alog_ref, alogc_ref, dsk_ref, ng_ref, y_ref, hout_ref, xp_ref, h_ref):
    L = xbc.shape[0]
    pad = xp_ref.shape[0] - L
    hp = jnp.float32

    @pl.when(first)
    def _():
        init()

    xp_ref[pad:pad + L, :] = xbc
    cw = cw_ref[...]
    conv = cb_ref[...]
    for w in range(CONV_WIDTH):
        o = pad - (CONV_WIDTH - 1) + w
        conv = conv + xp_ref[o:o + L, :] * cw[w:w + 1, :]
    halo = xp_ref[L:L + pad, :]
    xp_ref[0:pad, :] = halo

    u = _silu(conv)
    xs = u[:, :SSD_WIDTH]
    gw = SSD_GROUPS * D_STATE
    bm = u[:, SSD_WIDTH:SSD_WIDTH + gw]
    cm = u[:, SSD_WIDTH + gw:]

    dt = jax.nn.softplus(sm + dtb_ref[...])
    dta = dt * (-jnp.exp(alog_ref[...]))
    row = lax.broadcasted_iota(jnp.int32, (L, L), 0)
    col = lax.broadcasted_iota(jnp.int32, (L, L), 1)
    causal = row >= col
    la = jnp.dot(causal.astype(hp), dta, preferred_element_type=hp,
                 precision=lax.Precision.HIGHEST)
    dtt = jax.nn.softplus(smt + dtbc_ref[...])
    dtat = dtt * (-jnp.exp(alogc_ref[...]))
    lat = jnp.dot(dtat, (row <= col).astype(hp), preferred_element_type=hp,
                  precision=lax.Precision.HIGHEST)
    la_last = la[L - 1:L, :]
    ela = jnp.exp(la)
    te = jnp.exp(la_last - la)
    cdec = jnp.exp(la_last)

    lane = lax.broadcasted_iota(jnp.int32, (L, LANES), 1)
    srow = lax.broadcasted_iota(jnp.int32, (LANES, LANES), 0)
    hpg = SSD_HEADS // SSD_GROUPS
    ys = []
    for g in range(SSD_GROUPS):
        bm_g = bm[:, g * D_STATE:(g + 1) * D_STATE]
        cm_g = cm[:, g * D_STATE:(g + 1) * D_STATE].astype(BF16)
        bm_gb = bm_g.astype(BF16)
        cb = lax.dot_general(cm_g, bm_gb, (((1,), (1,)), ((), ())), preferred_element_type=hp)
        for pp in range(hpg // 2):
            p = g * (hpg // 2) + pp
            xs_p = xs[:, LANES * p:LANES * (p + 1)]
            xdt = xs_p * _pair_cols(dt, p, (L, LANES))
            xdt_b = xdt.astype(BF16)
            yd = []
            for r in (2 * p, 2 * p + 1):
                seg = la[:, r:r + 1] - lat[r:r + 1, :]
                dec = jnp.where(causal, jnp.exp(jnp.where(causal, seg, 0.0)), 0.0)
                yd.append(jnp.dot((cb * dec).astype(BF16), xdt_b, preferred_element_type=hp))
            y_diag = jnp.where(lane < HEAD_DIM, yd[0], yd[1])
            h_p = h_ref[p]
            y_off = lax.dot_general(cm_g, h_p.astype(BF16), (((1,), (1,)), ((), ())),
                                    preferred_element_type=hp) * _pair_cols(ela, p, (L, LANES))
            ys.append(y_diag + y_off + dsk_ref[:, LANES * p:LANES * (p + 1)] * xs_p)
            xw = (xdt * _pair_cols(te, p, (L, LANES))).astype(BF16)
            st = lax.dot_general(xw, bm_gb, (((0,), (0,)), ((), ())), preferred_element_type=hp)
            cd = jnp.where(srow < HEAD_DIM,
                           jnp.broadcast_to(cdec[:, 2 * p:2 * p + 1], (LANES, LANES)),
                           jnp.broadcast_to(cdec[:, 2 * p + 1:2 * p + 2], (LANES, LANES)))
            h_ref[p] = h_p * cd + st

    y = jnp.concatenate(ys, axis=1)
    t = y * _silu(zs)
    ms = jnp.mean(t * t, axis=-1, keepdims=True)
    y_ref[0] = ((t * lax.rsqrt(ms + EPS)) * ng_ref[...]).astype(y_ref.dtype)

    del last
    hout_ref[0] = h_ref[...]


def _ssd(xbc, sm, zs, conv_state, h0, conv_w, conv_b, dt_bias, a_log, d_skip, norm_g):
    bn, t, _ = xbc.shape
    L = min(SSD_CHUNK, t)
    assert t % L == 0
    nc = t // L
    pad = 8
    smt = jnp.swapaxes(sm, 1, 2)
    conv0 = jnp.pad(conv_state, ((0, 0), (pad - (CONV_WIDTH - 1), 0), (0, 0)))
    hp2 = h0.reshape(bn, SSD_HEADS // 2, 2 * HEAD_DIM, D_STATE)
    params = _ssd_params(conv_w, conv_b, dt_bias, a_log, d_skip, norm_g)

    def full(shape):
        return pl.BlockSpec(shape, lambda b, c: tuple(0 for _ in shape))

    y, hout = pl.pallas_call(
        _ssd_kernel,
        grid=(bn, nc),
        in_specs=[pl.BlockSpec((1, L, CONV_CH), lambda b, c: (b, c, 0)),
                  pl.BlockSpec((1, L, LANES), lambda b, c: (b, c, 0)),
                  pl.BlockSpec((1, LANES, L), lambda b, c: (b, 0, c)),
                  pl.BlockSpec((1, L, SSD_WIDTH), lambda b, c: (b, c, 0)),
                  pl.BlockSpec((1, pad, CONV_CH), lambda b, c: (b, 0, 0)),
                  pl.BlockSpec((1, SSD_HEADS // 2, 2 * HEAD_DIM, D_STATE), lambda b, c: (b, 0, 0, 0)),
                  ] + [full(p.shape) for p in params],
        out_specs=[pl.BlockSpec((1, L, SSD_WIDTH), lambda b, c: (b, c, 0)),
                   pl.BlockSpec((1, SSD_HEADS // 2, 2 * HEAD_DIM, D_STATE), lambda b, c: (b, 0, 0, 0))],
        out_shape=[jax.ShapeDtypeStruct((bn, t, SSD_WIDTH), BF16),
                   jax.ShapeDtypeStruct((bn, SSD_HEADS // 2, 2 * HEAD_DIM, D_STATE), F32)],
        scratch_shapes=[pltpu.VMEM((L + pad, CONV_CH), F32),
                        pltpu.VMEM((SSD_HEADS // 2, 2 * HEAD_DIM, D_STATE), F32)],
        compiler_params=_cparams(("arbitrary", "arbitrary")),
        name="ssd",
    )(xbc, sm, smt, zs, conv0, hp2, *params)
    return y, hout.reshape(bn, SSD_HEADS, HEAD_DIM, D_STATE)


def _ssd_params(conv_w, conv_b, dt_bias, a_log, d_skip, norm_g):
    zpad = jnp.zeros((LANES - SSD_HEADS,), F32)
    dtb = jnp.concatenate([dt_bias.astype(F32), zpad])
    alog = jnp.concatenate([a_log.astype(F32), zpad])
    dsk = jnp.repeat(d_skip.astype(F32), HEAD_DIM).reshape(1, SSD_WIDTH)
    return (conv_w, conv_b.reshape(1, CONV_CH), dtb.reshape(1, LANES), dtb.reshape(LANES, 1),
            alog.reshape(1, LANES), alog.reshape(LANES, 1), dsk, norm_g.reshape(1, SSD_WIDTH))


def _half_mask(shape, kvh):
    lane = lax.broadcasted_iota(jnp.int32, shape, 1)
    return (lane >= HEAD_DIM) if kvh else (lane < HEAD_DIM)


def _stack_heads(q, kvh):
    tq = q.shape[0]
    keep = _half_mask((tq, LANES), kvh)
    blocks = []
    for g in range(GQA_GROUP):
        h = kvh * GQA_GROUP + g
        v = q[:, LANES * (h // 2):LANES * (h // 2 + 1)]
        if (h % 2) != kvh:
            v = pltpu.roll(v, HEAD_DIM, 1)
        blocks.append(jnp.where(keep, v, 0.0))
    return jnp.concatenate(blocks, axis=0)


def _unstack_heads(o_by_kvh, gates, branch, src_half_is_kvh=True):
    tq = gates.shape[0]
    lane = lax.broadcasted_iota(jnp.int32, (tq, LANES), 1)
    blocks = []
    for h in range(ATT_HEADS):
        kvh, g = divmod(h, GQA_GROUP)
        v = o_by_kvh[kvh][g * tq:(g + 1) * tq, :]
        gl = GATE_LANE0 + branch * ATT_HEADS + h
        v = v * gates[:, gl:gl + 1]
        src_half = kvh if src_half_is_kvh else 0
        if (h % 2) != src_half:
            v = pltpu.roll(v, HEAD_DIM, 1)
        blocks.append(v)
    outs = [jnp.where(lane < HEAD_DIM, blocks[2 * j], blocks[2 * j + 1]) for j in range(ATT_HEADS // 2)]
    return jnp.concatenate(outs, axis=1)


def _tile_rows(v, n):
    return jnp.concatenate([v] * n, axis=0)


def _flash_update(s, vaug, m_ref, acc_ref, kvh):
    m_prev = m_ref[kvh]
    m_new = jnp.maximum(m_prev, jnp.max(s, axis=1, keepdims=True))
    alpha = jnp.exp(m_prev - m_new)
    p = jnp.exp(s - m_new[:, 0:1])
    pv = jnp.dot(p.astype(BF16), vaug, preferred_element_type=F32)
    acc_ref[kvh] = acc_ref[kvh] * jnp.concatenate([alpha, alpha], axis=1) + pv
    m_ref[kvh] = m_new


def _flash_update_t(s, vaug_t, m_ref, acc_ref, kvh):
    m_prev = m_ref[kvh]
    m_new = jnp.maximum(m_prev, jnp.max(s, axis=1, keepdims=True))
    alpha = jnp.exp(m_prev - m_new)
    p = jnp.exp(s - m_new[:, 0:1])
    pv = lax.dot_general(p.astype(BF16), vaug_t, (((1,), (1,)), ((), ())), preferred_element_type=F32)
    acc_ref[kvh] = acc_ref[kvh] * jnp.concatenate([alpha, alpha], axis=1) + pv
    m_ref[kvh] = m_new


def _flash_out(acc_ref, kvh):
    acc = acc_ref[kvh]
    return acc[:, :LANES] / acc[:, LANES:]


def _vaug(v01):
    return jnp.concatenate([v01.astype(BF16), jnp.ones(v01.shape, BF16)], axis=1)


def _topk_cols(score_t, rounds):
    nb = score_t.shape[0]
    ridx = lax.broadcasted_iota(jnp.int32, score_t.shape, 0)
    sel = jnp.zeros(score_t.shape, F32)
    cur = score_t
    for _ in range(rounds):
        mx = jnp.max(cur, axis=0, keepdims=True)
        idx = jnp.min(jnp.where(cur == mx, ridx, nb), axis=0, keepdims=True)
        hit = ridx == idx
        sel = jnp.where(hit & (mx > -jnp.inf), 1.0, sel)
        cur = jnp.where(hit, -jnp.inf, cur)
    return sel


def _topk_rows(score, rounds):
    nb = score.shape[1]
    lidx = lax.broadcasted_iota(jnp.int32, score.shape, 1)
    ahead = jnp.zeros(score.shape, jnp.int32)
    for r in range(1, nb):
        other = pltpu.roll(score, r, 1)
        wins = (other > score) | ((other == score) & (lidx >= r))
        ahead = ahead + jnp.where(wins, 1, 0)
    return jnp.where((ahead < rounds) & (score > -jnp.inf), 1.0, 0.0)


def _cmp_attend(q, kc01, vc01, pos):
    tq = q.shape[0]
    ncl = kc01.shape[0]
    rows = GQA_GROUP * tq
    lane = lax.broadcasted_iota(jnp.int32, (rows, ncl), 1)
    cblk = 2 * (lane % MAX_SEL_LANES) + lane // MAX_SEL_LANES
    c_end = (cblk + 1) * CMP_BLOCK - 1
    mask = c_end <= _tile_rows(pos, GQA_GROUP)
    kcb = kc01.astype(BF16)
    vcb = vc01.astype(BF16)
    outs, imps = [], []
    for kvh in range(KV_HEADS):
        qs = _stack_heads(q, kvh).astype(BF16)
        s = lax.dot_general(qs, kcb, (((1,), (1,)), ((), ())), preferred_element_type=F32)
        s = jnp.where(mask, s, -jnp.inf)
        mx = jnp.max(s, axis=1, keepdims=True)
        mx = jnp.where(mx > -jnp.inf, mx, 0.0)
        e = jnp.exp(s - mx)
        dsum = jnp.sum(e, axis=1, keepdims=True)
        p = e / jnp.where(dsum > 0, dsum, 1.0)
        outs.append(jnp.dot(p.astype(BF16), vcb, preferred_element_type=F32))
        imp = p[0:tq]
        for g in range(1, GQA_GROUP):
            imp = imp + p[g * tq:(g + 1) * tq]
        imps.append(imp[:, :MAX_SEL_LANES] + imp[:, MAX_SEL_LANES:])
    return outs, imps


def _cmp_prompt_kernel(q_ref, kc_ref, vc_ref, sm_ref, o_ref, bias_ref):
    tq = q_ref.shape[1]
    i = pl.program_id(1)
    pos = i * tq + lax.broadcasted_iota(jnp.int32, (tq, 1), 0)
    q = q_ref[0] * SCALE
    outs, imps = _cmp_attend(q, kc_ref[0], vc_ref[0], pos)
    gates = jax.nn.sigmoid(sm_ref[0])
    o_ref[0] = _unstack_heads(outs, gates, 0)
    blk = lax.broadcasted_iota(jnp.int32, (tq, MAX_SEL_LANES), 1)
    cur = pos // SEL_BLOCK
    valid = blk <= cur
    forced = (blk == 0) | ((cur - blk >= 0) & (cur - blk < N_LOCAL_BLOCKS))
    for kvh in range(KV_HEADS):
        score = jnp.where(valid & jnp.logical_not(forced), imps[kvh], -jnp.inf)
        sel_t = _topk_cols(score.T, TOP_N - 1 - N_LOCAL_BLOCKS)
        keep_t = jnp.where(forced & valid, 1.0, 0.0).T
        bias_ref[0, kvh] = jnp.where((sel_t > 0.5) | (keep_t > 0.5), 0.0, NEG).astype(BF16)


def _cmp_prompt(q, kcm_perm, sm, tq):
    b, t, _ = q.shape
    ncl = kcm_perm.shape[1]
    return pl.pallas_call(
        _cmp_prompt_kernel,
        grid=(b, t // tq),
        in_specs=[pl.BlockSpec((1, tq, ATT_WIDTH), lambda bb, i: (bb, i, 0)),
                  pl.BlockSpec((1, ncl, LANES), lambda bb, i: (bb, 0, 0)),
                  pl.BlockSpec((1, ncl, LANES), lambda bb, i: (bb, 0, 1)),
                  pl.BlockSpec((1, tq, LANES), lambda bb, i: (bb, i, 0))],
        out_specs=[pl.BlockSpec((1, tq, ATT_WIDTH), lambda bb, i: (bb, i, 0)),
                   pl.BlockSpec((1, KV_HEADS, MAX_SEL_LANES, tq), lambda bb, i: (bb, 0, 0, i))],
        out_shape=[jax.ShapeDtypeStruct((b, t, ATT_WIDTH), F32),
                   jax.ShapeDtypeStruct((b, KV_HEADS, MAX_SEL_LANES, t), BF16)],
        compiler_params=_cparams(("arbitrary", "arbitrary")),
        name="cmp_prompt",
    )(q, kcm_perm, kcm_perm, sm)


def _build_qaug(q, bias_ref, qaug_ref):
    for kvh in range(KV_HEADS):
        qs = _stack_heads(q, kvh).astype(BF16)
        bias = _tile_rows(bias_ref[0, kvh], GQA_GROUP)
        qaug_ref[kvh] = jnp.concatenate([qs, bias], axis=1)


def _head_rows(qt, h):
    blk = qt[h * HEAD_DIM:(h + 1) * HEAD_DIM]
    z = jnp.zeros_like(blk)
    return jnp.concatenate([blk, z] if h // GQA_GROUP == 0 else [z, blk], axis=0)


def _emit_heads(o_by_head, sm_ref, branch, o_ref):
    gates_t = jax.nn.sigmoid(sm_ref[0]).T
    cols = []
    for j in range(ATT_HEADS // 2):
        pair = []
        for h in (2 * j, 2 * j + 1):
            gl = GATE_LANE0 + branch * ATT_HEADS + h
            pair.append(o_by_head[h] * gates_t[gl:gl + 1, :])
        cols.append(jnp.concatenate(pair, axis=0).T)
    o_ref[0] = jnp.concatenate(cols, axis=1)


def _pipeline_order(n, lead):
    order = [("qk", h) for h in range(min(lead, n))]
    for h in range(n):
        order.append(("sm", h))
        if h + lead < n:
            order.append(("qk", h + lead))
        if h >= 1:
            order.append(("pv", h - 1))
    order.append(("pv", n - 1))
    return order


_SLC_ORDER = _pipeline_order(ATT_HEADS, 8)
_SLC_TILES_PER_ITER = 2


def _slc_prompt_kernel(q_ref, bias_ref, ka_ref, vat_ref, sm_ref, o_ref, qa_ref, m_ref, acc_ref):
    tq = q_ref.shape[1]
    tk = tq
    i = pl.program_id(1)
    qt = (q_ref[0] * SCALE_LOG2).T
    for h in range(ATT_HEADS):
        qa_ref[h] = jnp.concatenate([bias_ref[0, h // GQA_GROUP], _head_rows(qt, h).astype(BF16)], axis=0)
    m_ref[...] = jnp.full(m_ref.shape, NEG, F32)
    acc_ref[...] = jnp.zeros(acc_ref.shape, F32)

    def tiles(j0, n, masked):
        k0s = [pl.multiple_of((j0 + t) * tk, tk) for t in range(n)]
        kas = [ka_ref[0, pl.ds(k0, tk), :] for k0 in k0s]
        ss, ps, alphas = {}, {}, {}

        def qk(h):
            ss[h] = []
            for ka in kas:
                s = jnp.dot(ka, qa_ref[h], preferred_element_type=F32)
                if masked:
                    kofs = lax.broadcasted_iota(jnp.int32, (tk, tq), 0)
                    qofs = lax.broadcasted_iota(jnp.int32, (tk, tq), 1)
                    s = jnp.where(kofs <= qofs, s, NEG)
                ss[h].append(s)

        def sm(h):
            m_prev = m_ref[h]
            m_new = m_prev
            for s in ss[h]:
                m_new = jnp.maximum(m_new, jnp.max(s, axis=0, keepdims=True))
            alphas[h] = jnp.exp2(m_prev - m_new)
            ps[h] = [jnp.exp2(s - m_new).astype(BF16) for s in ss[h]]
            m_ref[h] = m_new

        def pv(h):
            acc = acc_ref[h] * alphas[h]
            for k0, p in zip(k0s, ps[h]):
                acc = acc + jnp.dot(vat_ref[0, h // GQA_GROUP, :, pl.ds(k0, tk)], p, preferred_element_type=F32)
            acc_ref[h] = acc

        for step in _SLC_ORDER:
            {"qk": qk, "sm": sm, "pv": pv}[step[0]](step[1])

    def body(jq, carry):
        tiles(_SLC_TILES_PER_ITER * jq, _SLC_TILES_PER_ITER, False)
        return carry

    lax.fori_loop(0, i // _SLC_TILES_PER_ITER, body, 0)
    done = (i // _SLC_TILES_PER_ITER) * _SLC_TILES_PER_ITER
    n = _SLC_TILES_PER_ITER // 2
    while n >= 1:
        take = ((i - done) // n) % 2 == 1

        @pl.when(take)
        def _(done=done, n=n):
            tiles(done, n, False)

        done = done + jnp.where(take, n, 0)
        n //= 2

    tiles(i, 1, True)
    outs = []
    for h in range(ATT_HEADS):
        acc = acc_ref[h]
        outs.append(acc[:HEAD_DIM] / acc[HEAD_DIM:HEAD_DIM + 1])
    _emit_heads(outs, sm_ref, 1, o_ref)


def _slc_prompt(q, bias_t, ka, vat, sm, tq):
    b, t, _ = q.shape
    kc = ka.shape[2]
    return pl.pallas_call(
        _slc_prompt_kernel,
        grid=(b, t // tq),
        in_specs=[pl.BlockSpec((1, tq, ATT_WIDTH), lambda bb, i: (bb, i, 0)),
                  pl.BlockSpec((1, KV_HEADS, MAX_SEL_LANES, tq), lambda bb, i: (bb, 0, 0, i)),
                  pl.BlockSpec((1, t, kc), lambda bb, i: (bb, 0, 0)),
                  pl.BlockSpec((1, KV_HEADS, VAT_ROWS, t), lambda bb, i: (bb, 0, 0, 0)),
                  pl.BlockSpec((1, tq, LANES), lambda bb, i: (bb, i, 0))],
        out_specs=pl.BlockSpec((1, tq, ATT_WIDTH), lambda bb, i: (bb, i, 0)),
        out_shape=jax.ShapeDtypeStruct((b, t, ATT_WIDTH), F32),
        scratch_shapes=[pltpu.VMEM((ATT_HEADS, kc, tq), BF16),
                        pltpu.VMEM((ATT_HEADS, 1, tq), F32),
                        pltpu.VMEM((ATT_HEADS, VAT_ROWS, tq), F32)],
        compiler_params=_cparams(("arbitrary", "arbitrary")),
        name="slc_prompt",
    )(q, bias_t, ka, vat, sm)


def _win_prompt_kernel(q_ref, sm_ref, *refs, n_prev):
    nt = n_prev + 1
    k_refs, vat_refs, o_ref = refs[:nt], refs[nt:2 * nt], refs[2 * nt]
    tq = q_ref.shape[1]
    i = pl.program_id(1)
    kofs = lax.broadcasted_iota(jnp.int32, (tq, tq), 0)
    qofs = lax.broadcasted_iota(jnp.int32, (tq, tq), 1)
    qt = (q_ref[0] * SCALE_LOG2).T
    scores = []
    for h in range(ATT_HEADS):
        qh = _head_rows(qt, h).astype(BF16)
        ss = []
        for n in range(nt):
            s = jnp.dot(k_refs[n][0], qh, preferred_element_type=F32)
            back = n_prev - n
            ok = i >= back
            if n == 0:
                ok = ok & (kofs >= qofs)
            if back == 0:
                ok = kofs <= qofs
            ss.append(jnp.where(ok, s, NEG))
        scores.append(ss)
    probs = []
    for ss in scores:
        mx = ss[0].max(axis=0, keepdims=True)
        for s in ss[1:]:
            mx = jnp.maximum(mx, s.max(axis=0, keepdims=True))
        probs.append([jnp.exp2(s - mx).astype(BF16) for s in ss])
    outs = []
    for h in range(ATT_HEADS):
        acc = None
        for n in range(nt):
            pv = jnp.dot(vat_refs[n][0, h // GQA_GROUP], probs[h][n], preferred_element_type=F32)
            acc = pv if acc is None else acc + pv
        outs.append(acc[:HEAD_DIM] / acc[HEAD_DIM:HEAD_DIM + 1])
    _emit_heads(outs, sm_ref, 2, o_ref)


def _win_prompt(q, kw, vat, sm, tq):
    b, t, _ = q.shape
    assert WINDOW % tq == 0 and WINDOW >= tq
    n_prev = WINDOW // tq

    def k_spec(back):
        return pl.BlockSpec((1, tq, KV_WIDTH), lambda bb, i: (bb, jnp.maximum(i - back, 0), 0))

    def v_spec(back):
        return pl.BlockSpec((1, KV_HEADS, VAT_ROWS, tq), lambda bb, i: (bb, 0, 0, jnp.maximum(i - back, 0)))

    backs = [n_prev - n for n in range(n_prev + 1)]
    nt = n_prev + 1
    return pl.pallas_call(
        functools.partial(_win_prompt_kernel, n_prev=n_prev),
        grid=(b, t // tq),
        in_specs=([pl.BlockSpec((1, tq, ATT_WIDTH), lambda bb, i: (bb, i, 0)),
                   pl.BlockSpec((1, tq, LANES), lambda bb, i: (bb, i, 0))]
                  + [k_spec(bk) for bk in backs] + [v_spec(bk) for bk in backs]),
        out_specs=pl.BlockSpec((1, tq, ATT_WIDTH), lambda bb, i: (bb, i, 0)),
        out_shape=jax.ShapeDtypeStruct((b, t, ATT_WIDTH), F32),
        compiler_params=_cparams(("arbitrary", "arbitrary")),
        name="win_prompt",
    )(q, sm, *([kw] * nt), *([vat] * nt))


CMP_PAGES = LANES * CMP_BLOCK // PAGE_SIZE


def _cmp_local_block(lane):
    half = LANES // 2
    return 2 * (lane % half) + lane // half


def _block_mean_matrix(n_pages):
    tok = jnp.arange(n_pages * PAGE_SIZE, dtype=jnp.int32)[:, None] // CMP_BLOCK
    col = jnp.arange(n_pages * PAGE_SIZE // CMP_BLOCK, dtype=jnp.int32)[None, :]
    blk = LANES * (col // LANES) + _cmp_local_block(col % LANES)
    return jnp.where(tok == blk, 1.0 / CMP_BLOCK, 0.0).astype(BF16)


def _cmp_sample_kernel(pt_ref, *refs, n_pg, past):
    page_refs = refs[:n_pg]
    a_ref, q_ref, sm_ref, o_ref, bias_ref, kct_ref = refs[n_pg:]
    s = pl.program_id(1)
    ns = pl.num_programs(1)
    n_chunks = kct_ref.shape[0]
    tn = q_ref.shape[1]
    rows = GQA_GROUP * tn

    x = jnp.concatenate([r[0] for r in page_refs], axis=1)
    hi = x.astype(BF16)
    lo = (x - hi.astype(F32)).astype(BF16)
    a = a_ref[...]
    means = jnp.dot(hi, a, preferred_element_type=F32) + jnp.dot(lo, a, preferred_element_type=F32)
    for c in range(n_chunks):
        kct_ref[c] = means[:, c * LANES:(c + 1) * LANES]

    @pl.when(s == ns - 1)
    def _():
        pos = past + lax.broadcasted_iota(jnp.int32, (tn, 1), 0)
        lane = lax.broadcasted_iota(jnp.int32, (rows, LANES), 1)
        q = q_ref[0] * SCALE
        outs, imps = [], []
        for kvh in range(KV_HEADS):
            qs = _stack_heads(q, kvh).astype(BF16)
            ss = []
            for c in range(n_chunks):
                sc = jnp.dot(qs, kct_ref[c, :KV_WIDTH, :].astype(BF16), preferred_element_type=F32)
                c_end = (c * LANES + _cmp_local_block(lane) + 1) * CMP_BLOCK - 1
                ss.append(jnp.where(c_end <= _tile_rows(pos, GQA_GROUP), sc, -jnp.inf))
            mx = ss[0].max(axis=1, keepdims=True)
            for sc in ss[1:]:
                mx = jnp.maximum(mx, sc.max(axis=1, keepdims=True))
            mx = jnp.where(mx > -jnp.inf, mx, 0.0)
            es = [jnp.exp(sc - mx) for sc in ss]
            dsum = es[0].sum(axis=1, keepdims=True)
            for e in es[1:]:
                dsum = dsum + e.sum(axis=1, keepdims=True)
            inv = 1.0 / jnp.where(dsum > 0, dsum, 1.0)
            o = None
            imp_blocks = []
            for c in range(n_chunks):
                p = es[c] * inv
                vc = kct_ref[c, KV_WIDTH:, :].astype(BF16)
                pv = lax.dot_general(p.astype(BF16), vc, (((1,), (1,)), ((), ())), preferred_element_type=F32)
                o = pv if o is None else o + pv
                imp = p[0:tn]
                for g in range(1, GQA_GROUP):
                    imp = imp + p[g * tn:(g + 1) * tn]
                imp_blocks.append(imp + pltpu.roll(imp, LANES // 2, 1))
            outs.append(o)
            if n_chunks == 1:
                imps.append(imp_blocks[0])
            else:
                lane_t = lax.broadcasted_iota(jnp.int32, (tn, LANES), 1)
                imps.append(jnp.where(lane_t < LANES // 2, imp_blocks[0], pltpu.roll(imp_blocks[1], LANES // 2, 1)))
        o_ref[0] = _unstack_heads(outs, jax.nn.sigmoid(sm_ref[0]), 0)
        n_past = past // SEL_BLOCK
        blk = lax.broadcasted_iota(jnp.int32, (tn, MAX_SEL_LANES), 1)
        forced = (blk == 0) | (blk == n_past - 1)
        rounds = min(TOP_N, n_past + 1) - 1
        for kvh in range(KV_HEADS):
            score = jnp.where(forced, jnp.inf, imps[kvh])
            score = jnp.where(blk < n_past, score, -jnp.inf)
            sel = _topk_rows(score, rounds)
            bias_ref[0, kvh] = jnp.where(sel > 0.5, 0.0, NEG).astype(BF16)


def _page_specs(n_pg):
    def spec(k):
        return pl.BlockSpec((1, 2 * KV_WIDTH, PAGE_SIZE), lambda b, s, pt: (pt[b * n_pg + k], 0, 0))
    return [spec(k) for k in range(n_pg)]


def _cmp_sample(page_table, pool_t, q, sm):
    bs, tn, _ = q.shape
    n_pages = page_table.shape[1]
    past = n_pages * PAGE_SIZE
    n_pg = n_pages
    assert n_pages % CMP_PAGES == 0 and n_pages // CMP_PAGES <= 2
    n_chunks = n_pages // CMP_PAGES
    grid_spec = pltpu.PrefetchScalarGridSpec(
        num_scalar_prefetch=1,
        grid=(bs, 1),
        in_specs=_page_specs(n_pg) + [
            pl.BlockSpec((n_pg * PAGE_SIZE, n_chunks * LANES), lambda b, s, pt: (0, 0)),
            pl.BlockSpec((1, tn, ATT_WIDTH), lambda b, s, pt: (b, 0, 0)),
            pl.BlockSpec((1, tn, LANES), lambda b, s, pt: (b, 0, 0))],
        out_specs=[pl.BlockSpec((1, tn, ATT_WIDTH), lambda b, s, pt: (b, 0, 0)),
                   pl.BlockSpec((1, KV_HEADS, tn, LANES), lambda b, s, pt: (b, 0, 0, 0))],
        scratch_shapes=[pltpu.VMEM((n_chunks, 2 * KV_WIDTH, LANES), F32)],
    )
    return pl.pallas_call(
        functools.partial(_cmp_sample_kernel, n_pg=n_pg, past=past),
        grid_spec=grid_spec,
        out_shape=[jax.ShapeDtypeStruct((bs, tn, ATT_WIDTH), F32),
                   jax.ShapeDtypeStruct((bs, KV_HEADS, tn, LANES), BF16)],
        compiler_params=_cparams(("arbitrary", "arbitrary")),
        name="cmp_sample",
    )(page_table.reshape(-1), *([pool_t] * n_pg), _block_mean_matrix(n_pages), q, sm)


def _slc_sample_kernel(pt_ref, *refs, n_pg, n_seq, past):
    page_refs = [refs[u * n_pg:(u + 1) * n_pg] for u in range(n_seq)]
    q_ref, bias_ref, new_ref, sm_ref, o_ref, qaug_ref, m_ref, acc_ref = refs[n_seq * n_pg:]
    s = pl.program_id(1)
    ns = pl.num_programs(1)
    tn = q_ref.shape[1]
    rows = GQA_GROUP * tn

    nq = KV_HEADS * rows

    @pl.when(s == 0)
    def _():
        for u in range(n_seq):
            q = q_ref[u] * SCALE
            blocks = [jnp.concatenate([_stack_heads(q, kvh).astype(BF16),
                                       _tile_rows(bias_ref[u, kvh], GQA_GROUP)], axis=1)
                      for kvh in range(KV_HEADS)]
            qaug_ref[u] = jnp.concatenate(blocks + [jnp.zeros((LANES - nq, 2 * LANES), BF16)], axis=0)
        m_ref[...] = jnp.full(m_ref.shape, NEG, F32)
        acc_ref[...] = jnp.zeros(acc_ref.shape, F32)

    def update(sts, vaugs):
        m_news, alphas, ps = [], [], []
        for u in range(n_seq):
            m_prev = m_ref[u]
            m_new = m_prev
            for st in sts[u]:
                m_new = jnp.maximum(m_new, jnp.max(st, axis=0, keepdims=True))
            m_news.append(m_new)
            alphas.append(jnp.exp(m_prev - m_new))
            ps.append([jnp.exp(st - m_new).astype(BF16) for st in sts[u]])
        for u in range(n_seq):
            acc = acc_ref[u] * alphas[u]
            for p, va in zip(ps[u], vaugs[u]):
                acc = acc + jnp.dot(va, p, preferred_element_type=F32)
            acc_ref[u] = acc
            m_ref[u] = m_news[u]

    tk = n_pg * PAGE_SIZE
    kblk = (s * tk + lax.broadcasted_iota(jnp.int32, (MAX_SEL_LANES, tk), 1)) // SEL_BLOCK
    onehot = jnp.where(kblk == lax.broadcasted_iota(jnp.int32, (MAX_SEL_LANES, tk), 0), 1.0, 0.0).astype(BF16)
    n_piece = 2
    w = tk // n_piece
    sts, vaugs = [], []
    for u in range(n_seq):
        x = jnp.concatenate([r[0] for r in page_refs[u]], axis=1)
        kaug_t = jnp.concatenate([x[:KV_WIDTH].astype(BF16), onehot], axis=0)
        vaug_t = jnp.concatenate([x[KV_WIDTH:].astype(BF16), jnp.ones((BF16_SUBLANES, tk), BF16)], axis=0)
        qa = qaug_ref[u]
        sts.append([lax.dot_general(kaug_t[:, i * w:(i + 1) * w], qa, (((0,), (1,)), ((), ())),
                                    preferred_element_type=F32) for i in range(n_piece)])
        vaugs.append([vaug_t[:, i * w:(i + 1) * w] for i in range(n_piece)])
    update(sts, vaugs)

    @pl.when(s == ns - 1)
    def _():
        nk = new_ref.shape[1]
        kidx = lax.broadcasted_iota(jnp.int32, (nk, LANES), 0)
        qidx = lax.broadcasted_iota(jnp.int32, (nk, LANES), 1) % tn
        sts, vaugs = [], []
        for u in range(n_seq):
            xn = new_ref[u]
            st = lax.dot_general(xn[:, :LANES].astype(BF16), qaug_ref[u, :, :LANES], (((1,), (1,)), ((), ())),
                                 preferred_element_type=F32)
            sts.append([jnp.where(kidx <= qidx, st, NEG)])
            vaugs.append([jnp.concatenate([xn[:, LANES:].T.astype(BF16), jnp.ones((BF16_SUBLANES, nk), BF16)],
                                          axis=0)])
        update(sts, vaugs)
        for u in range(n_seq):
            acc = acc_ref[u]
            o_t = (acc[:KV_WIDTH] / acc[KV_WIDTH:KV_WIDTH + 1]).T
            outs = [o_t[kvh * rows:(kvh + 1) * rows] for kvh in range(KV_HEADS)]
            o_ref[u] = _unstack_heads(outs, jax.nn.sigmoid(sm_ref[u]), 1)


def _slc_sample(page_table, pool, q, bias, new_pad, sm, n_pg):
    bs, tn, _ = q.shape
    n_pages = page_table.shape[1]
    past = n_pages * PAGE_SIZE
    rows = GQA_GROUP * tn
    npad = new_pad.shape[1]
    n_seq = 4
    assert bs % n_seq == 0 and KV_HEADS * rows <= LANES

    def page_spec(u, k):
        return pl.BlockSpec((1, 2 * KV_WIDTH, PAGE_SIZE),
                            lambda b, s, pt: (pt[(n_seq * b + u) * n_pages + s * n_pg + k], 0, 0))

    grid_spec = pltpu.PrefetchScalarGridSpec(
        num_scalar_prefetch=1,
        grid=(bs // n_seq, n_pages // n_pg),
        in_specs=[page_spec(u, k) for u in range(n_seq) for k in range(n_pg)] + [
            pl.BlockSpec((n_seq, tn, ATT_WIDTH), lambda b, s, pt: (b, 0, 0)),
            pl.BlockSpec((n_seq, KV_HEADS, tn, LANES), lambda b, s, pt: (b, 0, 0, 0)),
            pl.BlockSpec((n_seq, npad, 2 * KV_WIDTH), lambda b, s, pt: (b, 0, 0)),
            pl.BlockSpec((n_seq, tn, LANES), lambda b, s, pt: (b, 0, 0))],
        out_specs=pl.BlockSpec((n_seq, tn, ATT_WIDTH), lambda b, s, pt: (b, 0, 0)),
        scratch_shapes=[pltpu.VMEM((n_seq, LANES, 2 * LANES), BF16),
                        pltpu.VMEM((n_seq, 1, LANES), F32),
                        pltpu.VMEM((n_seq, KV_WIDTH + BF16_SUBLANES, LANES), F32)],
    )
    return pl.pallas_call(
        functools.partial(_slc_sample_kernel, n_pg=n_pg, n_seq=n_seq, past=past),
        grid_spec=grid_spec,
        out_shape=jax.ShapeDtypeStruct((bs, tn, ATT_WIDTH), F32),
        compiler_params=_cparams(("arbitrary", "arbitrary")),
        name="slc_sample",
    )(page_table.reshape(-1), *([pool] * (n_seq * n_pg)), q, bias, new_pad, sm)


def _win_sample_kernel(q_ref, wint_ref, new_ref, sm_ref, o_ref, *, past):
    nb, tn = q_ref.shape[0], q_ref.shape[1]
    rows = GQA_GROUP * tn
    wb = wint_ref.shape[2]
    npad = new_ref.shape[1]
    kidx = lax.broadcasted_iota(jnp.int32, (rows, wb), 1)
    t_w = lax.broadcasted_iota(jnp.int32, (rows, wb), 0) % tn
    ok_w = (wb + t_w - kidx <= WINDOW) & (past - wb + kidx >= 0)
    ok_n = (lax.broadcasted_iota(jnp.int32, (rows, npad), 1)
            <= lax.broadcasted_iota(jnp.int32, (rows, npad), 0) % tn)
    chains = [(sq, kvh) for sq in range(nb) for kvh in range(KV_HEADS)]
    scores, probs = {}, {}
    for sq, kvh in chains:
        qs = _stack_heads(q_ref[sq] * SCALE, kvh).astype(BF16)
        kt = wint_ref[sq, :KV_WIDTH, :].astype(BF16)
        kn = new_ref[sq, :, :KV_WIDTH].astype(BF16)
        s_w = jnp.where(ok_w, jnp.dot(qs, kt, preferred_element_type=F32), -jnp.inf)
        s_n = jnp.where(ok_n, lax.dot_general(qs, kn, (((1,), (1,)), ((), ())), preferred_element_type=F32),
                        -jnp.inf)
        scores[sq, kvh] = (s_w, s_n)
    for key in chains:
        s_w, s_n = scores[key]
        mx = jnp.maximum(jnp.max(s_w, axis=1, keepdims=True), jnp.max(s_n, axis=1, keepdims=True))
        e_w, e_n = jnp.exp(s_w - mx), jnp.exp(s_n - mx)
        inv = 1.0 / (jnp.sum(e_w, axis=1, keepdims=True) + jnp.sum(e_n, axis=1, keepdims=True))
        probs[key] = ((e_w * inv).astype(BF16), (e_n * inv).astype(BF16))
    for sq in range(nb):
        vt = wint_ref[sq, KV_WIDTH:, :].astype(BF16)
        vn = new_ref[sq, :, KV_WIDTH:].astype(BF16)
        outs = []
        for kvh in range(KV_HEADS):
            p_w, p_n = probs[sq, kvh]
            outs.append(lax.dot_general(p_w, vt, (((1,), (1,)), ((), ())), preferred_element_type=F32)
                        + jnp.dot(p_n, vn, preferred_element_type=F32))
        o_ref[sq] = _unstack_heads(outs, jax.nn.sigmoid(sm_ref[sq]), 2)


def _win_sample(q, win_t, new_pad, sm, past, nb):
    bs, tn, _ = q.shape
    wb = win_t.shape[2]
    npad = new_pad.shape[1]
    assert bs % nb == 0 and tn <= WINDOW
    return pl.pallas_call(
        functools.partial(_win_sample_kernel, past=past),
        grid=(bs // nb,),
        in_specs=[pl.BlockSpec((nb, tn, ATT_WIDTH), lambda b: (b, 0, 0)),
                  pl.BlockSpec((nb, 2 * KV_WIDTH, wb), lambda b: (b, 0, 0)),
                  pl.BlockSpec((nb, npad, 2 * KV_WIDTH), lambda b: (b, 0, 0)),
                  pl.BlockSpec((nb, tn, LANES), lambda b: (b, 0, 0))],
        out_specs=pl.BlockSpec((nb, tn, ATT_WIDTH), lambda b: (b, 0, 0)),
        out_shape=jax.ShapeDtypeStruct((bs, tn, ATT_WIDTH), F32),
        compiler_params=_cparams(("arbitrary",)),
        name="win_sample",
    )(q, win_t, new_pad, sm)


def _out_kernel(x_ref, oc_ref, os_ref, ow_ref, za_ref, ys_ref, gate_ref, ang_ref, wo_ref, fg_ref, y_ref):
    nbk, tt, d = x_ref.shape
    m = nbk * tt
    o = (oc_ref[...] + os_ref[...]) + ow_ref[...]
    t = o * _silu(za_ref[...])
    ms = jnp.mean(t * t, axis=-1, keepdims=True)
    y_att = (t * lax.rsqrt(ms + EPS)) * ang_ref[...].reshape(1, 1, ATT_WIDTH)
    ya = y_att.reshape(m, ATT_WIDTH).astype(BF16)
    ys = ys_ref[...].reshape(m, SSD_WIDTH)
    mix = (jnp.dot(ys, wo_ref[:SSD_WIDTH, :], preferred_element_type=F32)
           + jnp.dot(ya, wo_ref[SSD_WIDTH:, :], preferred_element_type=F32))
    xp = x_ref[...] + gate_ref[...] * mix.reshape(nbk, tt, d)
    ms2 = jnp.mean(xp * xp, axis=-1, keepdims=True)
    y_ref[...] = (xp * lax.rsqrt(ms2 + EPS)) * fg_ref[...].reshape(1, 1, d)


def _out(x3, o_c, o_s, o_w, za, y_ssd, gate, att_norm_g, w_out_b, final_g, *, nbk, groups_per_mod):
    g_total, tt, d = x3.shape
    steps = g_total // nbk
    if groups_per_mod is None:
        mod_spec = pl.BlockSpec((nbk, 1, d), lambda g: (g, 0, 0))
    else:
        mod_spec = pl.BlockSpec((1, 1, d), lambda g: (g // groups_per_mod, 0, 0))

    def tok_spec(c):
        return pl.BlockSpec((nbk, tt, c), lambda g: (g, 0, 0))

    return pl.pallas_call(
        _out_kernel,
        grid=(steps,),
        in_specs=[tok_spec(d), tok_spec(ATT_WIDTH), tok_spec(ATT_WIDTH), tok_spec(ATT_WIDTH),
                  tok_spec(ATT_WIDTH), tok_spec(SSD_WIDTH), mod_spec,
                  pl.BlockSpec((1, ATT_WIDTH), lambda g: (0, 0)),
                  pl.BlockSpec((SSD_WIDTH + ATT_WIDTH, d), lambda g: (0, 0)),
                  pl.BlockSpec((1, d), lambda g: (0, 0))],
        out_specs=tok_spec(d),
        out_shape=jax.ShapeDtypeStruct((g_total, tt, d), F32),
        compiler_params=_cparams(("arbitrary",)),
        name="outproj",
    )(x3, o_c, o_s, o_w, za, y_ssd, gate, att_norm_g.reshape(1, ATT_WIDTH), w_out_b, final_g.reshape(1, d))


def _perm_cmp_means(kcm, b):
    nc = kcm.shape[1]
    ns = nc // 2
    assert ns <= MAX_SEL_LANES
    eo = kcm.reshape(b, ns, 2, 2 * KV_WIDTH).transpose(0, 2, 1, 3)
    eo = jnp.pad(eo, ((0, 0), (0, 0), (0, MAX_SEL_LANES - ns), (0, 0)))
    return eo.reshape(b, 2 * MAX_SEL_LANES, 2 * KV_WIDTH)


def _prompt_layer(x, mod, lw, final_g, apply_final):
    b, t, d = x.shape
    shift, scale, gate = (mod[:, None, i * d:(i + 1) * d] for i in range(3))
    tt = min(SSD_CHUNK, t)
    assert t % tt == 0 and tt % CMP_BLOCK == 0 and t >= WINDOW
    gpb = t // tt
    x3 = x.reshape(b * gpb, tt, d)
    ssd_params = _ssd_params(lw["conv_w"], lw["conv_b"], lw["dt_bias"], lw["a_log"], lw["d_skip"],
                             lw["ssd_norm_g"])
    q, za, sm, kcm, kvt_c, kvt_s, kvt_w, ka_s, vat_s, kw, vat_w, y_ssd, ssm_new, xlast = _inproj_prompt(
        x, scale, shift, lw["norm_g"], lw["w_big"], lw["w_kvt"], ssd_params, tt)
    r = lambda a: a.reshape(b, t, a.shape[-1])
    q, za, sm, ka_s, kw, y_ssd = map(r, (q, za, sm, ka_s, kw, y_ssd))
    kcm = kcm.reshape(b, t // CMP_BLOCK, 2 * KV_WIDTH)
    ssm_new = ssm_new.reshape(b, SSD_HEADS, HEAD_DIM, D_STATE)

    o_c, bias = _cmp_prompt(q, _perm_cmp_means(kcm, b), sm, 128)
    tq = 256
    o_s = _slc_prompt(q, bias, ka_s, vat_s, sm, tq)
    o_w = _win_prompt(q, kw, vat_w, sm, tq)

    g3 = lambda a: a.reshape(b * gpb, tt, a.shape[-1])
    y3 = _out(x3, g3(o_c), g3(o_s), g3(o_w), g3(za), g3(y_ssd), gate, lw["att_norm_g"], lw["w_out_b"],
              final_g, nbk=1, groups_per_mod=gpb)
    assert apply_final
    kv6 = lambda a: a.reshape(b, 2, KV_HEADS, HEAD_DIM, a.shape[-1]).transpose(0, 4, 1, 2, 3)
    conv_new = xlast[:, xlast.shape[1] - (CONV_WIDTH - 1):]
    outs = (kv6(kvt_c), kv6(kvt_s), kv6(kvt_w[:, :, t - min(WINDOW, t):]), conv_new, ssm_new)
    return y3.reshape(b, t, d), outs


def _sample_layer(x, mod, lw, final_g, pool_c, pool_s, win_buf, conv_buf, ssm, page_table, apply_final):
    bs, tn, d = x.shape
    shift, scale, gate = (mod[:, None, i * d:(i + 1) * d] for i in range(3))
    n_pages = page_table.shape[1]
    past = n_pages * PAGE_SIZE
    nbk = 16
    n_pg = 16
    assert bs % nbk == 0 and tn % 8 == 0 and tn <= SEL_BLOCK and n_pages % n_pg == 0
    assert past // SEL_BLOCK <= MAX_SEL_LANES and past % SEL_BLOCK == 0
    pos = past + jnp.arange(tn, dtype=jnp.int32)
    tabs = tuple(jnp.tile(tb, (nbk, 1)) for tb in _rope_tables(pos))
    zs, xbc, q, za, sm, kvc, kvs, kvw = _inproj_sample(
        x, scale, shift, lw["norm_g"], lw["w_big"], tabs, nbk=nbk)

    y_ssd, ssm_new = _ssd(xbc, sm, zs, conv_buf, ssm, lw["conv_w"], lw["conv_b"], lw["dt_bias"], lw["a_log"],
                          lw["d_skip"], lw["ssd_norm_g"])

    npad = LANES
    to_rows = lambda p: p.transpose(0, 2, 3, 4, 1).reshape(p.shape[0], 2 * KV_WIDTH, p.shape[1])
    o_c, bias = _cmp_sample(page_table, to_rows(pool_c), q, sm)
    kvs_pad = jnp.pad(kvs, ((0, 0), (0, npad - tn), (0, 0)))
    o_s = _slc_sample(page_table, to_rows(pool_s), q, bias, kvs_pad, sm, n_pg)
    wb = win_buf.shape[1]
    win_t = to_rows(win_buf)
    kvw_pad = jnp.pad(kvw, ((0, 0), (0, npad - tn), (0, 0)))
    o_w = _win_sample(q, win_t, kvw_pad, sm, past, 8)

    y = _out(x, o_c, o_s, o_w, za, y_ssd, gate, lw["att_norm_g"], lw["w_out_b"], final_g,
             nbk=nbk, groups_per_mod=None)
    assert apply_final
    kv6 = lambda a: a.reshape(bs, a.shape[1], 2, KV_HEADS, HEAD_DIM)
    kv_w_all = jnp.concatenate([win_t, jnp.swapaxes(kvw, 1, 2)], axis=2)
    win_new_t = kv_w_all[:, :, kv_w_all.shape[2] - min(WINDOW, past + tn):]
    win_new = win_new_t.reshape(bs, 2, KV_HEADS, HEAD_DIM, win_new_t.shape[2]).transpose(0, 4, 1, 2, 3)
    conv_new = jnp.concatenate([conv_buf, xbc], axis=1)[:, tn:]
    outs = (kv6(kvc), kv6(kvs), win_new, conv_new, ssm_new)
    return y, outs


def kernel(x_prompt, x_sample, cache_cmp_kv, cache_slc_kv, state_win_kv, state_conv, state_ssm, page_table,
           c_prompt, c_sample, w_ada, b_ada, norm_g, w_in, conv_w, conv_b, dt_bias, a_log, d_skip,
           ssd_norm_g, att_norm_g, w_out, final_g):
    depth = w_ada.shape[0]
    assert depth == 1
    n_prompt = c_prompt.shape[0]
    xp, xs = x_prompt, x_sample
    out_p, out_s = [], []
    for l in range(depth):
        w_big, w_kvt = _rearrange_w_in(w_in[l])
        lw = dict(norm_g=norm_g[l], w_big=w_big, w_kvt=w_kvt, conv_w=conv_w[l], conv_b=conv_b[l],
                  dt_bias=dt_bias[l], a_log=a_log[l], d_skip=d_skip[l], ssd_norm_g=ssd_norm_g[l],
                  att_norm_g=att_norm_g[l], w_out_b=w_out[l].astype(BF16))
        mod = _mod(jnp.concatenate([c_prompt, c_sample], axis=0), w_ada[l], b_ada[l])
        last = l == depth - 1
        xp, op = _prompt_layer(xp, mod[:n_prompt], lw, final_g, last)
        xs, os_ = _sample_layer(xs, mod[n_prompt:], lw, final_g, cache_cmp_kv[l], cache_slc_kv[l],
                                state_win_kv[l], state_conv[l], state_ssm[l], page_table, last)
        out_p.append(op)
        out_s.append(os_)
    sp = [jnp.stack([o[k] for o in out_p]) for k in range(5)]
    sd = [jnp.stack([o[k] for o in out_s]) for k in range(5)]
    return (xp, xs, sp[0], sp[1], sp[2], sp[3], sp[4], sd[0], sd[1], sd[2], sd[3], sd[4])
```

```python
import functools

import jax
import jax.numpy as jnp
from jax import lax
from jax.experimental import pallas as pl
from jax.experimental.pallas import tpu as pltpu

F32 = jnp.float32
BF16 = jnp.bfloat16

HEAD_DIM = 64
SSD_HEADS = 8
SSD_WIDTH = SSD_HEADS * HEAD_DIM
SSD_GROUPS = 2
D_STATE = 128
CONV_WIDTH = 4
CONV_CH = SSD_WIDTH + 2 * SSD_GROUPS * D_STATE
SSD_CHUNK = 256
ATT_HEADS = 8
ATT_WIDTH = ATT_HEADS * HEAD_DIM
KV_HEADS = 2
GQA_GROUP = ATT_HEADS // KV_HEADS
KV_WIDTH = KV_HEADS * HEAD_DIM
CMP_BLOCK = 32
SEL_BLOCK = 64
TOP_N = 16
N_LOCAL_BLOCKS = 2
WINDOW = 512
N_BRANCH = 3
ROT_DIM = HEAD_DIM // 4
ROPE_THETA = 500000.0
PAGE_SIZE = 128
EPS = 1e-6
COL_SIZES = (SSD_WIDTH, CONV_CH, SSD_HEADS, ATT_WIDTH, 2 * KV_WIDTH, 2 * KV_WIDTH, 2 * KV_WIDTH,
             N_BRANCH * ATT_HEADS, ATT_WIDTH)

LANES = 128
MAX_SEL_LANES = LANES
NEG = -1e30
SCALE = HEAD_DIM ** -0.5
SCALE_LOG2 = SCALE * 1.4426950408889634
GATE_LANE0 = SSD_HEADS
BF16_SUBLANES = 16
VAT_ROWS = HEAD_DIM + BF16_SUBLANES
VMEM_LIMIT = 56 * 1024 * 1024

_O_ZS = 0
_O_XBC = _O_ZS + SSD_WIDTH
_O_Q = _O_XBC + CONV_CH
_O_KC = _O_Q + ATT_WIDTH
_O_KS = _O_KC + 2 * KV_WIDTH
_O_KW = _O_KS + 2 * KV_WIDTH
_O_ZA = _O_KW + 2 * KV_WIDTH
_O_SM = _O_ZA + ATT_WIDTH
_W_COLS = _O_SM + LANES


def _cparams(sem):
    return pltpu.CompilerParams(dimension_semantics=sem, vmem_limit_bytes=VMEM_LIMIT)


def _silu(v):
    return v * jax.nn.sigmoid(v)


def _mod_kernel(c_ref, w_ref, b_ref, o_ref):
    a = _silu(c_ref[...])
    o_ref[...] = jnp.dot(a, w_ref[...], preferred_element_type=F32,
                         precision=lax.Precision.HIGHEST) + b_ref[...]


def _mod(c, w_ada, b_ada):
    n, d = c.shape
    cols = w_ada.shape[1]
    tn = d
    assert cols % tn == 0
    return pl.pallas_call(
        _mod_kernel,
        grid=(cols // tn,),
        in_specs=[pl.BlockSpec((n, d), lambda j: (0, 0)),
                  pl.BlockSpec((d, tn), lambda j: (0, j)),
                  pl.BlockSpec((1, tn), lambda j: (0, j))],
        out_specs=pl.BlockSpec((n, tn), lambda j: (0, j)),
        out_shape=jax.ShapeDtypeStruct((n, cols), F32),
        compiler_params=_cparams(("arbitrary",)),
        name="mod",
    )(c, w_ada, b_ada.reshape(1, cols))


def _rope128(v, rc, ra, rb):
    half = ROT_DIM // 2
    return v * rc + pltpu.roll(v, LANES - half, 1) * ra + pltpu.roll(v, half, 1) * rb


def _modulated_norm(x_ref, sc_ref, sh_ref, g_ref):
    nbk, tt, d = x_ref.shape
    x = x_ref[...]
    ms = jnp.mean(x * x, axis=-1, keepdims=True)
    y = (x * lax.rsqrt(ms + EPS)) * g_ref[...].reshape(1, 1, d)
    h = y * (1.0 + sc_ref[...]) + sh_ref[...]
    return h.reshape(nbk * tt, d).astype(BF16)


def _inproj_common(hb, w_ref, rc, ra, rb, zs_ref, xbc_ref, q_ref, za_ref, sm_ref, consume_ssd_inputs=None):
    nbk, tt, _ = q_ref.shape

    def proj(lo, n):
        return jnp.dot(hb, w_ref[:, lo:lo + n], preferred_element_type=F32)

    zs = proj(_O_ZS, SSD_WIDTH)
    xbc = proj(_O_XBC, CONV_CH)
    sm = proj(_O_SM, LANES)
    if zs_ref is not None:
        zs_ref[...] = zs.reshape(nbk, tt, SSD_WIDTH)
    if xbc_ref is not None:
        xbc_ref[...] = xbc.reshape(nbk, tt, CONV_CH)
    if consume_ssd_inputs is not None:
        consume_ssd_inputs(zs, xbc, sm)
    za_ref[...] = proj(_O_ZA, ATT_WIDTH).reshape(nbk, tt, ATT_WIDTH)
    sm_ref[...] = sm.reshape(nbk, tt, LANES)
    qraw = proj(_O_Q, ATT_WIDTH)
    q = jnp.concatenate([_rope128(qraw[:, LANES * j:LANES * (j + 1)], rc, ra, rb)
                         for j in range(ATT_WIDTH // LANES)], axis=1)
    q_ref[...] = q.reshape(nbk, tt, ATT_WIDTH)
    return proj, zs, xbc, sm


def _inproj_sample_kernel(x_ref, sc_ref, sh_ref, g_ref, w_ref, rc_ref, ra_ref, rb_ref,
                          zs_ref, xbc_ref, q_ref, za_ref, sm_ref, kvc_ref, kvs_ref, kvw_ref):
    nbk, tt, _ = x_ref.shape
    hb = _modulated_norm(x_ref, sc_ref, sh_ref, g_ref)
    rc, ra, rb = rc_ref[...], ra_ref[...], rb_ref[...]
    proj = _inproj_common(hb, w_ref, rc, ra, rb, zs_ref, xbc_ref, q_ref, za_ref, sm_ref)[0]
    for off, ref in ((_O_KC, kvc_ref), (_O_KS, kvs_ref), (_O_KW, kvw_ref)):
        u = proj(off, 2 * KV_WIDTH)
        kv = jnp.concatenate([_rope128(u[:, :KV_WIDTH], rc, ra, rb), u[:, KV_WIDTH:]], axis=1)
        ref[...] = kv.reshape(nbk, tt, 2 * KV_WIDTH)


def _rope_rows(k, cos, sin):
    half = ROT_DIM // 2
    x1, x2 = k[:half], k[half:ROT_DIM]
    return jnp.concatenate([x1 * cos - x2 * sin, x2 * cos + x1 * sin, k[ROT_DIM:]], axis=0)


def _inproj_prompt_kernel(x_ref, sc_ref, sh_ref, g_ref, w_ref, wt_ref, rc_ref, ra_ref, rb_ref, cos_ref, sin_ref,
                          cw_ref, cb_ref, dtb_ref, dtbc_ref, alog_ref, alogc_ref, dsk_ref, ng_ref,
                          q_ref, za_ref, sm_ref, kcm_ref,
                          kvtc_ref, kvts_ref, kvtw_ref, kas_ref, vats_ref, kw_ref, vatw_ref,
                          y_ref, hout_ref, xlast_ref, xp_ref, h_ref, *, groups_per_seq):
    _, tt, _ = x_ref.shape
    hb = _modulated_norm(x_ref, sc_ref, sh_ref, g_ref)
    rc, ra, rb = rc_ref[...], ra_ref[...], rb_ref[...]
    w2 = 2 * KV_WIDTH

    def ssd_branch(zs, xbc, sm):
        c = pl.program_id(0) % groups_per_seq
        pad = xp_ref.shape[0] - tt
        xlast_ref[0] = xbc[tt - pad:]

        def init():
            xp_ref[0:pad, :] = jnp.zeros((pad, CONV_CH), F32)
            h_ref[...] = jnp.zeros(h_ref.shape, F32)

        smt = lax.dot_general(wt_ref[N_BRANCH * w2:, :], hb, (((1,), (1,)), ((), ())),
                              preferred_element_type=F32)
        _ssd_chunk(c == 0, c == groups_per_seq - 1, init, xbc, sm, smt, zs, cw_ref, cb_ref, dtb_ref, dtbc_ref,
                   alog_ref, alogc_ref, dsk_ref, ng_ref, y_ref, hout_ref, xp_ref, h_ref)

    proj = _inproj_common(hb, w_ref, rc, ra, rb, None, None, q_ref, za_ref, sm_ref, ssd_branch)[0]

    u = proj(_O_KC, 2 * KV_WIDTH)
    kv = jnp.concatenate([_rope128(u[:, :KV_WIDTH], rc, ra, rb), u[:, KV_WIDTH:]], axis=1)
    nblk = tt // CMP_BLOCK
    means = jnp.sum(kv.reshape(nblk, CMP_BLOCK, 2 * KV_WIDTH), axis=1) * (1.0 / CMP_BLOCK)
    kcm_ref[...] = means.reshape(1, nblk, 2 * KV_WIDTH)

    ti = pl.program_id(0) % groups_per_seq
    blk = (ti * tt + lax.broadcasted_iota(jnp.int32, (tt, MAX_SEL_LANES), 0)) // SEL_BLOCK
    onehot = jnp.where(blk == lax.broadcasted_iota(jnp.int32, (tt, MAX_SEL_LANES), 1), 1.0, 0.0)
    ks = _rope128(proj(_O_KS, KV_WIDTH), rc, ra, rb)
    kas_ref[0] = jnp.concatenate([onehot, ks], axis=1).astype(BF16)
    kw_ref[0] = _rope128(proj(_O_KW, KV_WIDTH), rc, ra, rb).astype(BF16)

    ut = lax.dot_general(wt_ref[:N_BRANCH * w2, :], hb, (((1,), (1,)), ((), ())), preferred_element_type=F32)
    cos, sin = cos_ref[...], sin_ref[...]
    slabs = []
    for br in range(N_BRANCH):
        s = ut[br * w2:(br + 1) * w2]
        ks = [_rope_rows(s[h * HEAD_DIM:(h + 1) * HEAD_DIM], cos, sin) for h in range(KV_HEADS)]
        slabs.append(jnp.concatenate(ks + [s[KV_WIDTH:]], axis=0))
    kvtc_ref[0] = slabs[0]
    kvts_ref[0] = slabs[1]
    kvtw_ref[0] = slabs[2]
    ones = jnp.ones((VAT_ROWS - HEAD_DIM, tt), F32)
    for slab, vat_ref in ((slabs[1], vats_ref), (slabs[2], vatw_ref)):
        for h in range(KV_HEADS):
            v = slab[KV_WIDTH + h * HEAD_DIM:KV_WIDTH + (h + 1) * HEAD_DIM]
            vat_ref[0, h] = jnp.concatenate([v, ones], axis=0).astype(BF16)


def _mod_spec(nbk, d, groups_per_mod):
    if groups_per_mod is None:
        return pl.BlockSpec((nbk, 1, d), lambda g: (g, 0, 0))
    return pl.BlockSpec((1, 1, d), lambda g: (g // groups_per_mod, 0, 0))


def _inproj_sample(x3, scale, shift, norm_g, w_big, rope_tabs, *, nbk):
    g_total, tt, d = x3.shape
    m = nbk * tt
    mod_spec = _mod_spec(nbk, d, None)
    tab_spec = pl.BlockSpec((m, LANES), lambda g: (0, 0))

    def tok_spec(c):
        return pl.BlockSpec((nbk, tt, c), lambda g: (g, 0, 0))

    widths = (SSD_WIDTH, CONV_CH, ATT_WIDTH, ATT_WIDTH, LANES, 2 * KV_WIDTH, 2 * KV_WIDTH, 2 * KV_WIDTH)
    return pl.pallas_call(
        _inproj_sample_kernel,
        grid=(g_total // nbk,),
        in_specs=[tok_spec(d), mod_spec, mod_spec,
                  pl.BlockSpec((1, d), lambda g: (0, 0)),
                  pl.BlockSpec((d, _W_COLS), lambda g: (0, 0)),
                  tab_spec, tab_spec, tab_spec],
        out_specs=[tok_spec(c) for c in widths],
        out_shape=[jax.ShapeDtypeStruct((g_total, tt, c), F32) for c in widths],
        compiler_params=_cparams(("arbitrary",)),
        name="inproj_sample",
    )(x3, scale, shift, norm_g.reshape(1, d), w_big, *rope_tabs)


def _inproj_prompt(x, scale, shift, norm_g, w_big, w_kvt, ssd_params, tt):
    b, t, d = x.shape
    assert tt == min(SSD_CHUNK, t)
    gps = t // tt
    pad = 8
    steps = b * gps
    x3 = x.reshape(steps, tt, d)
    pos = jnp.arange(t, dtype=jnp.int32)
    tabs = _rope_tables(pos)
    cos_t, sin_t = _rope_angles(pos)
    mod_spec = _mod_spec(1, d, gps)
    tab_spec = pl.BlockSpec((tt, LANES), lambda g: (g % gps, 0))
    ang_spec = pl.BlockSpec((ROT_DIM // 2, tt), lambda g: (0, g % gps))

    def tok_spec(c):
        return pl.BlockSpec((1, tt, c), lambda g: (g, 0, 0))

    def row_spec(r):
        return pl.BlockSpec((1, r, tt), lambda g: (g // gps, 0, g % gps))

    vat_spec = pl.BlockSpec((1, KV_HEADS, VAT_ROWS, tt), lambda g: (g // gps, 0, 0, g % gps))
    nblk = tt // CMP_BLOCK
    w2 = 2 * KV_WIDTH
    tok_widths = (ATT_WIDTH, ATT_WIDTH, LANES)
    state_spec = pl.BlockSpec((1, SSD_HEADS // 2, 2 * HEAD_DIM, D_STATE), lambda g: (g // gps, 0, 0, 0))
    out_specs = ([tok_spec(c) for c in tok_widths]
                 + [pl.BlockSpec((1, nblk, w2), lambda g: (g, 0, 0))]
                 + [row_spec(w2)] * 3
                 + [tok_spec(MAX_SEL_LANES + KV_WIDTH), vat_spec, tok_spec(KV_WIDTH), vat_spec]
                 + [tok_spec(SSD_WIDTH), state_spec, pl.BlockSpec((1, pad, CONV_CH), lambda g: (g // gps, 0, 0))])
    vat_shape = jax.ShapeDtypeStruct((b, KV_HEADS, VAT_ROWS, t), BF16)
    out_shape = ([jax.ShapeDtypeStruct((steps, tt, c), F32) for c in tok_widths]
                 + [jax.ShapeDtypeStruct((steps, nblk, w2), F32)]
                 + [jax.ShapeDtypeStruct((b, w2, t), F32)] * 3
                 + [jax.ShapeDtypeStruct((steps, tt, MAX_SEL_LANES + KV_WIDTH), BF16), vat_shape,
                    jax.ShapeDtypeStruct((steps, tt, KV_WIDTH), BF16), vat_shape]
                 + [jax.ShapeDtypeStruct((steps, tt, SSD_WIDTH), BF16),
                    jax.ShapeDtypeStruct((b, SSD_HEADS // 2, 2 * HEAD_DIM, D_STATE), F32),
                    jax.ShapeDtypeStruct((b, pad, CONV_CH), F32)])
    return pl.pallas_call(
        functools.partial(_inproj_prompt_kernel, groups_per_seq=gps),
        grid=(steps,),
        in_specs=[tok_spec(d), mod_spec, mod_spec,
                  pl.BlockSpec((1, d), lambda g: (0, 0)),
                  pl.BlockSpec((d, _W_COLS), lambda g: (0, 0)),
                  pl.BlockSpec(w_kvt.shape, lambda g: (0, 0)),
                  tab_spec, tab_spec, tab_spec, ang_spec, ang_spec]
                 + [pl.BlockSpec(p.shape, lambda g: (0, 0)) for p in ssd_params],
        out_specs=out_specs,
        out_shape=out_shape,
        scratch_shapes=[pltpu.VMEM((tt + pad, CONV_CH), F32),
                        pltpu.VMEM((SSD_HEADS // 2, 2 * HEAD_DIM, D_STATE), F32)],
        compiler_params=_cparams(("arbitrary",)),
        name="inproj_ssd_prompt",
    )(x3, scale, shift, norm_g.reshape(1, d), w_big, w_kvt, *tabs, cos_t, sin_t, *ssd_params)


def _rope_angles(pos):
    half = ROT_DIM // 2
    inv_freq = ROPE_THETA ** (-jnp.arange(half, dtype=F32) * 2.0 / ROT_DIM)
    ang = pos.astype(F32)[:, None] * inv_freq[None, :]
    return jnp.cos(ang).T, jnp.sin(ang).T


def _rope_tables(pos):
    half = ROT_DIM // 2
    inv_freq = ROPE_THETA ** (-jnp.arange(half, dtype=F32) * 2.0 / ROT_DIM)
    ang = pos.astype(F32)[:, None] * inv_freq[None, :]
    cos, sin = jnp.cos(ang), jnp.sin(ang)
    n = pos.shape[0]
    one = jnp.ones((n, HEAD_DIM - ROT_DIM), F32)
    zero_h = jnp.zeros((n, half), F32)
    zero_r = jnp.zeros((n, HEAD_DIM - ROT_DIM), F32)
    rc = jnp.concatenate([cos, cos, one], axis=1)
    ra = jnp.concatenate([-sin, zero_h, zero_r], axis=1)
    rb = jnp.concatenate([zero_h, sin, zero_r], axis=1)
    rep = LANES // HEAD_DIM
    return tuple(jnp.tile(t, (1, rep)) for t in (rc, ra, rb))


def _rearrange_w_in(w_in):
    parts, o = [], 0
    for n in COL_SIZES:
        parts.append(w_in[:, o:o + n])
        o += n
    z_s, xbc, dt, q, kc, ks, kw, g, z_a = parts
    d = w_in.shape[0]
    small = jnp.concatenate([dt, g, jnp.zeros((d, LANES - SSD_HEADS - N_BRANCH * ATT_HEADS), w_in.dtype)], axis=1)
    w_big = jnp.concatenate([z_s, xbc, q, kc, ks, kw, z_a, small], axis=1).astype(BF16)
    w_kvt = jnp.concatenate([kc, ks, kw, small], axis=1).T.astype(BF16)
    return w_big, w_kvt


def _pair_cols(mat, p, shape):
    lane = lax.broadcasted_iota(jnp.int32, shape, 1)
    a = jnp.broadcast_to(mat[:, 2 * p:2 * p + 1], shape)
    b = jnp.broadcast_to(mat[:, 2 * p + 1:2 * p + 2], shape)
    return jnp.where(lane < HEAD_DIM, a, b)


def _ssd_kernel(xbc_ref, sm_ref, smt_ref, zs_ref, conv0_ref, h0_ref, *rest):
    c = pl.program_id(1)
    xp_ref, h_ref = rest[-2:]
    pad = xp_ref.shape[0] - xbc_ref.shape[1]

    def init():
        xp_ref[0:pad, :] = conv0_ref[0]
        h_ref[...] = h0_ref[0]

    _ssd_chunk(c == 0, c == pl.num_programs(1) - 1, init, xbc_ref[0], sm_ref[0], smt_ref[0], zs_ref[0], *rest)


def _ssd_chunk(first, last, init, xbc, sm, smt, zs, cw_ref, cb_ref, dtb_ref, dtbc_ref,
               alog_ref, alogc_ref, dsk_ref, ng_ref, y_ref, hout_ref, xp_ref, h_ref):
    L = xbc.shape[0]
    pad = xp_ref.shape[0] - L
    hp = jnp.float32

    @pl.when(first)
    def _():
        init()

    xp_ref[pad:pad + L, :] = xbc
    cw = cw_ref[...]
    conv = cb_ref[...]
    for w in range(CONV_WIDTH):
        o = pad - (CONV_WIDTH - 1) + w
        conv = conv + xp_ref[o:o + L, :] * cw[w:w + 1, :]
    halo = xp_ref[L:L + pad, :]
    xp_ref[0:pad, :] = halo

    u = _silu(conv)
    xs = u[:, :SSD_WIDTH]
    gw = SSD_GROUPS * D_STATE
    bm = u[:, SSD_WIDTH:SSD_WIDTH + gw]
    cm = u[:, SSD_WIDTH + gw:]

    dt = jax.nn.softplus(sm + dtb_ref[...])
    dta = dt * (-jnp.exp(alog_ref[...]))
    row = lax.broadcasted_iota(jnp.int32, (L, L), 0)
    col = lax.broadcasted_iota(jnp.int32, (L, L), 1)
    causal = row >= col
    la = jnp.dot(causal.astype(hp), dta, preferred_element_type=hp,
                 precision=lax.Precision.HIGHEST)
    dtt = jax.nn.softplus(smt + dtbc_ref[...])
    dtat = dtt * (-jnp.exp(alogc_ref[...]))
    lat = jnp.dot(dtat, (row <= col).astype(hp), preferred_element_type=hp,
                  precision=lax.Precision.HIGHEST)
    la_last = la[L - 1:L, :]
    ela = jnp.exp(la)
    te = jnp.exp(la_last - la)
    cdec = jnp.exp(la_last)

    lane = lax.broadcasted_iota(jnp.int32, (L, LANES), 1)
    srow = lax.broadcasted_iota(jnp.int32, (LANES, LANES), 0)
    hpg = SSD_HEADS // SSD_GROUPS
    ys = []
    for g in range(SSD_GROUPS):
        bm_g = bm[:, g * D_STATE:(g + 1) * D_STATE]
        cm_g = cm[:, g * D_STATE:(g + 1) * D_STATE].astype(BF16)
        bm_gb = bm_g.astype(BF16)
        cb = lax.dot_general(cm_g, bm_gb, (((1,), (1,)), ((), ())), preferred_element_type=hp)
        for pp in range(hpg // 2):
            p = g * (hpg // 2) + pp
            xs_p = xs[:, LANES * p:LANES * (p + 1)]
            xdt = xs_p * _pair_cols(dt, p, (L, LANES))
            xdt_b = xdt.astype(BF16)
            yd = []
            for r in (2 * p, 2 * p + 1):
                seg = la[:, r:r + 1] - lat[r:r + 1, :]
                dec = jnp.where(causal, jnp.exp(jnp.where(causal, seg, 0.0)), 0.0)
                yd.append(jnp.dot((cb * dec).astype(BF16), xdt_b, preferred_element_type=hp))
            y_diag = jnp.where(lane < HEAD_DIM, yd[0], yd[1])
            h_p = h_ref[p]
            y_off = lax.dot_general(cm_g, h_p.astype(BF16), (((1,), (1,)), ((), ())),
                                    preferred_element_type=hp) * _pair_cols(ela, p, (L, LANES))
            ys.append(y_diag + y_off + dsk_ref[:, LANES * p:LANES * (p + 1)] * xs_p)
            xw = (xdt * _pair_cols(te, p, (L, LANES))).astype(BF16)
            st = lax.dot_general(xw, bm_gb, (((0,), (0,)), ((), ())), preferred_element_type=hp)
            cd = jnp.where(srow < HEAD_DIM,
                           jnp.broadcast_to(cdec[:, 2 * p:2 * p + 1], (LANES, LANES)),
                           jnp.broadcast_to(cdec[:, 2 * p + 1:2 * p + 2], (LANES, LANES)))
            h_ref[p] = h_p * cd + st

    y = jnp.concatenate(ys, axis=1)
    t = y * _silu(zs)
    ms = jnp.mean(t * t, axis=-1, keepdims=True)
    y_ref[0] = ((t * lax.rsqrt(ms + EPS)) * ng_ref[...]).astype(y_ref.dtype)

    del last
    hout_ref[0] = h_ref[...]


def _ssd(xbc, sm, zs, conv_state, h0, conv_w, conv_b, dt_bias, a_log, d_skip, norm_g):
    bn, t, _ = xbc.shape
    L = min(SSD_CHUNK, t)
    assert t % L == 0
    nc = t // L
    pad = 8
    smt = jnp.swapaxes(sm, 1, 2)
    conv0 = jnp.pad(conv_state, ((0, 0), (pad - (CONV_WIDTH - 1), 0), (0, 0)))
    hp2 = h0.reshape(bn, SSD_HEADS // 2, 2 * HEAD_DIM, D_STATE)
    params = _ssd_params(conv_w, conv_b, dt_bias, a_log, d_skip, norm_g)

    def full(shape):
        return pl.BlockSpec(shape, lambda b, c: tuple(0 for _ in shape))

    y, hout = pl.pallas_call(
        _ssd_kernel,
        grid=(bn, nc),
        in_specs=[pl.BlockSpec((1, L, CONV_CH), lambda b, c: (b, c, 0)),
                  pl.BlockSpec((1, L, LANES), lambda b, c: (b, c, 0)),
                  pl.BlockSpec((1, LANES, L), lambda b, c: (b, 0, c)),
                  pl.BlockSpec((1, L, SSD_WIDTH), lambda b, c: (b, c, 0)),
                  pl.BlockSpec((1, pad, CONV_CH), lambda b, c: (b, 0, 0)),
                  pl.BlockSpec((1, SSD_HEADS // 2, 2 * HEAD_DIM, D_STATE), lambda b, c: (b, 0, 0, 0)),
                  ] + [full(p.shape) for p in params],
        out_specs=[pl.BlockSpec((1, L, SSD_WIDTH), lambda b, c: (b, c, 0)),
                   pl.BlockSpec((1, SSD_HEADS // 2, 2 * HEAD_DIM, D_STATE), lambda b, c: (b, 0, 0, 0))],
        out_shape=[jax.ShapeDtypeStruct((bn, t, SSD_WIDTH), BF16),
                   jax.ShapeDtypeStruct((bn, SSD_HEADS // 2, 2 * HEAD_DIM, D_STATE), F32)],
        scratch_shapes=[pltpu.VMEM((L + pad, CONV_CH), F32),
                        pltpu.VMEM((SSD_HEADS // 2, 2 * HEAD_DIM, D_STATE), F32)],
        compiler_params=_cparams(("arbitrary", "arbitrary")),
        name="ssd",
    )(xbc, sm, smt, zs, conv0, hp2, *params)
    return y, hout.reshape(bn, SSD_HEADS, HEAD_DIM, D_STATE)


def _ssd_params(conv_w, conv_b, dt_bias, a_log, d_skip, norm_g):
    zpad = jnp.zeros((LANES - SSD_HEADS,), F32)
    dtb = jnp.concatenate([dt_bias.astype(F32), zpad])
    alog = jnp.concatenate([a_log.astype(F32), zpad])
    dsk = jnp.repeat(d_skip.astype(F32), HEAD_DIM).reshape(1, SSD_WIDTH)
    return (conv_w, conv_b.reshape(1, CONV_CH), dtb.reshape(1, LANES), dtb.reshape(LANES, 1),
            alog.reshape(1, LANES), alog.reshape(LANES, 1), dsk, norm_g.reshape(1, SSD_WIDTH))


def _half_mask(shape, kvh):
    lane = lax.broadcasted_iota(jnp.int32, shape, 1)
    return (lane >= HEAD_DIM) if kvh else (lane < HEAD_DIM)


def _stack_heads(q, kvh):
    tq = q.shape[0]
    keep = _half_mask((tq, LANES), kvh)
    blocks = []
    for g in range(GQA_GROUP):
        h = kvh * GQA_GROUP + g
        v = q[:, LANES * (h // 2):LANES * (h // 2 + 1)]
        if (h % 2) != kvh:
            v = pltpu.roll(v, HEAD_DIM, 1)
        blocks.append(jnp.where(keep, v, 0.0))
    return jnp.concatenate(blocks, axis=0)


def _unstack_heads(o_by_kvh, gates, branch, src_half_is_kvh=True):
    tq = gates.shape[0]
    lane = lax.broadcasted_iota(jnp.int32, (tq, LANES), 1)
    blocks = []
    for h in range(ATT_HEADS):
        kvh, g = divmod(h, GQA_GROUP)
        v = o_by_kvh[kvh][g * tq:(g + 1) * tq, :]
        gl = GATE_LANE0 + branch * ATT_HEADS + h
        v = v * gates[:, gl:gl + 1]
        src_half = kvh if src_half_is_kvh else 0
        if (h % 2) != src_half:
            v = pltpu.roll(v, HEAD_DIM, 1)
        blocks.append(v)
    outs = [jnp.where(lane < HEAD_DIM, blocks[2 * j], blocks[2 * j + 1]) for j in range(ATT_HEADS // 2)]
    return jnp.concatenate(outs, axis=1)


def _tile_rows(v, n):
    return jnp.concatenate([v] * n, axis=0)


def _flash_update(s, vaug, m_ref, acc_ref, kvh):
    m_prev = m_ref[kvh]
    m_new = jnp.maximum(m_prev, jnp.max(s, axis=1, keepdims=True))
    alpha = jnp.exp(m_prev - m_new)
    p = jnp.exp(s - m_new[:, 0:1])
    pv = jnp.dot(p.astype(BF16), vaug, preferred_element_type=F32)
    acc_ref[kvh] = acc_ref[kvh] * jnp.concatenate([alpha, alpha], axis=1) + pv
    m_ref[kvh] = m_new


def _flash_update_t(s, vaug_t, m_ref, acc_ref, kvh):
    m_prev = m_ref[kvh]
    m_new = jnp.maximum(m_prev, jnp.max(s, axis=1, keepdims=True))
    alpha = jnp.exp(m_prev - m_new)
    p = jnp.exp(s - m_new[:, 0:1])
    pv = lax.dot_general(p.astype(BF16), vaug_t, (((1,), (1,)), ((), ())), preferred_element_type=F32)
    acc_ref[kvh] = acc_ref[kvh] * jnp.concatenate([alpha, alpha], axis=1) + pv
    m_ref[kvh] = m_new


def _flash_out(acc_ref, kvh):
    acc = acc_ref[kvh]
    return acc[:, :LANES] / acc[:, LANES:]


def _vaug(v01):
    return jnp.concatenate([v01.astype(BF16), jnp.ones(v01.shape, BF16)], axis=1)


def _topk_cols(score_t, rounds):
    nb = score_t.shape[0]
    ridx = lax.broadcasted_iota(jnp.int32, score_t.shape, 0)
    sel = jnp.zeros(score_t.shape, F32)
    cur = score_t
    for _ in range(rounds):
        mx = jnp.max(cur, axis=0, keepdims=True)
        idx = jnp.min(jnp.where(cur == mx, ridx, nb), axis=0, keepdims=True)
        hit = ridx == idx
        sel = jnp.where(hit & (mx > -jnp.inf), 1.0, sel)
        cur = jnp.where(hit, -jnp.inf, cur)
    return sel


def _topk_rows(score, rounds):
    nb = score.shape[1]
    lidx = lax.broadcasted_iota(jnp.int32, score.shape, 1)
    ahead = jnp.zeros(score.shape, jnp.int32)
    for r in range(1, nb):
        other = pltpu.roll(score, r, 1)
        wins = (other > score) | ((other == score) & (lidx >= r))
        ahead = ahead + jnp.where(wins, 1, 0)
    return jnp.where((ahead < rounds) & (score > -jnp.inf), 1.0, 0.0)


def _cmp_attend(q, kc01, vc01, pos):
    tq = q.shape[0]
    ncl = kc01.shape[0]
    rows = GQA_GROUP * tq
    lane = lax.broadcasted_iota(jnp.int32, (rows, ncl), 1)
    cblk = 2 * (lane % MAX_SEL_LANES) + lane // MAX_SEL_LANES
    c_end = (cblk + 1) * CMP_BLOCK - 1
    mask = c_end <= _tile_rows(pos, GQA_GROUP)
    kcb = kc01.astype(BF16)
    vcb = vc01.astype(BF16)
    outs, imps = [], []
    for kvh in range(KV_HEADS):
        qs = _stack_heads(q, kvh).astype(BF16)
        s = lax.dot_general(qs, kcb, (((1,), (1,)), ((), ())), preferred_element_type=F32)
        s = jnp.where(mask, s, -jnp.inf)
        mx = jnp.max(s, axis=1, keepdims=True)
        mx = jnp.where(mx > -jnp.inf, mx, 0.0)
        e = jnp.exp(s - mx)
        dsum = jnp.sum(e, axis=1, keepdims=True)
        p = e / jnp.where(dsum > 0, dsum, 1.0)
        outs.append(jnp.dot(p.astype(BF16), vcb, preferred_element_type=F32))
        imp = p[0:tq]
        for g in range(1, GQA_GROUP):
            imp = imp + p[g * tq:(g + 1) * tq]
        imps.append(imp[:, :MAX_SEL_LANES] + imp[:, MAX_SEL_LANES:])
    return outs, imps


def _cmp_prompt_kernel(q_ref, kc_ref, vc_ref, sm_ref, o_ref, bias_ref):
    tq = q_ref.shape[1]
    i = pl.program_id(1)
    pos = i * tq + lax.broadcasted_iota(jnp.int32, (tq, 1), 0)
    q = q_ref[0] * SCALE
    outs, imps = _cmp_attend(q, kc_ref[0], vc_ref[0], pos)
    gates = jax.nn.sigmoid(sm_ref[0])
    o_ref[0] = _unstack_heads(outs, gates, 0)
    blk = lax.broadcasted_iota(jnp.int32, (tq, MAX_SEL_LANES), 1)
    cur = pos // SEL_BLOCK
    valid = blk <= cur
    forced = (blk == 0) | ((cur - blk >= 0) & (cur - blk < N_LOCAL_BLOCKS))
    for kvh in range(KV_HEADS):
        score = jnp.where(valid & jnp.logical_not(forced), imps[kvh], -jnp.inf)
        sel_t = _topk_cols(score.T, TOP_N - 1 - N_LOCAL_BLOCKS)
        keep_t = jnp.where(forced & valid, 1.0, 0.0).T
        bias_ref[0, kvh] = jnp.where((sel_t > 0.5) | (keep_t > 0.5), 0.0, NEG).astype(BF16)


def _cmp_prompt(q, kcm_perm, sm, tq):
    b, t, _ = q.shape
    ncl = kcm_perm.shape[1]
    return pl.pallas_call(
        _cmp_prompt_kernel,
        grid=(b, t // tq),
        in_specs=[pl.BlockSpec((1, tq, ATT_WIDTH), lambda bb, i: (bb, i, 0)),
                  pl.BlockSpec((1, ncl, LANES), lambda bb, i: (bb, 0, 0)),
                  pl.BlockSpec((1, ncl, LANES), lambda bb, i: (bb, 0, 1)),
                  pl.BlockSpec((1, tq, LANES), lambda bb, i: (bb, i, 0))],
        out_specs=[pl.BlockSpec((1, tq, ATT_WIDTH), lambda bb, i: (bb, i, 0)),
                   pl.BlockSpec((1, KV_HEADS, MAX_SEL_LANES, tq), lambda bb, i: (bb, 0, 0, i))],
        out_shape=[jax.ShapeDtypeStruct((b, t, ATT_WIDTH), F32),
                   jax.ShapeDtypeStruct((b, KV_HEADS, MAX_SEL_LANES, t), BF16)],
        compiler_params=_cparams(("arbitrary", "arbitrary")),
        name="cmp_prompt",
    )(q, kcm_perm, kcm_perm, sm)


def _build_qaug(q, bias_ref, qaug_ref):
    for kvh in range(KV_HEADS):
        qs = _stack_heads(q, kvh).astype(BF16)
        bias = _tile_rows(bias_ref[0, kvh], GQA_GROUP)
        qaug_ref[kvh] = jnp.concatenate([qs, bias], axis=1)


def _head_rows(qt, h):
    blk = qt[h * HEAD_DIM:(h + 1) * HEAD_DIM]
    z = jnp.zeros_like(blk)
    return jnp.concatenate([blk, z] if h // GQA_GROUP == 0 else [z, blk], axis=0)


def _emit_heads(o_by_head, sm_ref, branch, o_ref):
    gates_t = jax.nn.sigmoid(sm_ref[0]).T
    cols = []
    for j in range(ATT_HEADS // 2):
        pair = []
        for h in (2 * j, 2 * j + 1):
            gl = GATE_LANE0 + branch * ATT_HEADS + h
            pair.append(o_by_head[h] * gates_t[gl:gl + 1, :])
        cols.append(jnp.concatenate(pair, axis=0).T)
    o_ref[0] = jnp.concatenate(cols, axis=1)


def _pipeline_order(n, lead):
    order = [("qk", h) for h in range(min(lead, n))]
    for h in range(n):
        order.append(("sm", h))
        if h + lead < n:
            order.append(("qk", h + lead))
        if h >= 1:
            order.append(("pv", h - 1))
    order.append(("pv", n - 1))
    return order


_SLC_ORDER = _pipeline_order(ATT_HEADS, 8)
_SLC_TILES_PER_ITER = 2


def _slc_prompt_kernel(q_ref, bias_ref, ka_ref, vat_ref, sm_ref, o_ref, qa_ref, m_ref, acc_ref):
    tq = q_ref.shape[1]
    tk = tq
    i = pl.program_id(1)
    qt = (q_ref[0] * SCALE_LOG2).T
    for h in range(ATT_HEADS):
        qa_ref[h] = jnp.concatenate([bias_ref[0, h // GQA_GROUP], _head_rows(qt, h).astype(BF16)], axis=0)
    m_ref[...] = jnp.full(m_ref.shape, NEG, F32)
    acc_ref[...] = jnp.zeros(acc_ref.shape, F32)

    def tiles(j0, n, masked):
        k0s = [pl.multiple_of((j0 + t) * tk, tk) for t in range(n)]
        kas = [ka_ref[0, pl.ds(k0, tk), :] for k0 in k0s]
        ss, ps, alphas = {}, {}, {}

        def qk(h):
            ss[h] = []
            for ka in kas:
                s = jnp.dot(ka, qa_ref[h], preferred_element_type=F32)
                if masked:
                    kofs = lax.broadcasted_iota(jnp.int32, (tk, tq), 0)
                    qofs = lax.broadcasted_iota(jnp.int32, (tk, tq), 1)
                    s = jnp.where(kofs <= qofs, s, NEG)
                ss[h].append(s)

        def sm(h):
            m_prev = m_ref[h]
            m_new = m_prev
            for s in ss[h]:
                m_new = jnp.maximum(m_new, jnp.max(s, axis=0, keepdims=True))
            alphas[h] = jnp.exp2(m_prev - m_new)
            ps[h] = [jnp.exp2(s - m_new).astype(BF16) for s in ss[h]]
            m_ref[h] = m_new

        def pv(h):
            acc = acc_ref[h] * alphas[h]
            for k0, p in zip(k0s, ps[h]):
                acc = acc + jnp.dot(vat_ref[0, h // GQA_GROUP, :, pl.ds(k0, tk)], p, preferred_element_type=F32)
            acc_ref[h] = acc

        for step in _SLC_ORDER:
            {"qk": qk, "sm": sm, "pv": pv}[step[0]](step[1])

    def body(jq, carry):
        tiles(_SLC_TILES_PER_ITER * jq, _SLC_TILES_PER_ITER, False)
        return carry

    lax.fori_loop(0, i // _SLC_TILES_PER_ITER, body, 0)
    done = (i // _SLC_TILES_PER_ITER) * _SLC_TILES_PER_ITER
    n = _SLC_TILES_PER_ITER // 2
    while n >= 1:
        take = ((i - done) // n) % 2 == 1

        @pl.when(take)
        def _(done=done, n=n):
            tiles(done, n, False)

        done = done + jnp.where(take, n, 0)
        n //= 2

    tiles(i, 1, True)
    outs = []
    for h in range(ATT_HEADS):
        acc = acc_ref[h]
        outs.append(acc[:HEAD_DIM] / acc[HEAD_DIM:HEAD_DIM + 1])
    _emit_heads(outs, sm_ref, 1, o_ref)


def _win_prompt_kernel(q_ref, sm_ref, *refs, n_prev):
    nt = n_prev + 1
    k_refs, vat_refs, o_ref = refs[:nt], refs[nt:2 * nt], refs[2 * nt]
    tq = q_ref.shape[1]
    i = pl.program_id(1)
    kofs = lax.broadcasted_iota(jnp.int32, (tq, tq), 0)
    qofs = lax.broadcasted_iota(jnp.int32, (tq, tq), 1)
    qt = (q_ref[0] * SCALE_LOG2).T
    scores = []
    for h in range(ATT_HEADS):
        qh = _head_rows(qt, h).astype(BF16)
        ss = []
        for n in range(nt):
            s = jnp.dot(k_refs[n][0], qh, preferred_element_type=F32)
            back = n_prev - n
            ok = i >= back
            if n == 0:
                ok = ok & (kofs >= qofs)
            if back == 0:
                ok = kofs <= qofs
            ss.append(jnp.where(ok, s, NEG))
        scores.append(ss)
    probs = []
    for ss in scores:
        mx = ss[0].max(axis=0, keepdims=True)
        for s in ss[1:]:
            mx = jnp.maximum(mx, s.max(axis=0, keepdims=True))
        probs.append([jnp.exp2(s - mx).astype(BF16) for s in ss])
    outs = []
    for h in range(ATT_HEADS):
        acc = None
        for n in range(nt):
            pv = jnp.dot(vat_refs[n][0, h // GQA_GROUP], probs[h][n], preferred_element_type=F32)
            acc = pv if acc is None else acc + pv
        outs.append(acc[:HEAD_DIM] / acc[HEAD_DIM:HEAD_DIM + 1])
    _emit_heads(outs, sm_ref, 2, o_ref)


def _slc_win_prompt_kernel(q_ref, bias_ref, ka_ref, vat_ref, sm_ref, *refs, n_prev):
    nt = n_prev + 1
    win_in = refs[:2 * nt]
    os_ref, ow_ref, qa_ref, m_ref, acc_ref = refs[2 * nt:]
    _win_prompt_kernel(q_ref, sm_ref, *win_in, ow_ref, n_prev=n_prev)
    _slc_prompt_kernel(q_ref, bias_ref, ka_ref, vat_ref, sm_ref, os_ref, qa_ref, m_ref, acc_ref)


def _slc_win_prompt(q, bias_t, ka, vat, sm, kw, vat_w, tq):
    b, t, _ = q.shape
    assert WINDOW % tq == 0 and WINDOW >= tq
    n_prev = WINDOW // tq
    kc = ka.shape[2]

    def k_spec(back):
        return pl.BlockSpec((1, tq, KV_WIDTH), lambda bb, i: (bb, jnp.maximum(i - back, 0), 0))

    def v_spec(back):
        return pl.BlockSpec((1, KV_HEADS, VAT_ROWS, tq), lambda bb, i: (bb, 0, 0, jnp.maximum(i - back, 0)))

    backs = [n_prev - n for n in range(n_prev + 1)]
    nt = n_prev + 1
    o_spec = pl.BlockSpec((1, tq, ATT_WIDTH), lambda bb, i: (bb, i, 0))
    o_shape = jax.ShapeDtypeStruct((b, t, ATT_WIDTH), F32)
    return pl.pallas_call(
        functools.partial(_slc_win_prompt_kernel, n_prev=n_prev),
        grid=(b, t // tq),
        in_specs=([pl.BlockSpec((1, tq, ATT_WIDTH), lambda bb, i: (bb, i, 0)),
                   pl.BlockSpec((1, KV_HEADS, MAX_SEL_LANES, tq), lambda bb, i: (bb, 0, 0, i)),
                   pl.BlockSpec((1, t, kc), lambda bb, i: (bb, 0, 0)),
                   pl.BlockSpec((1, KV_HEADS, VAT_ROWS, t), lambda bb, i: (bb, 0, 0, 0)),
                   pl.BlockSpec((1, tq, LANES), lambda bb, i: (bb, i, 0))]
                  + [k_spec(bk) for bk in backs] + [v_spec(bk) for bk in backs]),
        out_specs=[o_spec, o_spec],
        out_shape=[o_shape, o_shape],
        scratch_shapes=[pltpu.VMEM((ATT_HEADS, kc, tq), BF16),
                        pltpu.VMEM((ATT_HEADS, 1, tq), F32),
                        pltpu.VMEM((ATT_HEADS, VAT_ROWS, tq), F32)],
        compiler_params=_cparams(("arbitrary", "arbitrary")),
        name="slc_win_prompt",
    )(q, bias_t, ka, vat, sm, *([kw] * nt), *([vat_w] * nt))


CMP_PAGES = LANES * CMP_BLOCK // PAGE_SIZE


def _cmp_local_block(lane):
    half = LANES // 2
    return 2 * (lane % half) + lane // half


def _block_mean_matrix(n_pages):
    tok = jnp.arange(n_pages * PAGE_SIZE, dtype=jnp.int32)[:, None] // CMP_BLOCK
    col = jnp.arange(n_pages * PAGE_SIZE // CMP_BLOCK, dtype=jnp.int32)[None, :]
    blk = LANES * (col // LANES) + _cmp_local_block(col % LANES)
    return jnp.where(tok == blk, 1.0 / CMP_BLOCK, 0.0).astype(BF16)


def _cmp_sample_kernel(pt_ref, *refs, n_pg, past):
    page_refs = refs[:n_pg]
    a_ref, q_ref, sm_ref, o_ref, bias_ref, kct_ref = refs[n_pg:]
    s = pl.program_id(1)
    ns = pl.num_programs(1)
    n_chunks = kct_ref.shape[0]
    tn = q_ref.shape[1]
    rows = GQA_GROUP * tn

    x = jnp.concatenate([r[0] for r in page_refs], axis=1)
    hi = x.astype(BF16)
    lo = (x - hi.astype(F32)).astype(BF16)
    a = a_ref[...]
    means = jnp.dot(hi, a, preferred_element_type=F32) + jnp.dot(lo, a, preferred_element_type=F32)
    for c in range(n_chunks):
        kct_ref[c] = means[:, c * LANES:(c + 1) * LANES]

    @pl.when(s == ns - 1)
    def _():
        pos = past + lax.broadcasted_iota(jnp.int32, (tn, 1), 0)
        lane = lax.broadcasted_iota(jnp.int32, (rows, LANES), 1)
        q = q_ref[0] * SCALE
        outs, imps = [], []
        for kvh in range(KV_HEADS):
            qs = _stack_heads(q, kvh).astype(BF16)
            ss = []
            for c in range(n_chunks):
                sc = jnp.dot(qs, kct_ref[c, :KV_WIDTH, :].astype(BF16), preferred_element_type=F32)
                c_end = (c * LANES + _cmp_local_block(lane) + 1) * CMP_BLOCK - 1
                ss.append(jnp.where(c_end <= _tile_rows(pos, GQA_GROUP), sc, -jnp.inf))
            mx = ss[0].max(axis=1, keepdims=True)
            for sc in ss[1:]:
                mx = jnp.maximum(mx, sc.max(axis=1, keepdims=True))
            mx = jnp.where(mx > -jnp.inf, mx, 0.0)
            es = [jnp.exp(sc - mx) for sc in ss]
            dsum = es[0].sum(axis=1, keepdims=True)
            for e in es[1:]:
                dsum = dsum + e.sum(axis=1, keepdims=True)
            inv = 1.0 / jnp.where(dsum > 0, dsum, 1.0)
            o = None
            imp_blocks = []
            for c in range(n_chunks):
                p = es[c] * inv
                vc = kct_ref[c, KV_WIDTH:, :].astype(BF16)
                pv = lax.dot_general(p.astype(BF16), vc, (((1,), (1,)), ((), ())), preferred_element_type=F32)
                o = pv if o is None else o + pv
                imp = p[0:tn]
                for g in range(1, GQA_GROUP):
                    imp = imp + p[g * tn:(g + 1) * tn]
                imp_blocks.append(imp + pltpu.roll(imp, LANES // 2, 1))
            outs.append(o)
            if n_chunks == 1:
                imps.append(imp_blocks[0])
            else:
                lane_t = lax.broadcasted_iota(jnp.int32, (tn, LANES), 1)
                imps.append(jnp.where(lane_t < LANES // 2, imp_blocks[0], pltpu.roll(imp_blocks[1], LANES // 2, 1)))
        o_ref[0] = _unstack_heads(outs, jax.nn.sigmoid(sm_ref[0]), 0)
        n_past = past // SEL_BLOCK
        blk = lax.broadcasted_iota(jnp.int32, (tn, MAX_SEL_LANES), 1)
        forced = (blk == 0) | (blk == n_past - 1)
        rounds = min(TOP_N, n_past + 1) - 1
        for kvh in range(KV_HEADS):
            score = jnp.where(forced, jnp.inf, imps[kvh])
            score = jnp.where(blk < n_past, score, -jnp.inf)
            sel = _topk_rows(score, rounds)
            bias_ref[0, kvh] = jnp.where(sel > 0.5, 0.0, NEG).astype(BF16)


def _page_specs(n_pg):
    def spec(k):
        return pl.BlockSpec((1, 2 * KV_WIDTH, PAGE_SIZE), lambda b, s, pt: (pt[b * n_pg + k], 0, 0))
    return [spec(k) for k in range(n_pg)]


def _cmp_sample(page_table, pool_t, q, sm):
    bs, tn, _ = q.shape
    n_pages = page_table.shape[1]
    past = n_pages * PAGE_SIZE
    n_pg = n_pages
    assert n_pages % CMP_PAGES == 0 and n_pages // CMP_PAGES <= 2
    n_chunks = n_pages // CMP_PAGES
    grid_spec = pltpu.PrefetchScalarGridSpec(
        num_scalar_prefetch=1,
        grid=(bs, 1),
        in_specs=_page_specs(n_pg) + [
            pl.BlockSpec((n_pg * PAGE_SIZE, n_chunks * LANES), lambda b, s, pt: (0, 0)),
            pl.BlockSpec((1, tn, ATT_WIDTH), lambda b, s, pt: (b, 0, 0)),
            pl.BlockSpec((1, tn, LANES), lambda b, s, pt: (b, 0, 0))],
        out_specs=[pl.BlockSpec((1, tn, ATT_WIDTH), lambda b, s, pt: (b, 0, 0)),
                   pl.BlockSpec((1, KV_HEADS, tn, LANES), lambda b, s, pt: (b, 0, 0, 0))],
        scratch_shapes=[pltpu.VMEM((n_chunks, 2 * KV_WIDTH, LANES), F32)],
    )
    return pl.pallas_call(
        functools.partial(_cmp_sample_kernel, n_pg=n_pg, past=past),
        grid_spec=grid_spec,
        out_shape=[jax.ShapeDtypeStruct((bs, tn, ATT_WIDTH), F32),
                   jax.ShapeDtypeStruct((bs, KV_HEADS, tn, LANES), BF16)],
        compiler_params=_cparams(("arbitrary", "arbitrary")),
        name="cmp_sample",
    )(page_table.reshape(-1), *([pool_t] * n_pg), _block_mean_matrix(n_pages), q, sm)


def _slc_sample_kernel(pt_ref, *refs, n_pg, n_seq, past):
    page_refs = [refs[u * n_pg:(u + 1) * n_pg] for u in range(n_seq)]
    q_ref, bias_ref, new_ref, sm_ref, o_ref, qaug_ref, m_ref, acc_ref = refs[n_seq * n_pg:]
    s = pl.program_id(1)
    ns = pl.num_programs(1)
    tn = q_ref.shape[1]
    rows = GQA_GROUP * tn

    nq = KV_HEADS * rows

    @pl.when(s == 0)
    def _():
        for u in range(n_seq):
            q = q_ref[u] * SCALE
            blocks = [jnp.concatenate([_stack_heads(q, kvh).astype(BF16),
                                       _tile_rows(bias_ref[u, kvh], GQA_GROUP)], axis=1)
                      for kvh in range(KV_HEADS)]
            qaug_ref[u] = jnp.concatenate(blocks + [jnp.zeros((LANES - nq, 2 * LANES), BF16)], axis=0)
        m_ref[...] = jnp.full(m_ref.shape, NEG, F32)
        acc_ref[...] = jnp.zeros(acc_ref.shape, F32)

    def update(sts, vaugs):
        m_news, alphas, ps = [], [], []
        for u in range(n_seq):
            m_prev = m_ref[u]
            m_new = m_prev
            for st in sts[u]:
                m_new = jnp.maximum(m_new, jnp.max(st, axis=0, keepdims=True))
            m_news.append(m_new)
            alphas.append(jnp.exp(m_prev - m_new))
            ps.append([jnp.exp(st - m_new).astype(BF16) for st in sts[u]])
        for u in range(n_seq):
            acc = acc_ref[u] * alphas[u]
            for p, va in zip(ps[u], vaugs[u]):
                acc = acc + jnp.dot(va, p, preferred_element_type=F32)
            acc_ref[u] = acc
            m_ref[u] = m_news[u]

    tk = n_pg * PAGE_SIZE
    kblk = (s * tk + lax.broadcasted_iota(jnp.int32, (MAX_SEL_LANES, tk), 1)) // SEL_BLOCK
    onehot = jnp.where(kblk == lax.broadcasted_iota(jnp.int32, (MAX_SEL_LANES, tk), 0), 1.0, 0.0).astype(BF16)
    n_piece = 2
    w = tk // n_piece
    sts, vaugs = [], []
    for u in range(n_seq):
        x = jnp.concatenate([r[0] for r in page_refs[u]], axis=1)
        kaug_t = jnp.concatenate([x[:KV_WIDTH].astype(BF16), onehot], axis=0)
        vaug_t = jnp.concatenate([x[KV_WIDTH:].astype(BF16), jnp.ones((BF16_SUBLANES, tk), BF16)], axis=0)
        qa = qaug_ref[u]
        sts.append([lax.dot_general(kaug_t[:, i * w:(i + 1) * w], qa, (((0,), (1,)), ((), ())),
                                    preferred_element_type=F32) for i in range(n_piece)])
        vaugs.append([vaug_t[:, i * w:(i + 1) * w] for i in range(n_piece)])
    update(sts, vaugs)

    @pl.when(s == ns - 1)
    def _():
        nk = new_ref.shape[1]
        kidx = lax.broadcasted_iota(jnp.int32, (nk, LANES), 0)
        qidx = lax.broadcasted_iota(jnp.int32, (nk, LANES), 1) % tn
        sts, vaugs = [], []
        for u in range(n_seq):
            xn = new_ref[u]
            st = lax.dot_general(xn[:, :LANES].astype(BF16), qaug_ref[u, :, :LANES], (((1,), (1,)), ((), ())),
                                 preferred_element_type=F32)
            sts.append([jnp.where(kidx <= qidx, st, NEG)])
            vaugs.append([jnp.concatenate([xn[:, LANES:].T.astype(BF16), jnp.ones((BF16_SUBLANES, nk), BF16)],
                                          axis=0)])
        update(sts, vaugs)
        for u in range(n_seq):
            acc = acc_ref[u]
            o_t = (acc[:KV_WIDTH] / acc[KV_WIDTH:KV_WIDTH + 1]).T
            outs = [o_t[kvh * rows:(kvh + 1) * rows] for kvh in range(KV_HEADS)]
            o_ref[u] = _unstack_heads(outs, jax.nn.sigmoid(sm_ref[u]), 1)


def _slc_sample(page_table, pool, q, bias, new_pad, sm, n_pg):
    bs, tn, _ = q.shape
    n_pages = page_table.shape[1]
    past = n_pages * PAGE_SIZE
    rows = GQA_GROUP * tn
    npad = new_pad.shape[1]
    n_seq = 4
    assert bs % n_seq == 0 and KV_HEADS * rows <= LANES

    def page_spec(u, k):
        return pl.BlockSpec((1, 2 * KV_WIDTH, PAGE_SIZE),
                            lambda b, s, pt: (pt[(n_seq * b + u) * n_pages + s * n_pg + k], 0, 0))

    grid_spec = pltpu.PrefetchScalarGridSpec(
        num_scalar_prefetch=1,
        grid=(bs // n_seq, n_pages // n_pg),
        in_specs=[page_spec(u, k) for u in range(n_seq) for k in range(n_pg)] + [
            pl.BlockSpec((n_seq, tn, ATT_WIDTH), lambda b, s, pt: (b, 0, 0)),
            pl.BlockSpec((n_seq, KV_HEADS, tn, LANES), lambda b, s, pt: (b, 0, 0, 0)),
            pl.BlockSpec((n_seq, npad, 2 * KV_WIDTH), lambda b, s, pt: (b, 0, 0)),
            pl.BlockSpec((n_seq, tn, LANES), lambda b, s, pt: (b, 0, 0))],
        out_specs=pl.BlockSpec((n_seq, tn, ATT_WIDTH), lambda b, s, pt: (b, 0, 0)),
        scratch_shapes=[pltpu.VMEM((n_seq, LANES, 2 * LANES), BF16),
                        pltpu.VMEM((n_seq, 1, LANES), F32),
                        pltpu.VMEM((n_seq, KV_WIDTH + BF16_SUBLANES, LANES), F32)],
    )
    return pl.pallas_call(
        functools.partial(_slc_sample_kernel, n_pg=n_pg, n_seq=n_seq, past=past),
        grid_spec=grid_spec,
        out_shape=jax.ShapeDtypeStruct((bs, tn, ATT_WIDTH), F32),
        compiler_params=_cparams(("arbitrary", "arbitrary")),
        name="slc_sample",
    )(page_table.reshape(-1), *([pool] * (n_seq * n_pg)), q, bias, new_pad, sm)


def _win_sample_kernel(q_ref, wint_ref, new_ref, sm_ref, o_ref, *, past):
    nb, tn = q_ref.shape[0], q_ref.shape[1]
    rows = GQA_GROUP * tn
    wb = wint_ref.shape[2]
    npad = new_ref.shape[1]
    kidx = lax.broadcasted_iota(jnp.int32, (rows, wb), 1)
    t_w = lax.broadcasted_iota(jnp.int32, (rows, wb), 0) % tn
    ok_w = (wb + t_w - kidx <= WINDOW) & (past - wb + kidx >= 0)
    ok_n = (lax.broadcasted_iota(jnp.int32, (rows, npad), 1)
            <= lax.broadcasted_iota(jnp.int32, (rows, npad), 0) % tn)
    chains = [(sq, kvh) for sq in range(nb) for kvh in range(KV_HEADS)]
    scores, probs = {}, {}
    for sq, kvh in chains:
        qs = _stack_heads(q_ref[sq] * SCALE, kvh).astype(BF16)
        kt = wint_ref[sq, :KV_WIDTH, :].astype(BF16)
        kn = new_ref[sq, :, :KV_WIDTH].astype(BF16)
        s_w = jnp.where(ok_w, jnp.dot(qs, kt, preferred_element_type=F32), -jnp.inf)
        s_n = jnp.where(ok_n, lax.dot_general(qs, kn, (((1,), (1,)), ((), ())), preferred_element_type=F32),
                        -jnp.inf)
        scores[sq, kvh] = (s_w, s_n)
    for key in chains:
        s_w, s_n = scores[key]
        mx = jnp.maximum(jnp.max(s_w, axis=1, keepdims=True), jnp.max(s_n, axis=1, keepdims=True))
        e_w, e_n = jnp.exp(s_w - mx), jnp.exp(s_n - mx)
        inv = 1.0 / (jnp.sum(e_w, axis=1, keepdims=True) + jnp.sum(e_n, axis=1, keepdims=True))
        probs[key] = ((e_w * inv).astype(BF16), (e_n * inv).astype(BF16))
    for sq in range(nb):
        vt = wint_ref[sq, KV_WIDTH:, :].astype(BF16)
        vn = new_ref[sq, :, KV_WIDTH:].astype(BF16)
        outs = []
        for kvh in range(KV_HEADS):
            p_w, p_n = probs[sq, kvh]
            outs.append(lax.dot_general(p_w, vt, (((1,), (1,)), ((), ())), preferred_element_type=F32)
                        + jnp.dot(p_n, vn, preferred_element_type=F32))
        o_ref[sq] = _unstack_heads(outs, jax.nn.sigmoid(sm_ref[sq]), 2)


def _win_sample(q, win_t, new_pad, sm, past, nb):
    bs, tn, _ = q.shape
    wb = win_t.shape[2]
    npad = new_pad.shape[1]
    assert bs % nb == 0 and tn <= WINDOW
    return pl.pallas_call(
        functools.partial(_win_sample_kernel, past=past),
        grid=(bs // nb,),
        in_specs=[pl.BlockSpec((nb, tn, ATT_WIDTH), lambda b: (b, 0, 0)),
                  pl.BlockSpec((nb, 2 * KV_WIDTH, wb), lambda b: (b, 0, 0)),
                  pl.BlockSpec((nb, npad, 2 * KV_WIDTH), lambda b: (b, 0, 0)),
                  pl.BlockSpec((nb, tn, LANES), lambda b: (b, 0, 0))],
        out_specs=pl.BlockSpec((nb, tn, ATT_WIDTH), lambda b: (b, 0, 0)),
        out_shape=jax.ShapeDtypeStruct((bs, tn, ATT_WIDTH), F32),
        compiler_params=_cparams(("arbitrary",)),
        name="win_sample",
    )(q, win_t, new_pad, sm)


def _out_kernel(x_ref, oc_ref, os_ref, ow_ref, za_ref, ys_ref, gate_ref, ang_ref, wo_ref, fg_ref, y_ref):
    nbk, tt, d = x_ref.shape
    m = nbk * tt
    o = (oc_ref[...] + os_ref[...]) + ow_ref[...]
    t = o * _silu(za_ref[...])
    ms = jnp.mean(t * t, axis=-1, keepdims=True)
    y_att = (t * lax.rsqrt(ms + EPS)) * ang_ref[...].reshape(1, 1, ATT_WIDTH)
    ya = y_att.reshape(m, ATT_WIDTH).astype(BF16)
    ys = ys_ref[...].reshape(m, SSD_WIDTH)
    mix = (jnp.dot(ys, wo_ref[:SSD_WIDTH, :], preferred_element_type=F32)
           + jnp.dot(ya, wo_ref[SSD_WIDTH:, :], preferred_element_type=F32))
    xp = x_ref[...] + gate_ref[...] * mix.reshape(nbk, tt, d)
    ms2 = jnp.mean(xp * xp, axis=-1, keepdims=True)
    y_ref[...] = (xp * lax.rsqrt(ms2 + EPS)) * fg_ref[...].reshape(1, 1, d)


def _out(x3, o_c, o_s, o_w, za, y_ssd, gate, att_norm_g, w_out_b, final_g, *, nbk, groups_per_mod):
    g_total, tt, d = x3.shape
    steps = g_total // nbk
    if groups_per_mod is None:
        mod_spec = pl.BlockSpec((nbk, 1, d), lambda g: (g, 0, 0))
    else:
        mod_spec = pl.BlockSpec((1, 1, d), lambda g: (g // groups_per_mod, 0, 0))

    def tok_spec(c):
        return pl.BlockSpec((nbk, tt, c), lambda g: (g, 0, 0))

    return pl.pallas_call(
        _out_kernel,
        grid=(steps,),
        in_specs=[tok_spec(d), tok_spec(ATT_WIDTH), tok_spec(ATT_WIDTH), tok_spec(ATT_WIDTH),
                  tok_spec(ATT_WIDTH), tok_spec(SSD_WIDTH), mod_spec,
                  pl.BlockSpec((1, ATT_WIDTH), lambda g: (0, 0)),
                  pl.BlockSpec((SSD_WIDTH + ATT_WIDTH, d), lambda g: (0, 0)),
                  pl.BlockSpec((1, d), lambda g: (0, 0))],
        out_specs=tok_spec(d),
        out_shape=jax.ShapeDtypeStruct((g_total, tt, d), F32),
        compiler_params=_cparams(("arbitrary",)),
        name="outproj",
    )(x3, o_c, o_s, o_w, za, y_ssd, gate, att_norm_g.reshape(1, ATT_WIDTH), w_out_b, final_g.reshape(1, d))


def _perm_cmp_means(kcm, b):
    nc = kcm.shape[1]
    ns = nc // 2
    assert ns <= MAX_SEL_LANES
    eo = kcm.reshape(b, ns, 2, 2 * KV_WIDTH).transpose(0, 2, 1, 3)
    eo = jnp.pad(eo, ((0, 0), (0, 0), (0, MAX_SEL_LANES - ns), (0, 0)))
    return eo.reshape(b, 2 * MAX_SEL_LANES, 2 * KV_WIDTH)


def _prompt_layer(x, mod, lw, final_g, apply_final):
    b, t, d = x.shape
    shift, scale, gate = (mod[:, None, i * d:(i + 1) * d] for i in range(3))
    tt = min(SSD_CHUNK, t)
    assert t % tt == 0 and tt % CMP_BLOCK == 0 and t >= WINDOW
    gpb = t // tt
    x3 = x.reshape(b * gpb, tt, d)
    ssd_params = _ssd_params(lw["conv_w"], lw["conv_b"], lw["dt_bias"], lw["a_log"], lw["d_skip"],
                             lw["ssd_norm_g"])
    q, za, sm, kcm, kvt_c, kvt_s, kvt_w, ka_s, vat_s, kw, vat_w, y_ssd, ssm_new, xlast = _inproj_prompt(
        x, scale, shift, lw["norm_g"], lw["w_big"], lw["w_kvt"], ssd_params, tt)
    r = lambda a: a.reshape(b, t, a.shape[-1])
    q, za, sm, ka_s, kw, y_ssd = map(r, (q, za, sm, ka_s, kw, y_ssd))
    kcm = kcm.reshape(b, t // CMP_BLOCK, 2 * KV_WIDTH)
    ssm_new = ssm_new.reshape(b, SSD_HEADS, HEAD_DIM, D_STATE)

    o_c, bias = _cmp_prompt(q, _perm_cmp_means(kcm, b), sm, 128)
    tq = 256
    o_s, o_w = _slc_win_prompt(q, bias, ka_s, vat_s, sm, kw, vat_w, tq)

    g3 = lambda a: a.reshape(b * gpb, tt, a.shape[-1])
    y3 = _out(x3, g3(o_c), g3(o_s), g3(o_w), g3(za), g3(y_ssd), gate, lw["att_norm_g"], lw["w_out_b"],
              final_g, nbk=1, groups_per_mod=gpb)
    assert apply_final
    kv6 = lambda a: a.reshape(b, 2, KV_HEADS, HEAD_DIM, a.shape[-1]).transpose(0, 4, 1, 2, 3)
    conv_new = xlast[:, xlast.shape[1] - (CONV_WIDTH - 1):]
    outs = (kv6(kvt_c), kv6(kvt_s), kv6(kvt_w[:, :, t - min(WINDOW, t):]), conv_new, ssm_new)
    return y3.reshape(b, t, d), outs


def _sample_layer(x, mod, lw, final_g, pool_c, pool_s, win_buf, conv_buf, ssm, page_table, apply_final):
    bs, tn, d = x.shape
    shift, scale, gate = (mod[:, None, i * d:(i + 1) * d] for i in range(3))
    n_pages = page_table.shape[1]
    past = n_pages * PAGE_SIZE
    nbk = 16
    n_pg = 16
    assert bs % nbk == 0 and tn % 8 == 0 and tn <= SEL_BLOCK and n_pages % n_pg == 0
    assert past // SEL_BLOCK <= MAX_SEL_LANES and past % SEL_BLOCK == 0
    pos = past + jnp.arange(tn, dtype=jnp.int32)
    tabs = tuple(jnp.tile(tb, (nbk, 1)) for tb in _rope_tables(pos))
    zs, xbc, q, za, sm, kvc, kvs, kvw = _inproj_sample(
        x, scale, shift, lw["norm_g"], lw["w_big"], tabs, nbk=nbk)

    y_ssd, ssm_new = _ssd(xbc, sm, zs, conv_buf, ssm, lw["conv_w"], lw["conv_b"], lw["dt_bias"], lw["a_log"],
                          lw["d_skip"], lw["ssd_norm_g"])

    npad = LANES
    to_rows = lambda p: p.transpose(0, 2, 3, 4, 1).reshape(p.shape[0], 2 * KV_WIDTH, p.shape[1])
    o_c, bias = _cmp_sample(page_table, to_rows(pool_c), q, sm)
    kvs_pad = jnp.pad(kvs, ((0, 0), (0, npad - tn), (0, 0)))
    o_s = _slc_sample(page_table, to_rows(pool_s), q, bias, kvs_pad, sm, n_pg)
    wb = win_buf.shape[1]
    win_t = to_rows(win_buf)
    kvw_pad = jnp.pad(kvw, ((0, 0), (0, npad - tn), (0, 0)))
    o_w = _win_sample(q, win_t, kvw_pad, sm, past, 8)

    y = _out(x, o_c, o_s, o_w, za, y_ssd, gate, lw["att_norm_g"], lw["w_out_b"], final_g,
             nbk=nbk, groups_per_mod=None)
    assert apply_final
    kv6 = lambda a: a.reshape(bs, a.shape[1], 2, KV_HEADS, HEAD_DIM)
    kv_w_all = jnp.concatenate([win_t, jnp.swapaxes(kvw, 1, 2)], axis=2)
    win_new_t = kv_w_all[:, :, kv_w_all.shape[2] - min(WINDOW, past + tn):]
    win_new = win_new_t.reshape(bs, 2, KV_HEADS, HEAD_DIM, win_new_t.shape[2]).transpose(0, 4, 1, 2, 3)
    conv_new = jnp.concatenate([conv_buf, xbc], axis=1)[:, tn:]
    outs = (kv6(kvc), kv6(kvs), win_new, conv_new, ssm_new)
    return y, outs


def kernel(x_prompt, x_sample, cache_cmp_kv, cache_slc_kv, state_win_kv, state_conv, state_ssm, page_table,
           c_prompt, c_sample, w_ada, b_ada, norm_g, w_in, conv_w, conv_b, dt_bias, a_log, d_skip,
           ssd_norm_g, att_norm_g, w_out, final_g):
    depth = w_ada.shape[0]
    assert depth == 1
    n_prompt = c_prompt.shape[0]
    xp, xs = x_prompt, x_sample
    out_p, out_s = [], []
    for l in range(depth):
        w_big, w_kvt = _rearrange_w_in(w_in[l])
        lw = dict(norm_g=norm_g[l], w_big=w_big, w_kvt=w_kvt, conv_w=conv_w[l], conv_b=conv_b[l],
                  dt_bias=dt_bias[l], a_log=a_log[l], d_skip=d_skip[l], ssd_norm_g=ssd_norm_g[l],
                  att_norm_g=att_norm_g[l], w_out_b=w_out[l].astype(BF16))
        mod = _mod(jnp.concatenate([c_prompt, c_sample], axis=0), w_ada[l], b_ada[l])
        last = l == depth - 1
        xp, op = _prompt_layer(xp, mod[:n_prompt], lw, final_g, last)
        xs, os_ = _sample_layer(xs, mod[n_prompt:], lw, final_g, cache_cmp_kv[l], cache_slc_kv[l],
                                state_win_kv[l], state_conv[l], state_ssm[l], page_table, last)
        out_p.append(op)
        out_s.append(os_)
    sp = [jnp.stack([o[k] for o in out_p]) for k in range(5)]
    sd = [jnp.stack([o[k] for o in out_s]) for k in range(5)]
    return (xp, xs, sp[0], sp[1], sp[2], sp[3], sp[4], sd[0], sd[1], sd[2], sd[3], sd[4])
```
